```python
import math
import jax
import jax.numpy as jnp
from jax import lax
import numpy as np

D_MODEL = 1024
BATCH = 4
SEQ = 4096
DEPTH = 1
DEC_BATCH = 128
DEC_SEQ = 4
PAST_LEN = 2048
PAGE_SIZE = 128

EPS = 1e-6
GLA_HEADS = 4
GLA_KDIM = D_MODEL // 2
GLA_VDIM = D_MODEL
GLA_DK = GLA_KDIM // GLA_HEADS
GLA_DV = GLA_VDIM // GLA_HEADS
GLA_RANK = 16
GLA_NORMALIZER = 16.0
GLA_CHUNK = 64
DIFF_HEADS = 8
DIFF_DH = D_MODEL // (2 * DIFF_HEADS)
DIFF_QK_HEADS = 2 * DIFF_HEADS
DIFF_QK_DIM = DIFF_QK_HEADS * DIFF_DH
DIFF_VDIM = DIFF_HEADS * 2 * DIFF_DH
ROPE_THETA = 10000.0
Q_BLOCK = 128
N_EXPERTS = 32
TOP_K = 4
D_EXPERT = D_MODEL
SWIGLU_ALPHA = 1.702
SWIGLU_LIMIT = 7.0
MOE_BLOCK = 128
IN_SPLITS = (GLA_KDIM, GLA_KDIM, GLA_VDIM, GLA_RANK, GLA_VDIM, DIFF_QK_DIM, DIFF_QK_DIM, DIFF_VDIM, D_MODEL, D_MODEL)
IN_COLS = 2 * GLA_KDIM + 2 * GLA_VDIM + GLA_RANK + 2 * DIFF_QK_DIM + DIFF_VDIM + 2 * D_MODEL

kernel_name = 'gla_diffattn_moe_hybrid_step'


def rms_norm(x, g):
    xf = x.astype(jnp.float32)
    y = xf * lax.rsqrt(jnp.mean(xf * xf, axis=-1, keepdims=True) + EPS)
    return (y * g.astype(jnp.float32)).astype(x.dtype)


def rotary(x, pos):
    half = x.shape[-1] // 2
    inv_freq = ROPE_THETA ** (-jnp.arange(half, dtype=jnp.float32) / half)
    ang = pos.astype(jnp.float32)[:, None] * inv_freq[None, :]
    cos = jnp.cos(ang)[:, None, :]
    sin = jnp.sin(ang)[:, None, :]
    xf = x.astype(jnp.float32)
    x1, x2 = xf[..., :half], xf[..., half:]
    return jnp.concatenate([x1 * cos - x2 * sin, x2 * cos + x1 * sin], axis=-1).astype(x.dtype)


def gla_recurrence(q, k, v, log_a, s0):
    b_, t, h, _ = q.shape
    dv = v.shape[-1]
    c = GLA_CHUNK if t % GLA_CHUNK == 0 else t
    n = t // c

    def to_chunks(a):
        return jnp.moveaxis(a.astype(jnp.float32).reshape(b_, n, c, h, a.shape[-1]), (1, 3), (0, 2))

    causal = jnp.tril(jnp.ones((c, c), dtype=bool))

    def step(s, chunk):
        qc, kc, vc, gc = chunk
        bcum = jnp.cumsum(gc, axis=-2)
        q_t = qc * jnp.exp(bcum)
        k_t = kc * jnp.exp(-bcum)
        scores = jnp.where(causal, jnp.einsum('bhid,bhjd->bhij', q_t, k_t), 0.0)
        o = jnp.einsum('bhid,bhde->bhie', q_t, s) + jnp.einsum('bhij,bhje->bhie', scores, vc)
        b_last = bcum[:, :, -1:, :]
        s_new = jnp.exp(b_last[:, :, 0, :])[..., None] * s + jnp.einsum('bhjd,bhje->bhde', kc * jnp.exp(b_last - bcum), vc)
        return s_new, o

    s_fin, o = lax.scan(step, s0.astype(jnp.float32), (to_chunks(q), to_chunks(k), to_chunks(v), to_chunks(log_a)))
    o = jnp.moveaxis(o, (0, 2), (1, 3)).reshape(b_, t, h, dv)
    return o, s_fin


def diff_attend(q, k, v, mask, lam):
    s = jnp.einsum('bqhd,bkhd->bhqk', q, k).astype(jnp.float32)
    p = jax.nn.softmax(jnp.where(mask, s, -jnp.inf), axis=-1)
    b_, h2, tq, sk = p.shape
    p = p.reshape(b_, h2 // 2, 2, tq, sk)
    a = p[:, :, 0] - lam * p[:, :, 1]
    return jnp.einsum('bhqk,bkhe->bqhe', a, v.astype(jnp.float32))


def diff_prompt(q, k, v, lam):
    b_, t, h2, dh = q.shape
    nb = t // Q_BLOCK
    qb = jnp.moveaxis(q.reshape(b_, nb, Q_BLOCK, h2, dh), 1, 0)
    starts = jnp.arange(nb, dtype=jnp.int32) * Q_BLOCK
    k_pos = jnp.arange(t, dtype=jnp.int32)

    def block(args):
        qi, start = args
        q_pos = start + jnp.arange(Q_BLOCK, dtype=jnp.int32)
        return diff_attend(qi, k, v, k_pos[None, :] <= q_pos[:, None], lam)

    o = lax.map(block, (qb, starts))
    return jnp.moveaxis(o, 0, 1).reshape(b_, t, DIFF_HEADS, 2 * DIFF_DH)


def token_mixers(xn, pos, gla_s0, past_k, past_v, layer, w_in, gla_gate_up, gla_gate_bias, gla_norm,
                 lam_q1, lam_k1, lam_q2, lam_k2, diff_subln, w_branch_gla, w_branch_diff, w_out):
    b_, t, _ = xn.shape
    points = [int(p) for p in np.cumsum(IN_SPLITS)[:-1]]
    gq, gk, gv, glr, gr, dq, dk, dv, gate_a, gate_b = jnp.split(xn @ w_in, points, axis=-1)

    def heads(a, h):
        return a.reshape(b_, t, h, -1)

    log_a = jax.nn.log_sigmoid((glr @ gla_gate_up + gla_gate_bias).astype(jnp.float32)) / GLA_NORMALIZER
    o_gla, s_new = gla_recurrence(heads(gq, GLA_HEADS) * (GLA_DK ** -0.5), heads(gk, GLA_HEADS),
                                  heads(gv, GLA_HEADS), heads(log_a, GLA_HEADS), gla_s0)
    o_gla = rms_norm(o_gla, gla_norm) * jax.nn.silu(heads(gr, GLA_HEADS).astype(jnp.float32))
    o_gla = o_gla.reshape(b_, t, GLA_VDIM).astype(xn.dtype)

    q = rotary(heads(dq, DIFF_QK_HEADS), pos) * (DIFF_DH ** -0.5)
    k = rotary(heads(dk, DIFF_QK_HEADS), pos)
    v = heads(dv, DIFF_HEADS)
    lam_init = 0.8 - 0.6 * math.exp(-0.3 * layer)
    lam = (jnp.exp(jnp.sum(lam_q1.astype(jnp.float32) * lam_k1.astype(jnp.float32)))
           - jnp.exp(jnp.sum(lam_q2.astype(jnp.float32) * lam_k2.astype(jnp.float32))) + lam_init)
    if past_k is None:
        o_diff = diff_prompt(q, k, v, lam)
    else:
        past_len = past_k.shape[1]
        k_all = jnp.concatenate([past_k, k.astype(past_k.dtype)], axis=1)
        v_all = jnp.concatenate([past_v, v.astype(past_v.dtype)], axis=1)
        k_pos = jnp.arange(past_len + t, dtype=jnp.int32)
        q_pos = past_len + jnp.arange(t, dtype=jnp.int32)
        o_diff = diff_attend(q, k_all, v_all, k_pos[None, :] <= q_pos[:, None], lam)
    o_diff = (rms_norm(o_diff, diff_subln) * (1.0 - lam_init)).astype(xn.dtype).reshape(b_, t, DIFF_VDIM)

    merged = jax.nn.sigmoid(gate_a) * (o_gla @ w_branch_gla) + jax.nn.sigmoid(gate_b) * (o_diff @ w_branch_diff)
    return merged @ w_out, s_new, k, v


def moe_ffn(x, w_router, b_router, w_up, b_up, w_down, b_down):
    shp = x.shape
    xt = x.reshape(-1, shp[-1])
    n = xt.shape[0]
    logits = (xt @ w_router + b_router).astype(jnp.float32)
    top_val, top_idx = lax.top_k(logits, TOP_K)
    gates = jax.nn.softmax(top_val, axis=-1)
    a = n * TOP_K
    e_flat = top_idx.reshape(a)
    tok_flat = jnp.repeat(jnp.arange(n, dtype=jnp.int32), TOP_K)
    g_flat = gates.reshape(a)
    order = jnp.argsort(e_flat)
    e_sorted = e_flat[order]
    counts = jnp.bincount(e_flat, length=N_EXPERTS)
    padded = (counts + MOE_BLOCK - 1) // MOE_BLOCK * MOE_BLOCK
    pad_end = jnp.cumsum(padded)
    pad_start = pad_end - padded
    start = jnp.cumsum(counts) - counts
    dest = pad_start[e_sorted] + jnp.arange(a, dtype=jnp.int32) - start[e_sorted]
    n_rows = -(-(a + N_EXPERTS * (MOE_BLOCK - 1)) // MOE_BLOCK) * MOE_BLOCK
    n_blocks = n_rows // MOE_BLOCK
    row_tok = jnp.zeros((n_rows,), jnp.int32).at[dest].set(tok_flat[order])
    row_gate = jnp.zeros((n_rows,), jnp.float32).at[dest].set(g_flat[order])
    blk_exp = jnp.minimum(jnp.searchsorted(pad_end, jnp.arange(n_blocks, dtype=jnp.int32) * MOE_BLOCK, side='right'),
                          N_EXPERTS - 1)

    def expert_block(args):
        toks, e = args
        h = (xt[toks] @ w_up[e] + b_up[e]).astype(jnp.float32)
        x_glu = jnp.minimum(h[..., ::2], SWIGLU_LIMIT)
        x_lin = jnp.clip(h[..., 1::2], -SWIGLU_LIMIT, SWIGLU_LIMIT)
        act = x_glu * jax.nn.sigmoid(SWIGLU_ALPHA * x_glu) * (x_lin + 1.0)
        return (act.astype(x.dtype) @ w_down[e] + b_down[e]).astype(jnp.float32)

    y = lax.map(expert_block, (row_tok.reshape(n_blocks, MOE_BLOCK), blk_exp))
    y = y.reshape(n_rows, -1) * row_gate[:, None]
    out = jax.ops.segment_sum(y, row_tok, num_segments=n)
    return out.astype(x.dtype).reshape(shp)


def setup_inputs(seed: int = 0) -> dict:
    key = jax.random.key(seed)
    ks = jax.random.split(key, 28)
    f32 = jnp.float32
    n_pages = PAST_LEN // PAGE_SIZE
    n_used = DEC_BATCH * n_pages
    n_pool = (5 * n_used) // 4

    def nrm(k, shape, scale):
        return jax.random.normal(k, shape, f32) * scale

    perm = jax.random.permutation(ks[5], n_pool)
    page_table = perm[:n_used].reshape(DEC_BATCH, n_pages).astype(jnp.int32)
    return {
        'x_prompt': nrm(ks[0], (BATCH, SEQ, D_MODEL), 1.0),
        'x_sample': nrm(ks[1], (DEC_BATCH, DEC_SEQ, D_MODEL), 1.0),
        'cache_k': nrm(ks[2], (DEPTH, n_pool, PAGE_SIZE, DIFF_QK_HEADS, DIFF_DH), 1.0),
        'cache_v': nrm(ks[3], (DEPTH, n_pool, PAGE_SIZE, DIFF_HEADS, 2 * DIFF_DH), 1.0),
        'state_gla': nrm(ks[4], (DEPTH, DEC_BATCH, GLA_HEADS, GLA_DK, GLA_DV), 0.1),
        'page_table': page_table,
        'norm_attn': 1.0 + nrm(ks[6], (DEPTH, D_MODEL), 0.01),
        'w_in': nrm(ks[7], (DEPTH, D_MODEL, IN_COLS), D_MODEL ** -0.5),
        'gla_gate_up': nrm(ks[8], (DEPTH, GLA_RANK, GLA_KDIM), GLA_RANK ** -0.5),
        'gla_gate_bias': nrm(ks[9], (DEPTH, GLA_KDIM), 0.1),
        'gla_norm': 1.0 + nrm(ks[10], (DEPTH, GLA_DV), 0.01),
        'lam_q1': nrm(ks[11], (DEPTH, DIFF_DH), 0.1),
        'lam_k1': nrm(ks[12], (DEPTH, DIFF_DH), 0.1),
        'lam_q2': nrm(ks[13], (DEPTH, DIFF_DH), 0.1),
        'lam_k2': nrm(ks[14], (DEPTH, DIFF_DH), 0.1),
        'diff_subln': 1.0 + nrm(ks[15], (DEPTH, 2 * DIFF_DH), 0.01),
        'w_branch_gla': nrm(ks[16], (DEPTH, GLA_VDIM, D_MODEL), GLA_VDIM ** -0.5),
        'w_branch_diff': nrm(ks[17], (DEPTH, DIFF_VDIM, D_MODEL), DIFF_VDIM ** -0.5),
        'w_out': nrm(ks[18], (DEPTH, D_MODEL, D_MODEL), D_MODEL ** -0.5),
        'norm_ffn': 1.0 + nrm(ks[19], (DEPTH, D_MODEL), 0.01),
        'w_router': nrm(ks[20], (DEPTH, D_MODEL, N_EXPERTS), D_MODEL ** -0.5),
        'b_router': nrm(ks[21], (DEPTH, N_EXPERTS), 0.01),
        'w_up': nrm(ks[22], (DEPTH, N_EXPERTS, D_MODEL, 2 * D_EXPERT), D_MODEL ** -0.5),
        'b_up': nrm(ks[23], (DEPTH, N_EXPERTS, 2 * D_EXPERT), 0.01),
        'w_down': nrm(ks[24], (DEPTH, N_EXPERTS, D_EXPERT, D_MODEL), D_EXPERT ** -0.5),
        'b_down': nrm(ks[25], (DEPTH, N_EXPERTS, D_MODEL), 0.01),
        'norm_final': 1.0 + nrm(ks[26], (D_MODEL,), 0.01),
    }


def reference(x_prompt, x_sample, cache_k, cache_v, state_gla, page_table, norm_attn, w_in, gla_gate_up,
              gla_gate_bias, gla_norm, lam_q1, lam_k1, lam_q2, lam_k2, diff_subln, w_branch_gla, w_branch_diff,
              w_out, norm_ffn, w_router, b_router, w_up, b_up, w_down, b_down, norm_final):
    b_p, t_p, _ = x_prompt.shape
    b_s, t_s, _ = x_sample.shape
    past_len = page_table.shape[1] * cache_k.shape[2]
    pos_p = jnp.arange(t_p, dtype=jnp.int32)
    pos_s = past_len + jnp.arange(t_s, dtype=jnp.int32)
    h_p, h_s = x_prompt, x_sample
    k_p_l, v_p_l, s_p_l, k_s_l, v_s_l, s_s_l = [], [], [], [], [], []
    for layer in range(DEPTH):
        mix_w = (w_in[layer], gla_gate_up[layer], gla_gate_bias[layer], gla_norm[layer], lam_q1[layer],
                 lam_k1[layer], lam_q2[layer], lam_k2[layer], diff_subln[layer], w_branch_gla[layer],
                 w_branch_diff[layer], w_out[layer])
        moe_w = (w_router[layer], b_router[layer], w_up[layer], b_up[layer], w_down[layer], b_down[layer])
        s0_p = jnp.zeros((b_p, GLA_HEADS, GLA_DK, GLA_DV), jnp.float32)
        mix_p, s_p, k_p, v_p = token_mixers(rms_norm(h_p, norm_attn[layer]), pos_p, s0_p, None, None, layer, *mix_w)
        past_k = cache_k[layer][page_table].reshape(b_s, past_len, DIFF_QK_HEADS, DIFF_DH)
        past_v = cache_v[layer][page_table].reshape(b_s, past_len, DIFF_HEADS, 2 * DIFF_DH)
        mix_s, s_s, k_s, v_s = token_mixers(rms_norm(h_s, norm_attn[layer]), pos_s, state_gla[layer],
                                            past_k, past_v, layer, *mix_w)
        h_p = h_p + mix_p
        h_s = h_s + mix_s
        h_p = h_p + moe_ffn(rms_norm(h_p, norm_ffn[layer]), *moe_w)
        h_s = h_s + moe_ffn(rms_norm(h_s, norm_ffn[layer]), *moe_w)
        k_p_l.append(k_p)
        v_p_l.append(v_p)
        s_p_l.append(s_p)
        k_s_l.append(k_s)
        v_s_l.append(v_s)
        s_s_l.append(s_s)
    y_prompt = rms_norm(h_p, norm_final)
    y_sample = rms_norm(h_s, norm_final)
    k_prompt = jnp.stack(k_p_l)
    v_prompt = jnp.stack(v_p_l)
    gla_prompt = jnp.stack(s_p_l)
    k_sample = jnp.stack(k_s_l)
    v_sample = jnp.stack(v_s_l)
    gla_sample = jnp.stack(s_s_l)
    return (y_prompt, y_sample, k_prompt, v_prompt, gla_prompt, k_sample, v_sample, gla_sample)
```

```python
import functools
import math

import jax
import jax.numpy as jnp
from jax import lax
from jax.experimental import pallas as pl
from jax.experimental.pallas import tpu as pltpu

F32 = jnp.float32
BF16 = jnp.bfloat16

EPS = 1e-6
GLA_HEADS = 4
GLA_DK = 128
GLA_DV = 256
GLA_RANK = 16
GLA_NORMALIZER = 16.0
GLA_CHUNK = 64
DIFF_HEADS = 8
DIFF_DH = 64
ROPE_THETA = 10000.0
N_EXPERTS = 32
TOP_K = 4
SWIGLU_ALPHA = 1.702
SWIGLU_LIMIT = 7.0

LANES = 128
MOE_ROWS = 256
ATTN_BLOCK = 512
PAGES_PER_STEP = 4
VMEM_LIMIT = 48 * 1024 * 1024


def _cparams(sem):
    return pltpu.CompilerParams(dimension_semantics=sem, vmem_limit_bytes=VMEM_LIMIT)


def _dot(a, b):
    return jnp.dot(a, b, preferred_element_type=F32)


def _dot_nt(a, b):
    return lax.dot_general(a, b, (((1,), (1,)), ((), ())), preferred_element_type=F32)


def _dot_tn(a, b):
    return lax.dot_general(a, b, (((0,), (0,)), ((), ())), preferred_element_type=F32)


def _rms(x, g):
    ms = jnp.mean(x * x, axis=-1, keepdims=True)
    return x * lax.rsqrt(ms + EPS) * g


def _sigmoid(x):
    return 1.0 / (1.0 + jnp.exp(-x))


def _row_tile(n):
    for t in (512, 256, 128, 64, 32, 16, 8):
        if n % t == 0:
            return t
    raise ValueError(f"token count {n} is not a multiple of 8")


def _full(shape):
    return pl.BlockSpec(shape, lambda *_: (0,) * len(shape))


def _gla_proj_kernel(x_ref, g_ref, wq_ref, wk_ref, wv_ref, wlr_ref, wr_ref, gup_ref, gb_ref,
                     q_ref, k_ref, v_ref, la_ref, r_ref):
    xn = _rms(x_ref[...], g_ref[...]).astype(BF16)
    q_ref[...] = _dot(xn, wq_ref[...]) * (GLA_DK ** -0.5)
    k_ref[...] = _dot(xn, wk_ref[...])
    v_ref[...] = _dot(xn, wv_ref[...]).astype(BF16)
    glr = _dot(xn, wlr_ref[...]).astype(BF16)
    z = _dot(glr, gup_ref[...]) + gb_ref[...]
    log_sig = jnp.minimum(z, 0.0) - jnp.log1p(jnp.exp(-jnp.abs(z)))
    la_ref[...] = log_sig * (1.0 / GLA_NORMALIZER)
    r_ref[...] = _dot(xn, wr_ref[...])


def _gla_proj(x, g, wq, wk, wv, wlr, wr, gup, gb):
    n, d = x.shape
    tm = _row_tile(n)
    kd, vd = wq.shape[1], wv.shape[1]
    row = lambda w: pl.BlockSpec((tm, w), lambda i: (i, 0))
    return pl.pallas_call(
        _gla_proj_kernel,
        grid=(n // tm,),
        in_specs=[row(d), _full(g.shape), _full(wq.shape), _full(wk.shape), _full(wv.shape),
                  _full(wlr.shape), _full(wr.shape), _full(gup.shape), _full(gb.shape)],
        out_specs=[row(kd), row(kd), row(vd), row(kd), row(vd)],
        out_shape=[jax.ShapeDtypeStruct((n, kd), F32), jax.ShapeDtypeStruct((n, kd), F32),
                   jax.ShapeDtypeStruct((n, vd), BF16), jax.ShapeDtypeStruct((n, kd), F32),
                   jax.ShapeDtypeStruct((n, vd), F32)],
        compiler_params=_cparams(("parallel",)),
        name="gla_proj",
    )(x, g, wq, wk, wv, wlr, wr, gup, gb)


def _rope_chunk(xc, cos, sin_signed, first_half):
    swapped = jnp.where(first_half, pltpu.roll(xc, LANES - DIFF_DH // 2, 1), pltpu.roll(xc, DIFF_DH // 2, 1))
    return xc * cos + swapped * sin_signed


def _diff_proj_kernel(x_ref, g_ref, cos_ref, sin_ref, wq_ref, wk_ref, wv_ref, wa_ref, wb_ref,
                      q_ref, kf_ref, kb_ref, vf_ref, vb_ref, sa_ref, sb_ref):
    xn = _rms(x_ref[...], g_ref[...]).astype(BF16)
    cos = cos_ref[...]
    sin_signed = sin_ref[...]
    lane = lax.broadcasted_iota(jnp.int32, (1, LANES), 1)
    first_half = (lane % DIFF_DH) < (DIFF_DH // 2)
    n_chunks = q_ref.shape[1] // LANES
    q = _dot(xn, wq_ref[...])
    for c in range(n_chunks):
        sl = slice(c * LANES, (c + 1) * LANES)
        q_ref[:, sl] = (_rope_chunk(q[:, sl], cos, sin_signed, first_half) * (DIFF_DH ** -0.5)).astype(BF16)
    k = _dot(xn, wk_ref[...])
    for c in range(n_chunks):
        sl = slice(c * LANES, (c + 1) * LANES)
        kr = _rope_chunk(k[:, sl], cos, sin_signed, first_half)
        kf_ref[:, sl] = kr
        kb_ref[:, sl] = kr.astype(BF16)
    v = _dot(xn, wv_ref[...])
    vf_ref[...] = v
    vb_ref[...] = v.astype(BF16)
    sa_ref[...] = _sigmoid(_dot(xn, wa_ref[...]))
    sb_ref[...] = _sigmoid(_dot(xn, wb_ref[...]))


def _diff_proj(x, g, cos, sin_signed, wq, wk, wv, wa, wb):
    n, d = x.shape
    tm = min(_row_tile(n), 256)
    row = lambda w: pl.BlockSpec((tm, w), lambda i: (i, 0))
    sds = lambda dt: jax.ShapeDtypeStruct((n, d), dt)
    return pl.pallas_call(
        _diff_proj_kernel,
        grid=(n // tm,),
        in_specs=[row(d), _full(g.shape), row(LANES), row(LANES)] + [_full(wq.shape)] * 5,
        out_specs=[row(d)] * 7,
        out_shape=[sds(BF16), sds(F32), sds(BF16), sds(F32), sds(BF16), sds(F32), sds(F32)],
        compiler_params=_cparams(("parallel",)),
        name="diff_proj",
    )(x, g, cos, sin_signed, wq, wk, wv, wa, wb)


def _gla_kernel(has_s0, *refs):
    if has_s0:
        q_ref, k_ref, v_ref, la_ref, r_ref, gn_ref, s0_ref, o_ref, sfin_ref, st_ref = refs
    else:
        q_ref, k_ref, v_ref, la_ref, r_ref, gn_ref, o_ref, sfin_ref, st_ref = refs
    n = pl.program_id(1)
    c = q_ref.shape[0]

    @pl.when(n == 0)
    def _():
        for h in range(GLA_HEADS):
            if has_s0:
                st_ref[h] = s0_ref[0, h].T
            else:
                st_ref[h] = jnp.zeros(st_ref.shape[1:], F32)

    row = lax.broadcasted_iota(jnp.int32, (c, c), 0)
    col = lax.broadcasted_iota(jnp.int32, (c, c), 1)
    causal = row >= col
    tri = jnp.where(causal, 1.0, 0.0).astype(BF16)
    gn = gn_ref[...]
    for h in range(GLA_HEADS):
        ks = slice(h * GLA_DK, (h + 1) * GLA_DK)
        vs = slice(h * GLA_DV, (h + 1) * GLA_DV)
        la = la_ref[:, ks]
        hi = la.astype(BF16)
        rem = la - hi.astype(F32)
        mid = rem.astype(BF16)
        lo = (rem - mid.astype(F32)).astype(BF16)
        bcum = _dot(tri, hi) + _dot(tri, mid) + _dot(tri, lo)
        qh = q_ref[:, ks]
        kh = k_ref[:, ks]
        vh = v_ref[:, vs]
        q_t = (qh * jnp.exp(bcum)).astype(BF16)
        k_t = (kh * jnp.exp(-bcum)).astype(BF16)
        scores = jnp.where(causal, _dot_nt(q_t, k_t), 0.0)
        st = st_ref[h]
        o = _dot_nt(q_t, st.astype(BF16)) + _dot(scores.astype(BF16), vh)
        b_last = bcum[c - 1:c, :]
        k_dec = (kh * jnp.exp(b_last - bcum)).astype(BF16)
        st_ref[h] = st * jnp.exp(b_last) + _dot_tn(vh, k_dec)
        rh = r_ref[:, vs]
        o_ref[:, vs] = (_rms(o, gn) * (rh * _sigmoid(rh))).astype(BF16)

    @pl.when(n == pl.num_programs(1) - 1)
    def _():
        for h in range(GLA_HEADS):
            sfin_ref[0, h] = st_ref[h].T


def _gla_chunks(q, k, v, la, r, gn, s0, batch, seq, chunk):
    nc = seq // chunk
    kd, vd = q.shape[1], v.shape[1]
    blk = lambda w: pl.BlockSpec((chunk, w), lambda b, n: (b * nc + n, 0))
    state_spec = pl.BlockSpec((1, GLA_HEADS, GLA_DK, GLA_DV), lambda b, n: (b, 0, 0, 0))
    in_specs = [blk(kd), blk(kd), blk(vd), blk(kd), blk(vd), _full(gn.shape)]
    args = [q, k, v, la, r, gn]
    if s0 is not None:
        in_specs.append(state_spec)
        args.append(s0)
    return pl.pallas_call(
        functools.partial(_gla_kernel, s0 is not None),
        grid=(batch, nc),
        in_specs=in_specs,
        out_specs=[blk(vd), state_spec],
        out_shape=[jax.ShapeDtypeStruct((batch * seq, vd), BF16),
                   jax.ShapeDtypeStruct((batch, GLA_HEADS, GLA_DK, GLA_DV), F32)],
        scratch_shapes=[pltpu.VMEM((GLA_HEADS, GLA_DV, GLA_DK), F32)],
        compiler_params=_cparams(("parallel", "arbitrary")),
        name="gla_chunks",
    )(*args)


def _lambda(lq1_ref, lk1_ref, lq2_ref, lk2_ref, lam_init):
    s1 = jnp.sum(lq1_ref[...] * lk1_ref[...], axis=-1, keepdims=True)
    s2 = jnp.sum(lq2_ref[...] * lk2_ref[...], axis=-1, keepdims=True)
    return jnp.exp(s1) - jnp.exp(s2) + lam_init


def _diff_prompt_kernel(lam_init, q_ref, k_ref, v_ref, lq1_ref, lk1_ref, lq2_ref, lk2_ref, sub_ref,
                        o_ref, m1_ref, l1_ref, a1_ref, m2_ref, l2_ref, a2_ref):
    qi = pl.program_id(2)
    tq = q_ref.shape[0]
    lam = _lambda(lq1_ref, lk1_ref, lq2_ref, lk2_ref, lam_init)
    q = q_ref[...]
    lane = lax.broadcasted_iota(jnp.int32, (1, LANES), 1)
    zero = jnp.zeros_like(q)
    q_maps = (jnp.where(lane < DIFF_DH, q, zero), jnp.where(lane >= DIFF_DH, q, zero))
    stats = ((m1_ref, l1_ref, a1_ref), (m2_ref, l2_ref, a2_ref))
    for m_ref, l_ref, a_ref in stats:
        m_ref[...] = jnp.full(m_ref.shape, -jnp.inf, F32)
        l_ref[...] = jnp.zeros(l_ref.shape, F32)
        a_ref[...] = jnp.zeros(a_ref.shape, F32)

    def step(start, mask):
        kj = k_ref[pl.ds(start, tq), :]
        vj = v_ref[pl.ds(start, tq), :]
        for qm, (m_ref, l_ref, a_ref) in zip(q_maps, stats):
            s = _dot_nt(qm, kj)
            if mask is not None:
                s = jnp.where(mask, s, -jnp.inf)
            m_old = m_ref[...]
            m_new = jnp.maximum(m_old, jnp.max(s, axis=-1, keepdims=True))
            p = jnp.exp(s - m_new)
            alpha = jnp.exp(m_old - m_new)
            l_ref[...] = alpha * l_ref[...] + jnp.sum(p, axis=-1, keepdims=True)
            a_ref[...] = alpha * a_ref[...] + _dot(p.astype(BF16), vj)
            m_ref[...] = m_new

    def body(j, carry):
        step(pl.multiple_of(j * tq, tq), None)
        return carry

    lax.fori_loop(0, qi, body, 0)
    row = lax.broadcasted_iota(jnp.int32, (tq, tq), 0)
    col = lax.broadcasted_iota(jnp.int32, (tq, tq), 1)
    step(pl.multiple_of(qi * tq, tq), row >= col)

    o = a1_ref[...] * (1.0 / l1_ref[...]) - lam * (a2_ref[...] * (1.0 / l2_ref[...]))
    o_ref[...] = (_rms(o, sub_ref[...]) * (1.0 - lam_init)).astype(BF16)


def _diff_prompt(q, k, v, lams, subln, batch, seq, lam_init):
    tq = min(ATTN_BLOCK, seq)
    nq = seq // tq
    vec = _full(lams[0].shape)
    return pl.pallas_call(
        functools.partial(_diff_prompt_kernel, lam_init),
        grid=(batch, DIFF_HEADS, nq),
        in_specs=[pl.BlockSpec((tq, LANES), lambda b, h, i: (b * nq + i, h)),
                  pl.BlockSpec((seq, LANES), lambda b, h, i: (b, h)),
                  pl.BlockSpec((seq, LANES), lambda b, h, i: (b, h)),
                  vec, vec, vec, vec, _full(subln.shape)],
        out_specs=pl.BlockSpec((tq, LANES), lambda b, h, i: (b * nq + i, h)),
        out_shape=jax.ShapeDtypeStruct((batch * seq, DIFF_HEADS * LANES), BF16),
        scratch_shapes=[pltpu.VMEM((tq, 1), F32), pltpu.VMEM((tq, 1), F32), pltpu.VMEM((tq, LANES), F32)] * 2,
        compiler_params=_cparams(("parallel", "parallel", "arbitrary")),
        name="diff_prompt",
    )(q, k, v, *lams, subln)


def _diff_sample_kernel(lam_init, pt_ref, q_ref, kn_ref, vn_ref, lq1_ref, lk1_ref, lq2_ref, lk2_ref, sub_ref,
                        *rest):
    npg = PAGES_PER_STEP
    k_refs, v_refs = rest[:npg], rest[npg:2 * npg]
    o_ref, m_ref, l_ref, acc_ref, qe_ref = rest[2 * npg:]
    g = pl.program_id(1)
    t_new = q_ref.shape[1]
    rows = 2 * t_new * DIFF_HEADS
    d = q_ref.shape[2]

    @pl.when(g == 0)
    def _():
        q = q_ref[0]
        parts = [jnp.broadcast_to(q[t:t + 1, :], (DIFF_HEADS, d)) for _ in range(2) for t in range(t_new)]
        qx = jnp.concatenate(parts, axis=0)
        r = lax.broadcasted_iota(jnp.int32, (rows, d), 0)
        c = lax.broadcasted_iota(jnp.int32, (rows, d), 1)
        qk_head = 2 * (r % DIFF_HEADS) + r // (t_new * DIFF_HEADS)
        qe_ref[...] = jnp.where(c // DIFF_DH == qk_head, qx, 0.0).astype(BF16)
        m_ref[...] = jnp.full(m_ref.shape, -jnp.inf, F32)
        l_ref[...] = jnp.zeros(l_ref.shape, F32)
        acc_ref[...] = jnp.zeros(acc_ref.shape, F32)

    qe = qe_ref[...]
    kcat = jnp.concatenate([kr[0].astype(BF16) for kr in k_refs], axis=0)
    vcat = jnp.concatenate([vr[0].astype(BF16) for vr in v_refs], axis=0)
    s = _dot_nt(qe, kcat)
    m_old = m_ref[...]
    m_new = jnp.maximum(m_old, jnp.max(s, axis=-1, keepdims=True))
    p = jnp.exp(s - m_new)
    alpha = jnp.exp(m_old - m_new)
    l_ref[...] = alpha * l_ref[...] + jnp.sum(p, axis=-1, keepdims=True)
    acc_ref[...] = alpha * acc_ref[...] + _dot(p.astype(BF16), vcat)
    m_ref[...] = m_new

    @pl.when(g == pl.num_programs(1) - 1)
    def _():
        lam = _lambda(lq1_ref, lk1_ref, lq2_ref, lk2_ref, lam_init)
        qf = qe.astype(F32)
        kn = kn_ref[0]
        vn = vn_ref[0]
        r1 = lax.broadcasted_iota(jnp.int32, (rows, 1), 0)
        tok = (r1 % (t_new * DIFF_HEADS)) // DIFF_HEADS
        s_new = []
        for t in range(t_new):
            st = jnp.sum(qf * kn[t:t + 1, :], axis=-1, keepdims=True)
            s_new.append(jnp.where(t <= tok, st, -jnp.inf))
        m_old = m_ref[...]
        m_fin = m_old
        for st in s_new:
            m_fin = jnp.maximum(m_fin, st)
        alpha = jnp.exp(m_old - m_fin)
        l_fin = alpha * l_ref[...]
        acc = alpha * acc_ref[...]
        for t in range(t_new):
            pt = jnp.exp(s_new[t] - m_fin)
            l_fin = l_fin + pt
            acc = acc + pt * vn[t:t + 1, :]
        an = acc * (1.0 / l_fin)
        half = rows // 2
        dmap = an[:half] - lam * an[half:]
        r = lax.broadcasted_iota(jnp.int32, (half, d), 0)
        c = lax.broadcasted_iota(jnp.int32, (half, d), 1)
        own = (c // LANES) == (r % DIFF_HEADS)
        ms = jnp.sum(jnp.where(own, dmap * dmap, 0.0), axis=-1, keepdims=True) * (1.0 / LANES)
        dn = jnp.where(own, dmap * lax.rsqrt(ms + EPS), 0.0)
        outs = [jnp.sum(dn[t * DIFF_HEADS:(t + 1) * DIFF_HEADS], axis=0, keepdims=True) for t in range(t_new)]
        o_ref[0] = jnp.concatenate(outs, axis=0) * sub_ref[...] * (1.0 - lam_init)


def _diff_sample(page_table, q, k_new, v_new, lams, subln_row, cache_k, cache_v, lam_init):
    batch, t_new, d = q.shape
    n_pages = page_table.shape[1]
    page = cache_k.shape[1]
    npg = PAGES_PER_STEP
    assert n_pages % npg == 0
    rows = 2 * t_new * DIFF_HEADS
    vec = pl.BlockSpec(lams[0].shape, lambda b, g, pt: (0, 0))
    tok = pl.BlockSpec((1, t_new, d), lambda b, g, pt: (b, 0, 0))

    def page_spec(j):
        return pl.BlockSpec((1, page, d), lambda b, g, pt: (pt[b * n_pages + g * npg + j], 0, 0))

    grid_spec = pltpu.PrefetchScalarGridSpec(
        num_scalar_prefetch=1,
        grid=(batch, n_pages // npg),
        in_specs=[tok, tok, tok, vec, vec, vec, vec, pl.BlockSpec(subln_row.shape, lambda b, g, pt: (0, 0))]
        + [page_spec(j) for j in range(npg)] * 2,
        out_specs=pl.BlockSpec((1, t_new, d), lambda b, g, pt: (b, 0, 0)),
        scratch_shapes=[pltpu.VMEM((rows, 1), F32), pltpu.VMEM((rows, 1), F32), pltpu.VMEM((rows, d), F32),
                        pltpu.VMEM((rows, d), BF16)],
    )
    return pl.pallas_call(
        functools.partial(_diff_sample_kernel, lam_init),
        grid_spec=grid_spec,
        out_shape=jax.ShapeDtypeStruct((batch, t_new, d), F32),
        compiler_params=_cparams(("parallel", "arbitrary")),
        name="diff_sample",
    )(page_table.reshape(-1), q, k_new, v_new, *lams, subln_row, *([cache_k] * npg), *([cache_v] * npg))


def _merge_route_kernel(x_ref, og_ref, od_ref, sa_ref, sb_ref, wg_ref, wd_ref, wo_ref, gf_ref, wr_ref, br_ref,
                        h_ref, hn_ref, idx_ref, gate_ref, rank_ref, cnt_ref, carry_ref):
    i = pl.program_id(0)
    tm = x_ref.shape[0]

    @pl.when(i == 0)
    def _():
        carry_ref[...] = jnp.zeros(carry_ref.shape, F32)

    merged = sa_ref[...] * _dot(og_ref[...], wg_ref[...]) + sb_ref[...] * _dot(od_ref[...], wd_ref[...])
    h = x_ref[...] + _dot(merged.astype(BF16), wo_ref[...])
    h_ref[...] = h
    hn = _rms(h, gf_ref[...]).astype(BF16)
    hn_ref[...] = hn
    lane = lax.broadcasted_iota(jnp.int32, (tm, LANES), 1)
    logits = jnp.where(lane < N_EXPERTS, _dot(hn, wr_ref[...]) + br_ref[...], -jnp.inf)
    idx_out = jnp.zeros((tm, LANES), jnp.int32)
    val_out = jnp.zeros((tm, LANES), F32)
    onehot = jnp.zeros((tm, LANES), F32)
    picks = []
    for kk in range(TOP_K):
        mx = jnp.max(logits, axis=-1, keepdims=True)
        pick = jnp.min(jnp.where(logits == mx, lane, LANES), axis=-1, keepdims=True)
        hit = lane == pick
        logits = jnp.where(hit, -jnp.inf, logits)
        onehot = jnp.where(hit, 1.0, onehot)
        idx_out = jnp.where(lane == kk, pick, idx_out)
        val_out = jnp.where(lane == kk, mx, val_out)
        picks.append(hit)
        if kk == 0:
            top = mx
    e = jnp.where(lane < TOP_K, jnp.exp(val_out - top), 0.0)
    gate_ref[...] = e * (1.0 / jnp.sum(e, axis=-1, keepdims=True))
    idx_ref[...] = idx_out
    row = lax.broadcasted_iota(jnp.int32, (tm, tm), 0)
    col = lax.broadcasted_iota(jnp.int32, (tm, tm), 1)
    strict = jnp.where(row > col, 1.0, 0.0).astype(BF16)
    before = _dot(strict, onehot.astype(BF16)) + carry_ref[...]
    rank_out = jnp.zeros((tm, LANES), jnp.int32)
    for kk in range(TOP_K):
        rk = jnp.sum(jnp.where(picks[kk], before, 0.0), axis=-1, keepdims=True)
        rank_out = jnp.where(lane == kk, rk.astype(jnp.int32), rank_out)
    rank_ref[...] = rank_out
    carry_ref[...] = carry_ref[...] + jnp.sum(onehot, axis=0, keepdims=True)
    cnt_ref[...] = carry_ref[...].astype(jnp.int32)


def _merge_route(x, og, od, sa, sb, wg, wd, wo, gf, wr, br):
    n, d = x.shape
    tm = min(_row_tile(n), 256)
    row = lambda w: pl.BlockSpec((tm, w), lambda i: (i, 0))
    return pl.pallas_call(
        _merge_route_kernel,
        grid=(n // tm,),
        in_specs=[row(d)] * 5 + [_full(wg.shape), _full(wd.shape), _full(wo.shape), _full(gf.shape),
                                 _full(wr.shape), _full(br.shape)],
        out_specs=[row(d), row(d), row(LANES), row(LANES), row(LANES), _full((1, LANES))],
        out_shape=[jax.ShapeDtypeStruct((n, d), F32), jax.ShapeDtypeStruct((n, d), BF16),
                   jax.ShapeDtypeStruct((n, LANES), jnp.int32), jax.ShapeDtypeStruct((n, LANES), F32),
                   jax.ShapeDtypeStruct((n, LANES), jnp.int32), jax.ShapeDtypeStruct((1, LANES), jnp.int32)],
        scratch_shapes=[pltpu.VMEM((1, LANES), F32)],
        compiler_params=_cparams(("arbitrary",)),
        name="merge_route",
    )(x, og, od, sa, sb, wg, wd, wo, gf, wr, br)


def _moe_kernel(be_ref, nu_ref, x_ref, wg_ref, wl_ref, bg_ref, bl_ref, wd_ref, bd_ref, y_ref):
    i = pl.program_id(0)

    @pl.when(i < nu_ref[0])
    def _():
        x = x_ref[...]
        x_glu = jnp.minimum(_dot(x, wg_ref[0]) + bg_ref[0], SWIGLU_LIMIT)
        x_lin = jnp.clip(_dot(x, wl_ref[0]) + bl_ref[0], -SWIGLU_LIMIT, SWIGLU_LIMIT)
        act = x_glu * _sigmoid(SWIGLU_ALPHA * x_glu) * (x_lin + 1.0)
        y_ref[...] = _dot(act.astype(BF16), wd_ref[0]) + bd_ref[0]

    @pl.when(i >= nu_ref[0])
    def _():
        y_ref[...] = jnp.zeros(y_ref.shape, F32)


def _moe_experts(blk_exp, n_used, xs, wg, wl, bg, bl, wd, bd):
    n_rows, d = xs.shape
    de = wg.shape[2]
    wspec = lambda a, b: pl.BlockSpec((1, a, b), lambda i, be, nu: (be[i], 0, 0))
    grid_spec = pltpu.PrefetchScalarGridSpec(
        num_scalar_prefetch=2,
        grid=(n_rows // MOE_ROWS,),
        in_specs=[pl.BlockSpec((MOE_ROWS, d), lambda i, be, nu: (i, 0)),
                  wspec(d, de), wspec(d, de), wspec(1, de), wspec(1, de), wspec(de, d), wspec(1, d)],
        out_specs=pl.BlockSpec((MOE_ROWS, d), lambda i, be, nu: (i, 0)),
    )
    return pl.pallas_call(
        _moe_kernel,
        grid_spec=grid_spec,
        out_shape=jax.ShapeDtypeStruct((n_rows, d), F32),
        compiler_params=_cparams(("arbitrary",)),
        name="moe_experts",
    )(blk_exp, n_used, xs, wg, wl, bg, bl, wd, bd)


def _combine_kernel(h_ref, gate_ref, y0_ref, y1_ref, y2_ref, y3_ref, gf_ref, o_ref):
    gate = gate_ref[...]
    acc = h_ref[...]
    moe = jnp.zeros(acc.shape, F32)
    for kk, y_ref in enumerate((y0_ref, y1_ref, y2_ref, y3_ref)):
        moe = moe + gate[:, kk:kk + 1] * y_ref[0]
    o_ref[...] = _rms(acc + moe, gf_ref[...])


def _combine(h, gate, yg, gf):
    n, d = h.shape
    tm = min(_row_tile(n), 256)
    row = lambda w: pl.BlockSpec((tm, w), lambda i: (i, 0))
    ysp = lambda kk: pl.BlockSpec((1, tm, d), lambda i: (kk, i, 0))
    return pl.pallas_call(
        _combine_kernel,
        grid=(n // tm,),
        in_specs=[row(d), row(LANES)] + [ysp(kk) for kk in range(TOP_K)] + [_full(gf.shape)],
        out_specs=row(d),
        out_shape=jax.ShapeDtypeStruct((n, d), F32),
        compiler_params=_cparams(("parallel",)),
        name="combine",
    )(h, gate, yg, yg, yg, yg, gf)


def _rope_tables(pos):
    half = DIFF_DH // 2
    inv_freq = ROPE_THETA ** (-jnp.arange(half, dtype=F32) / half)
    ang = pos.astype(F32)[:, None] * inv_freq[None, :]
    cos, sin = jnp.cos(ang), jnp.sin(ang)
    reps = LANES // DIFF_DH
    return jnp.tile(jnp.concatenate([cos, cos], axis=1), (1, reps)), jnp.tile(jnp.concatenate([-sin, sin], axis=1), (1, reps))


def _layer(layer, h_p, h_s, cache_k, cache_v, state_gla, page_table, norm_attn, w_in, gla_gate_up, gla_gate_bias,
           gla_norm, lam_q1, lam_k1, lam_q2, lam_k2, diff_subln, w_branch_gla, w_branch_diff, w_out, norm_ffn,
           w_router, b_router, w_up, b_up, w_down, b_down, norm_final):
    b_p, t_p, d = h_p.shape
    b_s, t_s, _ = h_s.shape
    n_p, n_s = b_p * t_p, b_s * t_s
    n = n_p + n_s
    n_pages, page = page_table.shape[1], cache_k.shape[1]
    past_len = n_pages * page
    kd, vd = GLA_HEADS * GLA_DK, GLA_HEADS * GLA_DV
    row2 = lambda a: a.reshape(1, -1)

    x = jnp.concatenate([h_p.reshape(n_p, d), h_s.reshape(n_s, d)], axis=0)

    splits = (kd, kd, vd, GLA_RANK, vd, d, d, d, d, d)
    offs = [0]
    for s in splits:
        offs.append(offs[-1] + s)
    wb = w_in.astype(BF16)
    w_gq, w_gk, w_gv, w_glr, w_gr, w_dq, w_dk, w_dv, w_ga, w_gb = [wb[:, offs[j]:offs[j + 1]] for j in range(10)]
    w_glr = jnp.pad(w_glr, ((0, 0), (0, LANES - GLA_RANK)))
    gup = jnp.pad(gla_gate_up.astype(BF16), ((0, LANES - GLA_RANK), (0, 0)))
    g_attn = row2(norm_attn)

    gq, gk, gv, gla_la, gr = _gla_proj(x, g_attn, w_gq, w_gk, w_gv, w_glr, w_gr, gup, row2(gla_gate_bias))

    pos = jnp.concatenate([jnp.tile(jnp.arange(t_p, dtype=jnp.int32), b_p),
                           jnp.tile(past_len + jnp.arange(t_s, dtype=jnp.int32), b_s)])
    cos, sin_signed = _rope_tables(pos)
    dq, dk_f, dk_b, dv_f, dv_b, sig_a, sig_b = _diff_proj(x, g_attn, cos, sin_signed, w_dq, w_dk, w_dv, w_ga, w_gb)

    gn = row2(gla_norm)
    og_p, s_p = _gla_chunks(gq, gk, gv, gla_la, gr, gn, None, b_p, t_p, GLA_CHUNK if t_p % GLA_CHUNK == 0 else t_p)
    t_pad = -(-t_s // GLA_CHUNK) * GLA_CHUNK
    pad_s = lambda a: jnp.pad(a[n_p:].reshape(b_s, t_s, -1), ((0, 0), (0, t_pad - t_s), (0, 0))).reshape(b_s * t_pad, -1)
    og_s, s_s = _gla_chunks(pad_s(gq), pad_s(gk), pad_s(gv), pad_s(gla_la), pad_s(gr), gn, state_gla, b_s, t_pad, t_pad)
    og_s = og_s.reshape(b_s, t_pad, vd)[:, :t_s].reshape(n_s, vd)
    o_gla = jnp.concatenate([og_p, og_s], axis=0)

    lam_init = 0.8 - 0.6 * math.exp(-0.3 * layer)
    lams = (row2(lam_q1), row2(lam_k1), row2(lam_q2), row2(lam_k2))
    od_p = _diff_prompt(dq, dk_b, dv_b, lams, row2(diff_subln), b_p, t_p, lam_init)
    tok3 = lambda a: a[n_p:].reshape(b_s, t_s, d)
    od_s = _diff_sample(page_table, tok3(dq).astype(F32), tok3(dk_b).astype(F32), tok3(dv_b).astype(F32), lams,
                        row2(jnp.tile(diff_subln, DIFF_HEADS)), cache_k, cache_v, lam_init)
    o_diff = jnp.concatenate([od_p, od_s.reshape(n_s, d).astype(BF16)], axis=0)

    w_r = jnp.pad(w_router.astype(BF16), ((0, 0), (0, LANES - N_EXPERTS)))
    b_r = jnp.pad(row2(b_router), ((0, 0), (0, LANES - N_EXPERTS)))
    h, hn, idx, gate, rank, counts = _merge_route(
        x, o_gla, o_diff, sig_a, sig_b, w_branch_gla.astype(BF16), w_branch_diff.astype(BF16), w_out.astype(BF16),
        row2(norm_ffn), w_r, b_r)

    counts = counts[0, :N_EXPERTS]
    padded = (counts + MOE_ROWS - 1) // MOE_ROWS * MOE_ROWS
    pad_end = jnp.cumsum(padded)
    pad_start = pad_end - padded
    idx4, rank4 = idx[:, :TOP_K], rank[:, :TOP_K]
    dest = pad_start[idx4] + rank4
    n_rows = -(-(n * TOP_K + N_EXPERTS * (MOE_ROWS - 1)) // MOE_ROWS) * MOE_ROWS
    n_blocks = n_rows // MOE_ROWS
    blk_exp = jnp.minimum(jnp.searchsorted(pad_end, jnp.arange(n_blocks, dtype=jnp.int32) * MOE_ROWS, side='right'),
                          N_EXPERTS - 1).astype(jnp.int32)
    n_used = (pad_end[-1:] // MOE_ROWS).astype(jnp.int32)
    row_tok = jnp.zeros((n_rows,), jnp.int32).at[dest.reshape(-1)].set(
        jnp.repeat(jnp.arange(n, dtype=jnp.int32), TOP_K))
    xs = jnp.take(hn, row_tok, axis=0)

    w_glu = w_up[:, :, 0::2].astype(BF16)
    w_lin = w_up[:, :, 1::2].astype(BF16)
    b_glu = b_up[:, None, 0::2]
    b_lin = b_up[:, None, 1::2]
    y = _moe_experts(blk_exp, n_used, xs, w_glu, w_lin, b_glu, b_lin, w_down.astype(BF16), b_down[:, None, :])
    yg = jnp.take(y, dest.T.reshape(-1), axis=0).reshape(TOP_K, n, d)

    out = _combine(h, gate, yg, row2(norm_final))
    return (out, h, n_p, dk_f, dv_f, s_p, s_s)


def kernel(x_prompt, x_sample, cache_k, cache_v, state_gla, page_table, norm_attn, w_in, gla_gate_up, gla_gate_bias, gla_norm, lam_q1, lam_k1, lam_q2, lam_k2, diff_subln, w_branch_gla, w_branch_diff, w_out, norm_ffn, w_router, b_router, w_up, b_up, w_down, b_down, norm_final):
    depth = w_in.shape[0]
    assert depth == 1, "single-layer step"
    b_p, t_p, d = x_prompt.shape
    b_s, t_s, _ = x_sample.shape
    n_pool, page = cache_k.shape[1], cache_k.shape[2]
    layer = 0
    out, _, n_p, dk_f, dv_f, s_p, s_s = _layer(
        layer, x_prompt, x_sample, cache_k[layer].reshape(n_pool, page, -1), cache_v[layer].reshape(n_pool, page, -1),
        state_gla[layer], page_table, norm_attn[layer], w_in[layer], gla_gate_up[layer], gla_gate_bias[layer],
        gla_norm[layer], lam_q1[layer], lam_k1[layer], lam_q2[layer], lam_k2[layer], diff_subln[layer],
        w_branch_gla[layer], w_branch_diff[layer], w_out[layer], norm_ffn[layer], w_router[layer], b_router[layer],
        w_up[layer], b_up[layer], w_down[layer], b_down[layer], norm_final)
    y_prompt = out[:n_p].reshape(b_p, t_p, d)
    y_sample = out[n_p:].reshape(b_s, t_s, d)
    k_prompt = dk_f[:n_p].reshape(1, b_p, t_p, 2 * DIFF_HEADS, DIFF_DH)
    v_prompt = dv_f[:n_p].reshape(1, b_p, t_p, DIFF_HEADS, 2 * DIFF_DH)
    k_sample = dk_f[n_p:].reshape(1, b_s, t_s, 2 * DIFF_HEADS, DIFF_DH)
    v_sample = dv_f[n_p:].reshape(1, b_s, t_s, DIFF_HEADS, 2 * DIFF_DH)
    return (y_prompt, y_sample, k_prompt, v_prompt, s_p[None], k_sample, v_sample, s_s[None])
```

```python
import functools
import math

import jax
import jax.numpy as jnp
from jax import lax
from jax.experimental import pallas as pl
from jax.experimental.pallas import tpu as pltpu

F32 = jnp.float32
BF16 = jnp.bfloat16

EPS = 1e-6
GLA_HEADS = 4
GLA_DK = 128
GLA_DV = 256
GLA_RANK = 16
GLA_NORMALIZER = 16.0
GLA_CHUNK = 64
DIFF_HEADS = 8
DIFF_QK_HEADS = 2 * DIFF_HEADS
DIFF_DH = 64
ROPE_THETA = 10000.0
N_EXPERTS = 32
TOP_K = 4
SWIGLU_ALPHA = 1.702
SWIGLU_LIMIT = 7.0
LOG2E = math.log2(math.e)

LANES = 128
SUBLANES = 8
MOE_ROWS = 256
ATTN_Q = 1024
ATTN_K = 512
ATTN_SUB = 256
ATTN_LOOKAHEAD = 3
PAGES_PER_STEP = 4
VMEM_LIMIT = 48 * 1024 * 1024


def _cparams(sem):
    return pltpu.CompilerParams(dimension_semantics=sem, vmem_limit_bytes=VMEM_LIMIT)


def _dot(a, b):
    return jnp.dot(a, b, preferred_element_type=F32)


def _dot_nt(a, b):
    return lax.dot_general(a, b, (((1,), (1,)), ((), ())), preferred_element_type=F32)


def _dot_tn(a, b):
    return lax.dot_general(a, b, (((0,), (0,)), ((), ())), preferred_element_type=F32)


def _rms(x, g):
    ms = jnp.mean(x * x, axis=-1, keepdims=True)
    return x * lax.rsqrt(ms + EPS) * g


def _sigmoid(x):
    return 1.0 / (1.0 + jnp.exp(-x))


def _row_tile(n, cap=512):
    for t in (512, 256, 128, 64, 32, 16, 8):
        if t <= cap and n % t == 0:
            return t
    raise ValueError(f"token count {n} is not a multiple of 8")


def _full(shape):
    return pl.BlockSpec(shape, lambda *_: (0,) * len(shape))


def _gla_proj_kernel(x_ref, g_ref, wq_ref, wk_ref, wv_ref, wlr_ref, wr_ref, gup_ref, gb_ref,
                     q_ref, k_ref, v_ref, la_ref, r_ref):
    xn = _rms(x_ref[...], g_ref[...]).astype(BF16)
    q_ref[...] = _dot(xn, wq_ref[...]) * (GLA_DK ** -0.5)
    k_ref[...] = _dot(xn, wk_ref[...])
    v_ref[...] = _dot(xn, wv_ref[...]).astype(BF16)
    glr = _dot(xn, wlr_ref[...]).astype(BF16)
    z = _dot(glr, gup_ref[...]) + gb_ref[...]
    log_sig = jnp.minimum(z, 0.0) - jnp.log1p(jnp.exp(-jnp.abs(z)))
    la_ref[...] = log_sig * (1.0 / GLA_NORMALIZER)
    r_ref[...] = _dot(xn, wr_ref[...])


def _gla_proj(x, g, wq, wk, wv, wlr, wr, gup, gb):
    n, d = x.shape
    tm = _row_tile(n)
    kd, vd = wq.shape[1], wv.shape[1]
    row = lambda w: pl.BlockSpec((tm, w), lambda i: (i, 0))
    return pl.pallas_call(
        _gla_proj_kernel,
        grid=(n // tm,),
        in_specs=[row(d), _full(g.shape), _full(wq.shape), _full(wk.shape), _full(wv.shape),
                  _full(wlr.shape), _full(wr.shape), _full(gup.shape), _full(gb.shape)],
        out_specs=[row(kd), row(kd), row(vd), row(kd), row(vd)],
        out_shape=[jax.ShapeDtypeStruct((n, kd), F32), jax.ShapeDtypeStruct((n, kd), F32),
                   jax.ShapeDtypeStruct((n, vd), BF16), jax.ShapeDtypeStruct((n, kd), F32),
                   jax.ShapeDtypeStruct((n, vd), F32)],
        compiler_params=_cparams(("parallel",)),
        name="gla_proj",
    )(x, g, wq, wk, wv, wlr, wr, gup, gb)


def _rope_chunk(xc, cos, sin_signed, first_half):
    swapped = jnp.where(first_half, pltpu.roll(xc, LANES - DIFF_DH // 2, 1), pltpu.roll(xc, DIFF_DH // 2, 1))
    return xc * cos + swapped * sin_signed


def _diff_proj_kernel(x_ref, g_ref, cos_ref, sin_ref, wq_ref, wk_ref, wv_ref, wa_ref, wb_ref,
                      q_ref, kb_ref, vb_ref, kc_ref, vc_ref, sa_ref, sb_ref):
    tm = x_ref.shape[0]
    xn = _rms(x_ref[...], g_ref[...]).astype(BF16)
    cos = cos_ref[...]
    sin_signed = sin_ref[...]
    lane = lax.broadcasted_iota(jnp.int32, (1, LANES), 1)
    first_half = (lane % DIFF_DH) < (DIFF_DH // 2)
    n_chunks = q_ref.shape[1] // LANES
    q = _dot(xn, wq_ref[...])
    for c in range(n_chunks):
        sl = slice(c * LANES, (c + 1) * LANES)
        q_ref[:, sl] = (_rope_chunk(q[:, sl], cos, sin_signed, first_half) * (DIFF_DH ** -0.5 * LOG2E)).astype(BF16)
    k = _dot(xn, wk_ref[...])
    for c in range(n_chunks):
        sl = slice(c * LANES, (c + 1) * LANES)
        kr = _rope_chunk(k[:, sl], cos, sin_signed, first_half)
        kb_ref[:, sl] = kr.astype(BF16)
        kc_ref[pl.ds(2 * c, tm, stride=DIFF_QK_HEADS), :] = kr[:, :DIFF_DH]
        kc_ref[pl.ds(2 * c + 1, tm, stride=DIFF_QK_HEADS), :] = kr[:, DIFF_DH:]
    v = _dot(xn, wv_ref[...])
    vb_ref[...] = v.astype(BF16)
    for c in range(n_chunks):
        vc_ref[pl.ds(c, tm, stride=DIFF_HEADS), :] = v[:, c * LANES:(c + 1) * LANES]
    sa_ref[...] = _sigmoid(_dot(xn, wa_ref[...]))
    sb_ref[...] = _sigmoid(_dot(xn, wb_ref[...]))


def _diff_proj(x, g, cos, sin_signed, wq, wk, wv, wa, wb):
    n, d = x.shape
    tm = _row_tile(n, 256)
    row = lambda w: pl.BlockSpec((tm, w), lambda i: (i, 0))
    sds = lambda dt: jax.ShapeDtypeStruct((n, d), dt)
    return pl.pallas_call(
        _diff_proj_kernel,
        grid=(n // tm,),
        in_specs=[row(d), _full(g.shape), row(LANES), row(LANES)] + [_full(wq.shape)] * 5,
        out_specs=[row(d), row(d), row(d),
                   pl.BlockSpec((tm * DIFF_QK_HEADS, DIFF_DH), lambda i: (i, 0)),
                   pl.BlockSpec((tm * DIFF_HEADS, LANES), lambda i: (i, 0)),
                   row(d), row(d)],
        out_shape=[sds(BF16), sds(BF16), sds(BF16),
                   jax.ShapeDtypeStruct((n * DIFF_QK_HEADS, DIFF_DH), F32),
                   jax.ShapeDtypeStruct((n * DIFF_HEADS, LANES), F32),
                   sds(F32), sds(F32)],
        compiler_params=_cparams(("parallel",)),
        name="diff_proj",
    )(x, g, cos, sin_signed, wq, wk, wv, wa, wb)


def _gla_kernel(has_s0, *refs):
    if has_s0:
        q_ref, k_ref, v_ref, la_ref, r_ref, gn_ref, s0_ref, o_ref, sfin_ref, st_ref = refs
    else:
        q_ref, k_ref, v_ref, la_ref, r_ref, gn_ref, o_ref, sfin_ref, st_ref = refs
    n = pl.program_id(1)
    c = q_ref.shape[0]

    @pl.when(n == 0)
    def _():
        for h in range(GLA_HEADS):
            if has_s0:
                st_ref[h] = s0_ref[0, h].T
            else:
                st_ref[h] = jnp.zeros(st_ref.shape[1:], F32)

    row = lax.broadcasted_iota(jnp.int32, (c, c), 0)
    col = lax.broadcasted_iota(jnp.int32, (c, c), 1)
    causal = row >= col
    tri = jnp.where(causal, 1.0, 0.0).astype(BF16)
    gn = gn_ref[...]
    for h in range(GLA_HEADS):
        ks = slice(h * GLA_DK, (h + 1) * GLA_DK)
        vs = slice(h * GLA_DV, (h + 1) * GLA_DV)
        la = la_ref[:, ks]
        hi = la.astype(BF16)
        rem = la - hi.astype(F32)
        mid = rem.astype(BF16)
        lo = (rem - mid.astype(F32)).astype(BF16)
        bcum = _dot(tri, hi) + _dot(tri, mid) + _dot(tri, lo)
        qh = q_ref[:, ks]
        kh = k_ref[:, ks]
        vh = v_ref[:, vs]
        q_t = (qh * jnp.exp(bcum)).astype(BF16)
        k_t = (kh * jnp.exp(-bcum)).astype(BF16)
        scores = jnp.where(causal, _dot_nt(q_t, k_t), 0.0)
        st = st_ref[h]
        o = _dot_nt(q_t, st.astype(BF16)) + _dot(scores.astype(BF16), vh)
        b_last = bcum[c - 1:c, :]
        k_dec = (kh * jnp.exp(b_last - bcum)).astype(BF16)
        st_ref[h] = st * jnp.exp(b_last) + _dot_tn(vh, k_dec)
        rh = r_ref[:, vs]
        o_ref[:, vs] = (_rms(o, gn) * (rh * _sigmoid(rh))).astype(BF16)

    @pl.when(n == pl.num_programs(1) - 1)
    def _():
        for h in range(GLA_HEADS):
            sfin_ref[0, h] = st_ref[h].T


def _gla_chunks(q, k, v, la, r, gn, s0, batch, seq, chunk):
    nc = seq // chunk
    kd, vd = q.shape[1], v.shape[1]
    blk = lambda w: pl.BlockSpec((chunk, w), lambda b, n: (b * nc + n, 0))
    state_spec = pl.BlockSpec((1, GLA_HEADS, GLA_DK, GLA_DV), lambda b, n: (b, 0, 0, 0))
    in_specs = [blk(kd), blk(kd), blk(vd), blk(kd), blk(vd), _full(gn.shape)]
    args = [q, k, v, la, r, gn]
    if s0 is not None:
        in_specs.append(state_spec)
        args.append(s0)
    return pl.pallas_call(
        functools.partial(_gla_kernel, s0 is not None),
        grid=(batch, nc),
        in_specs=in_specs,
        out_specs=[blk(vd), state_spec],
        out_shape=[jax.ShapeDtypeStruct((batch * seq, vd), BF16),
                   jax.ShapeDtypeStruct((batch, GLA_HEADS, GLA_DK, GLA_DV), F32)],
        scratch_shapes=[pltpu.VMEM((GLA_HEADS, GLA_DV, GLA_DK), F32)],
        compiler_params=_cparams(("parallel", "arbitrary")),
        name="gla_chunks",
    )(*args)


def _lambda(lq1_ref, lk1_ref, lq2_ref, lk2_ref, lam_init):
    s1 = jnp.sum(lq1_ref[...] * lk1_ref[...], axis=-1, keepdims=True)
    s2 = jnp.sum(lq2_ref[...] * lk2_ref[...], axis=-1, keepdims=True)
    return jnp.exp(s1) - jnp.exp(s2) + lam_init


def _diff_prompt_kernel(lam_init, tk, sub, q_ref, k_ref, v_ref, lq1_ref, lk1_ref, lq2_ref, lk2_ref, sub_ref,
                        o_ref, m_ref, a_ref):
    qi = pl.program_id(2)
    tq = q_ref.shape[0]
    lane = lax.broadcasted_iota(jnp.int32, (1, LANES), 1)
    m_ref[...] = jnp.full(m_ref.shape, -jnp.inf, F32)
    a_ref[...] = jnp.zeros(a_ref.shape, F32)

    def scores(mp, r0, k0, nk, mask):
        q = q_ref[pl.ds(r0, sub), :]
        qm = jnp.where((lane < DIFF_DH) if mp == 0 else (lane >= DIFF_DH), q, jnp.zeros_like(q))
        s = _dot_nt(qm, k_ref[pl.ds(k0, nk), :])
        return s if mask is None else jnp.where(mask, s, -jnp.inf)

    def update(mp, r0, k0, nk, s):
        rows = pl.ds(r0, sub)
        n_chunks = nk // LANES
        mx = s[:, :LANES]
        for c in range(1, n_chunks):
            mx = jnp.maximum(mx, s[:, c * LANES:(c + 1) * LANES])
        m_old = m_ref[mp, rows, :]
        m_new = jnp.maximum(m_old, jnp.max(mx, axis=-1, keepdims=True))
        alpha = jnp.exp2(m_old - m_new)
        p = jnp.concatenate(
            [jnp.exp2(s[:, c * LANES:(c + 1) * LANES] - m_new).astype(BF16) for c in range(n_chunks)], axis=1)
        m_ref[mp, rows, :] = m_new
        vj = v_ref[pl.ds(k0, nk), :]
        v_ones = jnp.concatenate([vj, jnp.ones_like(vj)], axis=1)
        a_ref[mp, rows, :] = jnp.concatenate([alpha, alpha], axis=1) * a_ref[mp, rows, :] + _dot(p, v_ones)

    def run(chains):
        pending = {}
        for i in range(min(ATTN_LOOKAHEAD, len(chains))):
            pending[i] = scores(*chains[i])
        for i, (mp, r0, k0, nk, _) in enumerate(chains):
            if i + ATTN_LOOKAHEAD < len(chains):
                pending[i + ATTN_LOOKAHEAD] = scores(*chains[i + ATTN_LOOKAHEAD])
            update(mp, r0, k0, nk, pending.pop(i))

    n_sub = tq // sub

    def body(j, carry):
        k0 = pl.multiple_of(j * tk, tk)
        run([(mp, si * sub, k0, tk, None) for si in range(n_sub) for mp in range(2)])
        return carry

    lax.fori_loop(0, qi * (tq // tk), body, 0)

    base = pl.multiple_of(qi * tq, tq)
    chains = []
    for si in range(n_sub):
        need = (si + 1) * sub
        for k0 in range(0, need, tk):
            nk = min(tk, need - k0)
            mask = None
            if k0 + nk == need:
                row = lax.broadcasted_iota(jnp.int32, (sub, nk), 0) + si * sub
                col = lax.broadcasted_iota(jnp.int32, (sub, nk), 1) + k0
                mask = row >= col
            chains += [(mp, si * sub, base + k0, nk, mask) for mp in range(2)]
    run(chains)

    lam = _lambda(lq1_ref, lk1_ref, lq2_ref, lk2_ref, lam_init)
    o1 = a_ref[0, :, :LANES] * (1.0 / a_ref[0, :, LANES:])
    o2 = a_ref[1, :, :LANES] * (1.0 / a_ref[1, :, LANES:])
    o_ref[...] = (_rms(o1 - lam * o2, sub_ref[...]) * (1.0 - lam_init)).astype(BF16)


def _diff_prompt(q, k, v, lams, subln, batch, seq, lam_init):
    tq = min(ATTN_Q, seq)
    tk = min(ATTN_K, tq)
    sub = min(ATTN_SUB, tk)
    nq = seq // tq
    vec = _full(lams[0].shape)
    return pl.pallas_call(
        functools.partial(_diff_prompt_kernel, lam_init, tk, sub),
        grid=(batch, DIFF_HEADS, nq),
        in_specs=[pl.BlockSpec((tq, LANES), lambda b, h, i: (b * nq + i, h)),
                  pl.BlockSpec((seq, LANES), lambda b, h, i: (b, h)),
                  pl.BlockSpec((seq, LANES), lambda b, h, i: (b, h)),
                  vec, vec, vec, vec, _full(subln.shape)],
        out_specs=pl.BlockSpec((tq, LANES), lambda b, h, i: (b * nq + i, h)),
        out_shape=jax.ShapeDtypeStruct((batch * seq, DIFF_HEADS * LANES), BF16),
        scratch_shapes=[pltpu.VMEM((2, tq, LANES), F32), pltpu.VMEM((2, tq, 2 * LANES), F32)],
        compiler_params=_cparams(("parallel", "parallel", "arbitrary")),
        name="diff_prompt",
    )(q, k, v, *lams, subln)


def _diff_sample_kernel(lam_init, t_new, pt_ref, q_ref, kn_ref, vn_ref, lq1_ref, lk1_ref, lq2_ref, lk2_ref, sub_ref,
                        *rest):
    npg = PAGES_PER_STEP
    k_refs, v_refs = rest[:npg], rest[npg:2 * npg]
    o_ref, m_ref, l_ref, acc_ref = rest[2 * npg:]
    g = pl.program_id(1)
    half = t_new * DIFF_HEADS
    rows = 2 * half
    page_keys = v_refs[0].shape[1]

    @pl.when(g == 0)
    def _():
        m_ref[...] = jnp.full(m_ref.shape, -jnp.inf, F32)
        l_ref[...] = jnp.zeros(l_ref.shape, F32)
        acc_ref[...] = jnp.zeros(acc_ref.shape, F32)

    q = q_ref[0]

    def attend(s, v, valid):
        s = jnp.where(valid, s, -jnp.inf)
        m_old = m_ref[...]
        m_new = jnp.maximum(m_old, jnp.max(s, axis=-1, keepdims=True))
        p = jnp.exp2(s - m_new)
        alpha = jnp.exp2(m_old - m_new)
        l_ref[...] = alpha * l_ref[...] + jnp.sum(p, axis=-1, keepdims=True)
        acc_ref[...] = alpha * acc_ref[...] + _dot(p.astype(BF16), v)
        m_ref[...] = m_new

    def same_head(n_keys):
        r = lax.broadcasted_iota(jnp.int32, (rows, n_keys), 0)
        c = lax.broadcasted_iota(jnp.int32, (rows, n_keys), 1)
        return (r % DIFF_HEADS) == (c % DIFF_HEADS), r, c

    s_maps = []
    for j in range(2):
        kj = jnp.concatenate([kr[0, pl.ds(j, page_keys, stride=2), :].astype(BF16) for kr in k_refs], axis=0)
        s_maps.append(_dot_nt(q[j * half:(j + 1) * half], kj))
    vcat = jnp.concatenate([vr[0].astype(BF16) for vr in v_refs], axis=0)
    valid, _, _ = same_head(npg * page_keys)
    attend(jnp.concatenate(s_maps, axis=0), vcat, valid)

    @pl.when(g == pl.num_programs(1) - 1)
    def _():
        n_keys = vn_ref.shape[1]
        s_new = jnp.concatenate([_dot_nt(q[j * half:(j + 1) * half], kn_ref[0, j]) for j in range(2)], axis=0)
        valid, r, c = same_head(n_keys)
        tok = (r % half) // DIFF_HEADS
        attend(s_new, vn_ref[0], valid & (c // DIFF_HEADS <= tok))
        lam = _lambda(lq1_ref, lk1_ref, lq2_ref, lk2_ref, lam_init)
        an = acc_ref[...] * (1.0 / l_ref[...])
        o_ref[0] = _rms(an[:half] - lam * an[half:], sub_ref[...]) * (1.0 - lam_init)


def _diff_sample(page_table, q, k_new, v_new, lams, subln, cache_k, cache_v, lam_init, t_new):
    batch, rows, _ = q.shape
    n_pages = page_table.shape[1]
    npg = PAGES_PER_STEP
    assert n_pages % npg == 0
    vec = pl.BlockSpec(lams[0].shape, lambda b, g, pt: (0, 0))
    per_batch = lambda a: pl.BlockSpec((1,) + a.shape[1:], lambda b, g, pt: (b,) + (0,) * (a.ndim - 1))

    def page_spec(a, j):
        return pl.BlockSpec((1,) + a.shape[1:], lambda b, g, pt: (pt[b * n_pages + g * npg + j], 0, 0))

    grid_spec = pltpu.PrefetchScalarGridSpec(
        num_scalar_prefetch=1,
        grid=(batch, n_pages // npg),
        in_specs=[per_batch(q), per_batch(k_new), per_batch(v_new), vec, vec, vec, vec,
                  pl.BlockSpec(subln.shape, lambda b, g, pt: (0, 0))]
        + [page_spec(cache_k, j) for j in range(npg)] + [page_spec(cache_v, j) for j in range(npg)],
        out_specs=pl.BlockSpec((1, rows // 2, LANES), lambda b, g, pt: (b, 0, 0)),
        scratch_shapes=[pltpu.VMEM((rows, 1), F32), pltpu.VMEM((rows, 1), F32), pltpu.VMEM((rows, LANES), F32)],
    )
    return pl.pallas_call(
        functools.partial(_diff_sample_kernel, lam_init, t_new),
        grid_spec=grid_spec,
        out_shape=jax.ShapeDtypeStruct((batch, rows // 2, LANES), F32),
        compiler_params=_cparams(("parallel", "arbitrary")),
        name="diff_sample",
    )(page_table.reshape(-1), q, k_new, v_new, *lams, subln, *([cache_k] * npg), *([cache_v] * npg))


def _merge_route_kernel(x_ref, og_ref, od_ref, sa_ref, sb_ref, wg_ref, wd_ref, wo_ref, gf_ref, wr_ref, br_ref,
                        cin_ref, h_ref, hn_ref, idx_ref, gate_ref, rank_ref, cnt_ref, carry_ref):
    i = pl.program_id(0)
    tm = x_ref.shape[0]

    @pl.when(i == 0)
    def _():
        carry_ref[...] = cin_ref[...].astype(F32)

    merged = sa_ref[...] * _dot(og_ref[...], wg_ref[...]) + sb_ref[...] * _dot(od_ref[...], wd_ref[...])
    h = x_ref[...] + _dot(merged.astype(BF16), wo_ref[...])
    h_ref[...] = h
    hn = _rms(h, gf_ref[...]).astype(BF16)
    hn32 = hn.astype(F32)
    for s in range(hn_ref.shape[0] // tm):
        hn_ref[pl.ds(s, tm, stride=SUBLANES), :] = hn32[:, s * LANES:(s + 1) * LANES]
    lane = lax.broadcasted_iota(jnp.int32, (tm, LANES), 1)
    logits = jnp.where(lane < N_EXPERTS, _dot(hn, wr_ref[...]) + br_ref[...], -jnp.inf)
    idx_out = jnp.zeros((tm, LANES), jnp.int32)
    val_out = jnp.zeros((tm, LANES), F32)
    onehot = jnp.zeros((tm, LANES), F32)
    picks = []
    for kk in range(TOP_K):
        mx = jnp.max(logits, axis=-1, keepdims=True)
        pick = jnp.min(jnp.where(logits == mx, lane, LANES), axis=-1, keepdims=True)
        hit = lane == pick
        logits = jnp.where(hit, -jnp.inf, logits)
        onehot = jnp.where(hit, 1.0, onehot)
        idx_out = jnp.where(lane == kk, pick, idx_out)
        val_out = jnp.where(lane == kk, mx, val_out)
        picks.append(hit)
        if kk == 0:
            top = mx
    e = jnp.where(lane < TOP_K, jnp.exp(val_out - top), 0.0)
    gate_ref[...] = e * (1.0 / jnp.sum(e, axis=-1, keepdims=True))
    idx_ref[...] = idx_out
    row = lax.broadcasted_iota(jnp.int32, (tm, tm), 0)
    col = lax.broadcasted_iota(jnp.int32, (tm, tm), 1)
    strict = jnp.where(row > col, 1.0, 0.0).astype(BF16)
    before = _dot(strict, onehot.astype(BF16)) + carry_ref[...]
    rank_out = jnp.zeros((tm, LANES), jnp.int32)
    for kk in range(TOP_K):
        rk = jnp.sum(jnp.where(picks[kk], before, 0.0), axis=-1, keepdims=True)
        rank_out = jnp.where(lane == kk, rk.astype(jnp.int32), rank_out)
    rank_ref[...] = rank_out
    carry_ref[...] = carry_ref[...] + jnp.sum(onehot, axis=0, keepdims=True)
    cnt_ref[...] = carry_ref[...].astype(jnp.int32)


def _merge_route(x, og, od, sa, sb, wg, wd, wo, gf, wr, br, counts_in):
    n, d = x.shape
    tm = _row_tile(n, 256)
    row = lambda w: pl.BlockSpec((tm, w), lambda i: (i, 0))
    return pl.pallas_call(
        _merge_route_kernel,
        grid=(n // tm,),
        in_specs=[row(d)] * 5 + [_full(wg.shape), _full(wd.shape), _full(wo.shape), _full(gf.shape),
                                 _full(wr.shape), _full(br.shape), _full(counts_in.shape)],
        out_specs=[row(d), pl.BlockSpec((tm * d // LANES, LANES), lambda i: (i, 0)),
                   row(LANES), row(LANES), row(LANES), _full((1, LANES))],
        out_shape=[jax.ShapeDtypeStruct((n, d), F32), jax.ShapeDtypeStruct((n * d // LANES, LANES), F32),
                   jax.ShapeDtypeStruct((n, LANES), jnp.int32), jax.ShapeDtypeStruct((n, LANES), F32),
                   jax.ShapeDtypeStruct((n, LANES), jnp.int32), jax.ShapeDtypeStruct((1, LANES), jnp.int32)],
        scratch_shapes=[pltpu.VMEM((1, LANES), F32)],
        compiler_params=_cparams(("arbitrary",)),
        name="merge_route",
    )(x, og, od, sa, sb, wg, wd, wo, gf, wr, br, counts_in)


def _pair_shuffle(h):
    a, b = h[:, :LANES], h[:, LANES:]
    even = (lax.broadcasted_iota(jnp.int32, (1, LANES), 1) % 2) == 0
    glu = jnp.where(even, a, pltpu.roll(b, 1, 1))
    lin = jnp.where(even, pltpu.roll(a, LANES - 1, 1), b)
    return glu, lin


def _moe_kernel(be_ref, nu_ref, x_ref, wu_ref, bu_ref, wd_ref, bd_ref, y_ref, xs_ref):
    i = pl.program_id(0)
    bm, d = xs_ref.shape
    n_tiles = d // LANES

    @pl.when(i < nu_ref[0])
    def _():
        for s in range(n_tiles):
            xs_ref[:, s * LANES:(s + 1) * LANES] = x_ref[pl.ds(s, bm, stride=SUBLANES), :].astype(BF16)
        h = _dot(xs_ref[...], wu_ref[0]) + bu_ref[0]
        acts = []
        for c in range(h.shape[1] // (2 * LANES)):
            glu, lin = _pair_shuffle(h[:, 2 * c * LANES:2 * (c + 1) * LANES])
            x_glu = jnp.minimum(glu, SWIGLU_LIMIT)
            x_lin = jnp.clip(lin, -SWIGLU_LIMIT, SWIGLU_LIMIT)
            acts.append((x_glu * _sigmoid(SWIGLU_ALPHA * x_glu) * (x_lin + 1.0)).astype(BF16))
        y = _dot(jnp.concatenate(acts, axis=1), wd_ref[0]) + bd_ref[0]
        for s in range(n_tiles):
            y_ref[pl.ds(s, bm, stride=SUBLANES), :] = y[:, s * LANES:(s + 1) * LANES]

    @pl.when(i >= nu_ref[0])
    def _():
        y_ref[...] = jnp.zeros(y_ref.shape, F32)


def _moe_experts(blk_exp, n_used, xs, wu, bu, wd, bd, d):
    tiles = d // LANES
    n_rows = xs.shape[0] // tiles
    de2 = wu.shape[2]
    wspec = lambda a, b: pl.BlockSpec((1, a, b), lambda i, be, nu: (be[i], 0, 0))
    rows = pl.BlockSpec((MOE_ROWS * tiles, LANES), lambda i, be, nu: (i, 0))
    grid_spec = pltpu.PrefetchScalarGridSpec(
        num_scalar_prefetch=2,
        grid=(n_rows // MOE_ROWS,),
        in_specs=[rows, wspec(d, de2), wspec(1, de2), wspec(de2 // 2, d), wspec(1, d)],
        out_specs=rows,
        scratch_shapes=[pltpu.VMEM((MOE_ROWS, d), BF16)],
    )
    return pl.pallas_call(
        _moe_kernel,
        grid_spec=grid_spec,
        out_shape=jax.ShapeDtypeStruct(xs.shape, F32),
        compiler_params=_cparams(("arbitrary",)),
        name="moe_experts",
    )(blk_exp, n_used, xs, wu, bu, wd, bd)


def _combine_kernel(h_ref, gate_ref, y0_ref, y1_ref, y2_ref, y3_ref, gf_ref, o_ref):
    tm, d = h_ref.shape
    gate = gate_ref[...]
    parts = []
    for s in range(d // LANES):
        acc = h_ref[:, s * LANES:(s + 1) * LANES]
        for kk, y_ref in enumerate((y0_ref, y1_ref, y2_ref, y3_ref)):
            acc = acc + gate[:, kk:kk + 1] * y_ref[0, pl.ds(s, tm, stride=SUBLANES), :]
        parts.append(acc)
    o_ref[...] = _rms(jnp.concatenate(parts, axis=1), gf_ref[...])


def _combine(h, gate, yg, gf, tok0):
    n, d = h.shape
    tm = _row_tile(n, 256)
    assert tok0 % tm == 0
    b0 = tok0 // tm
    tiles = d // LANES
    row = lambda w: pl.BlockSpec((tm, w), lambda i: (i, 0))
    ysp = lambda kk: pl.BlockSpec((1, tm * tiles, LANES), lambda i: (kk, b0 + i, 0))
    return pl.pallas_call(
        _combine_kernel,
        grid=(n // tm,),
        in_specs=[row(d), row(LANES)] + [ysp(kk) for kk in range(TOP_K)] + [_full(gf.shape)],
        out_specs=row(d),
        out_shape=jax.ShapeDtypeStruct((n, d), F32),
        compiler_params=_cparams(("parallel",)),
        name="combine",
    )(h, gate, yg, yg, yg, yg, gf)


def _rope_tables(pos):
    half = DIFF_DH // 2
    inv_freq = ROPE_THETA ** (-jnp.arange(half, dtype=F32) / half)
    ang = pos.astype(F32)[:, None] * inv_freq[None, :]
    cos, sin = jnp.cos(ang), jnp.sin(ang)
    reps = LANES // DIFF_DH
    return jnp.tile(jnp.concatenate([cos, cos], axis=1), (1, reps)), jnp.tile(jnp.concatenate([-sin, sin], axis=1), (1, reps))


def kernel(x_prompt, x_sample, cache_k, cache_v, state_gla, page_table, norm_attn, w_in, gla_gate_up, gla_gate_bias, gla_norm, lam_q1, lam_k1, lam_q2, lam_k2, diff_subln, w_branch_gla, w_branch_diff, w_out, norm_ffn, w_router, b_router, w_up, b_up, w_down, b_down, norm_final):
    depth = w_in.shape[0]
    assert depth == 1, "single-layer step"
    layer = 0
    b_p, t_p, d = x_prompt.shape
    b_s, t_s, _ = x_sample.shape
    n_p, n_s = b_p * t_p, b_s * t_s
    n = n_p + n_s
    n_pool, page = cache_k.shape[1], cache_k.shape[2]
    n_pages = page_table.shape[1]
    past_len = n_pages * page
    kd, vd = GLA_HEADS * GLA_DK, GLA_HEADS * GLA_DV
    tiles = d // LANES
    row2 = lambda a: a.reshape(1, -1)
    assert t_s <= DIFF_HEADS and t_s * DIFF_HEADS <= LANES

    splits = (kd, kd, vd, GLA_RANK, vd, d, d, d, d, d)
    offs = [0]
    for s in splits:
        offs.append(offs[-1] + s)
    wb = w_in.reshape(d, -1).astype(BF16)
    w_gq, w_gk, w_gv, w_glr, w_gr, w_dq, w_dk, w_dv, w_ga, w_gb = [wb[:, offs[j]:offs[j + 1]] for j in range(10)]
    w_glr = jnp.pad(w_glr, ((0, 0), (0, LANES - GLA_RANK)))
    gup = jnp.pad(gla_gate_up.reshape(GLA_RANK, kd).astype(BF16), ((0, LANES - GLA_RANK), (0, 0)))
    g_attn = row2(norm_attn)
    gn = row2(gla_norm)
    lam_init = 0.8 - 0.6 * math.exp(-0.3 * layer)
    lams = (row2(lam_q1), row2(lam_k1), row2(lam_q2), row2(lam_k2))
    subln = row2(diff_subln)
    w_bg = w_branch_gla.reshape(vd, d).astype(BF16)
    w_bd = w_branch_diff.reshape(d, d).astype(BF16)
    w_o = w_out.reshape(d, d).astype(BF16)
    w_r = jnp.pad(w_router.reshape(d, N_EXPERTS).astype(BF16), ((0, 0), (0, LANES - N_EXPERTS)))
    b_r = jnp.pad(row2(b_router), ((0, 0), (0, LANES - N_EXPERTS)))
    g_ffn = row2(norm_ffn)

    def mixers(x, pos, carry):
        gq, gk, gv, gla_la, gr = _gla_proj(x, g_attn, w_gq, w_gk, w_gv, w_glr, w_gr, gup, row2(gla_gate_bias))
        cos, sin_signed = _rope_tables(pos)
        return (gq, gk, gv, gla_la, gr) + tuple(_diff_proj(x, g_attn, cos, sin_signed, w_dq, w_dk, w_dv, w_ga, w_gb))

    x_p = x_prompt.reshape(n_p, d)
    gq, gk, gv, gla_la, gr, dq, dk_b, dv_b, k_cache_p, v_cache_p, sig_a, sig_b = mixers(
        x_p, jnp.tile(jnp.arange(t_p, dtype=jnp.int32), b_p), None)
    og_p, s_p = _gla_chunks(gq, gk, gv, gla_la, gr, gn, None, b_p, t_p, GLA_CHUNK if t_p % GLA_CHUNK == 0 else t_p)
    od_p = _diff_prompt(dq, dk_b, dv_b, lams, subln, b_p, t_p, lam_init)
    zero_counts = jnp.zeros((1, LANES), jnp.int32)
    h_p, hn_p, idx_p, gate_p, rank_p, counts_p = _merge_route(
        x_p, og_p, od_p, sig_a, sig_b, w_bg, w_bd, w_o, g_ffn, w_r, b_r, zero_counts)

    x_s = x_sample.reshape(n_s, d)
    gq, gk, gv, gla_la, gr, dq, dk_b, dv_b, k_cache_s, v_cache_s, sig_a, sig_b = mixers(
        x_s, jnp.tile(past_len + jnp.arange(t_s, dtype=jnp.int32), b_s), None)
    t_pad = -(-t_s // GLA_CHUNK) * GLA_CHUNK
    pad_s = lambda a: jnp.pad(a.reshape(b_s, t_s, -1), ((0, 0), (0, t_pad - t_s), (0, 0))).reshape(b_s * t_pad, -1)
    og_s, s_s = _gla_chunks(pad_s(gq), pad_s(gk), pad_s(gv), pad_s(gla_la), pad_s(gr), gn,
                            state_gla.reshape(b_s, GLA_HEADS, GLA_DK, GLA_DV), b_s, t_pad, t_pad)
    og_s = og_s.reshape(b_s, t_pad, vd)[:, :t_s].reshape(n_s, vd)
    q_s = dq.reshape(b_s, t_s, DIFF_HEADS, 2, DIFF_DH).transpose(0, 3, 1, 2, 4).reshape(b_s, 2 * t_s * DIFF_HEADS, DIFF_DH)
    kn = dk_b.reshape(b_s, t_s, DIFF_HEADS, 2, DIFF_DH).transpose(0, 3, 1, 2, 4).reshape(b_s, 2, t_s * DIFF_HEADS, DIFF_DH)
    kn = jnp.pad(kn, ((0, 0), (0, 0), (0, LANES - t_s * DIFF_HEADS), (0, 0)))
    vn = jnp.pad(dv_b.reshape(b_s, t_s * DIFF_HEADS, LANES), ((0, 0), (0, LANES - t_s * DIFF_HEADS), (0, 0)))
    od_s = _diff_sample(page_table, q_s, kn, vn, lams, subln,
                        cache_k.reshape(depth * n_pool, page * DIFF_QK_HEADS, DIFF_DH),
                        cache_v.reshape(depth * n_pool, page * DIFF_HEADS, LANES), lam_init, t_s)
    od_s = od_s.reshape(n_s, d).astype(BF16)
    h_s, hn_s, idx_s, gate_s, rank_s, counts = _merge_route(
        x_s, og_s, od_s, sig_a, sig_b, w_bg, w_bd, w_o, g_ffn, w_r, b_r, counts_p)

    counts = counts[0, :N_EXPERTS]
    padded = (counts + MOE_ROWS - 1) // MOE_ROWS * MOE_ROWS
    pad_end = jnp.cumsum(padded)
    pad_start = pad_end - padded
    idx4 = jnp.concatenate([idx_p[:, :TOP_K], idx_s[:, :TOP_K]], axis=0)
    rank4 = jnp.concatenate([rank_p[:, :TOP_K], rank_s[:, :TOP_K]], axis=0)
    dest = pad_start[idx4] + rank4
    n_rows = -(-(n * TOP_K + N_EXPERTS * (MOE_ROWS - 1)) // MOE_ROWS) * MOE_ROWS
    n_blocks = n_rows // MOE_ROWS
    blk_start = jnp.arange(n_blocks, dtype=jnp.int32) * MOE_ROWS
    blk_exp = jnp.minimum(jnp.sum((blk_start[:, None] >= pad_end[None, :]).astype(jnp.int32), axis=1), N_EXPERTS - 1)
    n_used = (pad_end[-1:] // MOE_ROWS).astype(jnp.int32)
    row_tok = jnp.zeros((n_rows,), jnp.int32).at[dest.reshape(-1)].set(
        jnp.repeat(jnp.arange(n, dtype=jnp.int32), TOP_K))
    hn_all = jnp.concatenate([hn_p, hn_s], axis=0).reshape(n, tiles, LANES)
    xs = jnp.take(hn_all, row_tok, axis=0).reshape(n_rows * tiles, LANES)

    de = w_down.shape[2]
    w_u = w_up.reshape(N_EXPERTS, d, 2 * de).astype(BF16)
    b_u = b_up.reshape(N_EXPERTS, 1, 2 * de)
    half = LANES // 2
    w_d = w_down.reshape(N_EXPERTS, de // LANES, 2, half, d).transpose(0, 1, 3, 2, 4).reshape(N_EXPERTS, de, d).astype(BF16)
    y = _moe_experts(blk_exp, n_used, xs, w_u, b_u, w_d, b_down.reshape(N_EXPERTS, 1, d), d)
    yg = jnp.take(y.reshape(n_rows, tiles, LANES), dest.T.reshape(-1), axis=0).reshape(TOP_K, n * tiles, LANES)

    g_fin = row2(norm_final)
    y_prompt = _combine(h_p, gate_p, yg, g_fin, 0).reshape(b_p, t_p, d)
    y_sample = _combine(h_s, gate_s, yg, g_fin, n_p).reshape(b_s, t_s, d)
    k_prompt = k_cache_p.reshape(1, b_p, t_p, DIFF_QK_HEADS, DIFF_DH)
    v_prompt = v_cache_p.reshape(1, b_p, t_p, DIFF_HEADS, 2 * DIFF_DH)
    k_sample = k_cache_s.reshape(1, b_s, t_s, DIFF_QK_HEADS, DIFF_DH)
    v_sample = v_cache_s.reshape(1, b_s, t_s, DIFF_HEADS, 2 * DIFF_DH)
    return (y_prompt, y_sample, k_prompt, v_prompt, s_p[None], k_sample, v_sample, s_s[None])
```

```python
import functools
import math

import jax
import jax.numpy as jnp
from jax import lax
from jax.experimental import pallas as pl
from jax.experimental.pallas import tpu as pltpu

F32 = jnp.float32
BF16 = jnp.bfloat16

EPS = 1e-6
GLA_HEADS = 4
GLA_DK = 128
GLA_DV = 256
GLA_RANK = 16
GLA_NORMALIZER = 16.0
GLA_CHUNK = 64
DIFF_HEADS = 8
DIFF_QK_HEADS = 2 * DIFF_HEADS
DIFF_DH = 64
ROPE_THETA = 10000.0
N_EXPERTS = 32
TOP_K = 4
SWIGLU_ALPHA = 1.702
SWIGLU_LIMIT = 7.0
LOG2E = math.log2(math.e)

LANES = 128
SUBLANES = 8
MOE_ROWS = 256
ATTN_Q = 1024
ATTN_K = 512
ATTN_SUB = 256
ATTN_LOOKAHEAD = 3
PAGES_PER_STEP = 4
VMEM_LIMIT = 48 * 1024 * 1024


def _cparams(sem):
    return pltpu.CompilerParams(dimension_semantics=sem, vmem_limit_bytes=VMEM_LIMIT)


def _dot(a, b):
    return jnp.dot(a, b, preferred_element_type=F32)


def _dot_nt(a, b):
    return lax.dot_general(a, b, (((1,), (1,)), ((), ())), preferred_element_type=F32)


def _dot_tn(a, b):
    return lax.dot_general(a, b, (((0,), (0,)), ((), ())), preferred_element_type=F32)


def _rms(x, g):
    ms = jnp.mean(x * x, axis=-1, keepdims=True)
    return x * lax.rsqrt(ms + EPS) * g


def _sigmoid(x):
    return 1.0 / (1.0 + jnp.exp(-x))


def _row_tile(n, cap=512):
    for t in (512, 256, 128, 64, 32, 16, 8):
        if t <= cap and n % t == 0:
            return t
    raise ValueError(f"token count {n} is not a multiple of 8")


def _full(shape):
    return pl.BlockSpec(shape, lambda *_: (0,) * len(shape))


def _gla_proj_kernel(x_ref, g_ref, wq_ref, wk_ref, wv_ref, wlr_ref, wr_ref, gup_ref, gb_ref,
                     q_ref, k_ref, v_ref, la_ref, r_ref):
    xn = _rms(x_ref[...], g_ref[...]).astype(BF16)
    q_ref[...] = _dot(xn, wq_ref[...]) * (GLA_DK ** -0.5)
    k_ref[...] = _dot(xn, wk_ref[...])
    v_ref[...] = _dot(xn, wv_ref[...]).astype(BF16)
    glr = _dot(xn, wlr_ref[...]).astype(BF16)
    z = _dot(glr, gup_ref[...]) + gb_ref[...]
    log_sig = jnp.minimum(z, 0.0) - jnp.log1p(jnp.exp(-jnp.abs(z)))
    la_ref[...] = log_sig * (1.0 / GLA_NORMALIZER)
    r_ref[...] = _dot(xn, wr_ref[...])


def _gla_proj(x, g, wq, wk, wv, wlr, wr, gup, gb):
    n, d = x.shape
    tm = _row_tile(n)
    kd, vd = wq.shape[1], wv.shape[1]
    row = lambda w: pl.BlockSpec((tm, w), lambda i: (i, 0))
    return pl.pallas_call(
        _gla_proj_kernel,
        grid=(n // tm,),
        in_specs=[row(d), _full(g.shape), _full(wq.shape), _full(wk.shape), _full(wv.shape),
                  _full(wlr.shape), _full(wr.shape), _full(gup.shape), _full(gb.shape)],
        out_specs=[row(kd), row(kd), row(vd), row(kd), row(vd)],
        out_shape=[jax.ShapeDtypeStruct((n, kd), F32), jax.ShapeDtypeStruct((n, kd), F32),
                   jax.ShapeDtypeStruct((n, vd), BF16), jax.ShapeDtypeStruct((n, kd), F32),
                   jax.ShapeDtypeStruct((n, vd), F32)],
        compiler_params=_cparams(("parallel",)),
        name="gla_proj",
    )(x, g, wq, wk, wv, wlr, wr, gup, gb)


def _rope_chunk(xc, cos, sin_signed, first_half):
    swapped = jnp.where(first_half, pltpu.roll(xc, LANES - DIFF_DH // 2, 1), pltpu.roll(xc, DIFF_DH // 2, 1))
    return xc * cos + swapped * sin_signed


def _diff_proj_kernel(x_ref, g_ref, cos_ref, sin_ref, wq_ref, wk_ref, wv_ref, wa_ref, wb_ref,
                      q_ref, kb_ref, vb_ref, kt_ref, vc_ref, sa_ref, sb_ref):
    tm = x_ref.shape[0]
    xn = _rms(x_ref[...], g_ref[...]).astype(BF16)
    cos = cos_ref[...]
    sin_signed = sin_ref[...]
    lane = lax.broadcasted_iota(jnp.int32, (1, LANES), 1)
    first_half = (lane % DIFF_DH) < (DIFF_DH // 2)
    n_chunks = q_ref.shape[1] // LANES
    q = _dot(xn, wq_ref[...])
    for c in range(n_chunks):
        sl = slice(c * LANES, (c + 1) * LANES)
        q_ref[:, sl] = (_rope_chunk(q[:, sl], cos, sin_signed, first_half) * (DIFF_DH ** -0.5 * LOG2E)).astype(BF16)
    k = _dot(xn, wk_ref[...])
    for c in range(n_chunks):
        sl = slice(c * LANES, (c + 1) * LANES)
        kr = _rope_chunk(k[:, sl], cos, sin_signed, first_half)
        kb_ref[:, sl] = kr.astype(BF16)
        kt_ref[0, sl, :] = kr.T
    v = _dot(xn, wv_ref[...])
    vb_ref[...] = v.astype(BF16)
    for c in range(n_chunks):
        vc_ref[pl.ds(c, tm, stride=DIFF_HEADS), :] = v[:, c * LANES:(c + 1) * LANES]
    sa_ref[...] = _sigmoid(_dot(xn, wa_ref[...]))
    sb_ref[...] = _sigmoid(_dot(xn, wb_ref[...]))


def _diff_proj(x, g, cos, sin_signed, wq, wk, wv, wa, wb, seq):
    n, d = x.shape
    tm = _row_tile(seq, 256)
    tps = seq // tm
    row = lambda w: pl.BlockSpec((tm, w), lambda i: (i, 0))
    sds = lambda dt: jax.ShapeDtypeStruct((n, d), dt)
    return pl.pallas_call(
        _diff_proj_kernel,
        grid=(n // tm,),
        in_specs=[row(d), _full(g.shape), row(LANES), row(LANES)] + [_full(wq.shape)] * 5,
        out_specs=[row(d), row(d), row(d),
                   pl.BlockSpec((1, d, tm), lambda i: (i // tps, 0, i % tps)),
                   pl.BlockSpec((tm * DIFF_HEADS, LANES), lambda i: (i, 0)),
                   row(d), row(d)],
        out_shape=[sds(BF16), sds(BF16), sds(BF16),
                   jax.ShapeDtypeStruct((n // seq, d, seq), F32),
                   jax.ShapeDtypeStruct((n * DIFF_HEADS, LANES), F32),
                   sds(F32), sds(F32)],
        compiler_params=_cparams(("parallel",)),
        name="diff_proj",
    )(x, g, cos, sin_signed, wq, wk, wv, wa, wb)


def _gla_kernel(has_s0, *refs):
    if has_s0:
        q_ref, k_ref, v_ref, la_ref, r_ref, gn_ref, s0_ref, o_ref, sfin_ref, st_ref = refs
    else:
        q_ref, k_ref, v_ref, la_ref, r_ref, gn_ref, o_ref, sfin_ref, st_ref = refs
    n = pl.program_id(1)
    c = q_ref.shape[0]

    @pl.when(n == 0)
    def _():
        for h in range(GLA_HEADS):
            if has_s0:
                st_ref[h] = s0_ref[0, h].T
            else:
                st_ref[h] = jnp.zeros(st_ref.shape[1:], F32)

    row = lax.broadcasted_iota(jnp.int32, (c, c), 0)
    col = lax.broadcasted_iota(jnp.int32, (c, c), 1)
    causal = row >= col
    tri = jnp.where(causal, 1.0, 0.0).astype(BF16)
    gn = gn_ref[...]
    for h in range(GLA_HEADS):
        ks = slice(h * GLA_DK, (h + 1) * GLA_DK)
        vs = slice(h * GLA_DV, (h + 1) * GLA_DV)
        la = la_ref[:, ks]
        hi = la.astype(BF16)
        rem = la - hi.astype(F32)
        mid = rem.astype(BF16)
        lo = (rem - mid.astype(F32)).astype(BF16)
        bcum = _dot(tri, hi) + _dot(tri, mid) + _dot(tri, lo)
        qh = q_ref[:, ks]
        kh = k_ref[:, ks]
        vh = v_ref[:, vs]
        q_t = (qh * jnp.exp(bcum)).astype(BF16)
        k_t = (kh * jnp.exp(-bcum)).astype(BF16)
        scores = jnp.where(causal, _dot_nt(q_t, k_t), 0.0)
        st = st_ref[h]
        o = _dot_nt(q_t, st.astype(BF16)) + _dot(scores.astype(BF16), vh)
        b_last = bcum[c - 1:c, :]
        k_dec = (kh * jnp.exp(b_last - bcum)).astype(BF16)
        st_ref[h] = st * jnp.exp(b_last) + _dot_tn(vh, k_dec)
        rh = r_ref[:, vs]
        o_ref[:, vs] = (_rms(o, gn) * (rh * _sigmoid(rh))).astype(BF16)

    @pl.when(n == pl.num_programs(1) - 1)
    def _():
        for h in range(GLA_HEADS):
            sfin_ref[0, h] = st_ref[h].T


def _gla_chunks(q, k, v, la, r, gn, s0, batch, seq, chunk):
    nc = seq // chunk
    kd, vd = q.shape[1], v.shape[1]
    blk = lambda w: pl.BlockSpec((chunk, w), lambda b, n: (b * nc + n, 0))
    state_spec = pl.BlockSpec((1, GLA_HEADS, GLA_DK, GLA_DV), lambda b, n: (b, 0, 0, 0))
    in_specs = [blk(kd), blk(kd), blk(vd), blk(kd), blk(vd), _full(gn.shape)]
    args = [q, k, v, la, r, gn]
    if s0 is not None:
        in_specs.append(state_spec)
        args.append(s0)
    return pl.pallas_call(
        functools.partial(_gla_kernel, s0 is not None),
        grid=(batch, nc),
        in_specs=in_specs,
        out_specs=[blk(vd), state_spec],
        out_shape=[jax.ShapeDtypeStruct((batch * seq, vd), BF16),
                   jax.ShapeDtypeStruct((batch, GLA_HEADS, GLA_DK, GLA_DV), F32)],
        scratch_shapes=[pltpu.VMEM((GLA_HEADS, GLA_DV, GLA_DK), F32)],
        compiler_params=_cparams(("parallel", "arbitrary")),
        name="gla_chunks",
    )(*args)


def _lambda(lq1_ref, lk1_ref, lq2_ref, lk2_ref, lam_init):
    s1 = jnp.sum(lq1_ref[...] * lk1_ref[...], axis=-1, keepdims=True)
    s2 = jnp.sum(lq2_ref[...] * lk2_ref[...], axis=-1, keepdims=True)
    return jnp.exp(s1) - jnp.exp(s2) + lam_init


def _diff_prompt_kernel(lam_init, tk, sub, q_ref, k_ref, v_ref, lq1_ref, lk1_ref, lq2_ref, lk2_ref, sub_ref,
                        o_ref, m_ref, a_ref):
    qi = pl.program_id(2)
    tq = q_ref.shape[0]
    lane = lax.broadcasted_iota(jnp.int32, (1, LANES), 1)
    m_ref[...] = jnp.full(m_ref.shape, -jnp.inf, F32)
    a_ref[...] = jnp.zeros(a_ref.shape, F32)

    def scores(mp, r0, k0, nk, mask):
        q = q_ref[pl.ds(r0, sub), :]
        qm = jnp.where((lane < DIFF_DH) if mp == 0 else (lane >= DIFF_DH), q, jnp.zeros_like(q))
        s = _dot_nt(qm, k_ref[pl.ds(k0, nk), :])
        return s if mask is None else jnp.where(mask, s, -jnp.inf)

    def update(mp, r0, k0, nk, s):
        rows = pl.ds(r0, sub)
        n_chunks = nk // LANES
        mx = s[:, :LANES]
        for c in range(1, n_chunks):
            mx = jnp.maximum(mx, s[:, c * LANES:(c + 1) * LANES])
        m_old = m_ref[mp, rows, :]
        m_new = jnp.maximum(m_old, jnp.max(mx, axis=-1, keepdims=True))
        alpha = jnp.exp2(m_old - m_new)
        p = jnp.concatenate(
            [jnp.exp2(s[:, c * LANES:(c + 1) * LANES] - m_new).astype(BF16) for c in range(n_chunks)], axis=1)
        m_ref[mp, rows, :] = m_new
        vj = v_ref[pl.ds(k0, nk), :]
        v_ones = jnp.concatenate([vj, jnp.ones_like(vj)], axis=1)
        a_ref[mp, rows, :] = jnp.concatenate([alpha, alpha], axis=1) * a_ref[mp, rows, :] + _dot(p, v_ones)

    def run(chains):
        pending = {}
        for i in range(min(ATTN_LOOKAHEAD, len(chains))):
            pending[i] = scores(*chains[i])
        for i, (mp, r0, k0, nk, _) in enumerate(chains):
            if i + ATTN_LOOKAHEAD < len(chains):
                pending[i + ATTN_LOOKAHEAD] = scores(*chains[i + ATTN_LOOKAHEAD])
            update(mp, r0, k0, nk, pending.pop(i))

    n_sub = tq // sub

    def body(j, carry):
        k0 = pl.multiple_of(j * tk, tk)
        run([(mp, si * sub, k0, tk, None) for si in range(n_sub) for mp in range(2)])
        return carry

    lax.fori_loop(0, qi * (tq // tk), body, 0)

    base = pl.multiple_of(qi * tq, tq)
    chains = []
    for si in range(n_sub):
        need = (si + 1) * sub
        for k0 in range(0, need, tk):
            nk = min(tk, need - k0)
            mask = None
            if k0 + nk == need:
                row = lax.broadcasted_iota(jnp.int32, (sub, nk), 0) + si * sub
                col = lax.broadcasted_iota(jnp.int32, (sub, nk), 1) + k0
                mask = row >= col
            chains += [(mp, si * sub, base + k0, nk, mask) for mp in range(2)]
    run(chains)

    lam = _lambda(lq1_ref, lk1_ref, lq2_ref, lk2_ref, lam_init)
    o1 = a_ref[0, :, :LANES] * (1.0 / a_ref[0, :, LANES:])
    o2 = a_ref[1, :, :LANES] * (1.0 / a_ref[1, :, LANES:])
    o_ref[...] = (_rms(o1 - lam * o2, sub_ref[...]) * (1.0 - lam_init)).astype(BF16)


def _diff_prompt(q, k, v, lams, subln, batch, seq, lam_init):
    tq = min(ATTN_Q, seq)
    tk = min(ATTN_K, tq)
    sub = min(ATTN_SUB, tk)
    nq = seq // tq
    vec = _full(lams[0].shape)
    return pl.pallas_call(
        functools.partial(_diff_prompt_kernel, lam_init, tk, sub),
        grid=(batch, DIFF_HEADS, nq),
        in_specs=[pl.BlockSpec((tq, LANES), lambda b, h, i: (b * nq + i, h)),
                  pl.BlockSpec((seq, LANES), lambda b, h, i: (b, h)),
                  pl.BlockSpec((seq, LANES), lambda b, h, i: (b, h)),
                  vec, vec, vec, vec, _full(subln.shape)],
        out_specs=pl.BlockSpec((tq, LANES), lambda b, h, i: (b * nq + i, h)),
        out_shape=jax.ShapeDtypeStruct((batch * seq, DIFF_HEADS * LANES), BF16),
        scratch_shapes=[pltpu.VMEM((2, tq, LANES), F32), pltpu.VMEM((2, tq, 2 * LANES), F32)],
        compiler_params=_cparams(("parallel", "parallel", "arbitrary")),
        name="diff_prompt",
    )(q, k, v, *lams, subln)


def _diff_sample_kernel(lam_init, t_new, pt_ref, q_ref, kn_ref, vn_ref, lq1_ref, lk1_ref, lq2_ref, lk2_ref, sub_ref,
                        *rest):
    npg = PAGES_PER_STEP
    k_refs, v_refs = rest[:npg], rest[npg:2 * npg]
    o_ref, m_ref, l_ref, acc_ref = rest[2 * npg:]
    g = pl.program_id(1)
    rows = q_ref.shape[1]
    grp = rows // DIFF_HEADS
    page = k_refs[0].shape[2]

    @pl.when(g == 0)
    def _():
        m_ref[...] = jnp.full(m_ref.shape, -jnp.inf, F32)
        l_ref[...] = jnp.zeros(l_ref.shape, F32)
        acc_ref[...] = jnp.zeros(acc_ref.shape, F32)

    q = q_ref[0]

    def attend(s, v_of_head):
        m_old = m_ref[...]
        m_new = jnp.maximum(m_old, jnp.max(s, axis=-1, keepdims=True))
        p = jnp.exp2(s - m_new)
        alpha = jnp.exp2(m_old - m_new)
        l_ref[...] = alpha * l_ref[...] + jnp.sum(p, axis=-1, keepdims=True)
        pv = [_dot(p[h * grp:(h + 1) * grp].astype(BF16), v_of_head(h)) for h in range(DIFF_HEADS)]
        acc_ref[...] = alpha * acc_ref[...] + jnp.concatenate(pv, axis=0)
        m_ref[...] = m_new

    kcat = jnp.concatenate([kr[0].astype(BF16) for kr in k_refs], axis=1)
    attend(_dot(q, kcat), lambda h: jnp.concatenate(
        [vr[0, pl.ds(h, page, stride=DIFF_HEADS), :].astype(BF16) for vr in v_refs], axis=0))

    @pl.when(g == pl.num_programs(1) - 1)
    def _():
        s_new = _dot(q, kn_ref[0])
        r = lax.broadcasted_iota(jnp.int32, s_new.shape, 0)
        c = lax.broadcasted_iota(jnp.int32, s_new.shape, 1)
        attend(jnp.where(c <= r % t_new, s_new, -jnp.inf), lambda h: vn_ref[0, h])
        lam = _lambda(lq1_ref, lk1_ref, lq2_ref, lk2_ref, lam_init)
        an = acc_ref[...] * (1.0 / l_ref[...])
        diff = an - lam * pltpu.roll(an, rows - t_new, 0)
        o_ref[0] = _rms(diff, sub_ref[...]) * (1.0 - lam_init)


def _diff_sample(page_table, q, k_new, v_new, lams, subln, cache_kt, cache_v, lam_init, t_new):
    batch, rows, _ = q.shape
    n_pages = page_table.shape[1]
    npg = PAGES_PER_STEP
    assert n_pages % npg == 0
    vec = pl.BlockSpec(lams[0].shape, lambda b, g, pt: (0, 0))
    per_batch = lambda a: pl.BlockSpec((1,) + a.shape[1:], lambda b, g, pt: (b,) + (0,) * (a.ndim - 1))

    def page_spec(a, j):
        return pl.BlockSpec((1,) + a.shape[1:], lambda b, g, pt: (pt[b * n_pages + g * npg + j], 0, 0))

    grid_spec = pltpu.PrefetchScalarGridSpec(
        num_scalar_prefetch=1,
        grid=(batch, n_pages // npg),
        in_specs=[per_batch(q), per_batch(k_new), per_batch(v_new), vec, vec, vec, vec,
                  pl.BlockSpec(subln.shape, lambda b, g, pt: (0, 0))]
        + [page_spec(cache_kt, j) for j in range(npg)] + [page_spec(cache_v, j) for j in range(npg)],
        out_specs=pl.BlockSpec((1, rows, LANES), lambda b, g, pt: (b, 0, 0)),
        scratch_shapes=[pltpu.VMEM((rows, 1), F32), pltpu.VMEM((rows, 1), F32), pltpu.VMEM((rows, LANES), F32)],
    )
    return pl.pallas_call(
        functools.partial(_diff_sample_kernel, lam_init, t_new),
        grid_spec=grid_spec,
        out_shape=jax.ShapeDtypeStruct((batch, rows, LANES), F32),
        compiler_params=_cparams(("parallel", "arbitrary")),
        name="diff_sample",
    )(page_table.reshape(-1), q, k_new, v_new, *lams, subln, *([cache_kt] * npg), *([cache_v] * npg))


def _merge_route_kernel(x_ref, og_ref, od_ref, sa_ref, sb_ref, wg_ref, wd_ref, wo_ref, gf_ref, wr_ref, br_ref,
                        cin_ref, h_ref, hn_ref, idx_ref, gate_ref, rank_ref, cnt_ref, carry_ref):
    i = pl.program_id(0)
    tm = x_ref.shape[0]

    @pl.when(i == 0)
    def _():
        carry_ref[...] = cin_ref[...].astype(F32)

    merged = sa_ref[...] * _dot(og_ref[...], wg_ref[...]) + sb_ref[...] * _dot(od_ref[...], wd_ref[...])
    h = x_ref[...] + _dot(merged.astype(BF16), wo_ref[...])
    h_ref[...] = h
    hn = _rms(h, gf_ref[...]).astype(BF16)
    hn32 = hn.astype(F32)
    for s in range(hn_ref.shape[0] // tm):
        hn_ref[pl.ds(s, tm, stride=SUBLANES), :] = hn32[:, s * LANES:(s + 1) * LANES]
    lane = lax.broadcasted_iota(jnp.int32, (tm, LANES), 1)
    logits = jnp.where(lane < N_EXPERTS, _dot(hn, wr_ref[...]) + br_ref[...], -jnp.inf)
    idx_out = jnp.zeros((tm, LANES), jnp.int32)
    val_out = jnp.zeros((tm, LANES), F32)
    onehot = jnp.zeros((tm, LANES), F32)
    picks = []
    for kk in range(TOP_K):
        mx = jnp.max(logits, axis=-1, keepdims=True)
        pick = jnp.min(jnp.where(logits == mx, lane, LANES), axis=-1, keepdims=True)
        hit = lane == pick
        logits = jnp.where(hit, -jnp.inf, logits)
        onehot = jnp.where(hit, 1.0, onehot)
        idx_out = jnp.where(lane == kk, pick, idx_out)
        val_out = jnp.where(lane == kk, mx, val_out)
        picks.append(hit)
        if kk == 0:
            top = mx
    e = jnp.where(lane < TOP_K, jnp.exp(val_out - top), 0.0)
    gate_ref[...] = e * (1.0 / jnp.sum(e, axis=-1, keepdims=True))
    idx_ref[...] = idx_out
    row = lax.broadcasted_iota(jnp.int32, (tm, tm), 0)
    col = lax.broadcasted_iota(jnp.int32, (tm, tm), 1)
    strict = jnp.where(row > col, 1.0, 0.0).astype(BF16)
    before = _dot(strict, onehot.astype(BF16)) + carry_ref[...]
    rank_out = jnp.zeros((tm, LANES), jnp.int32)
    for kk in range(TOP_K):
        rk = jnp.sum(jnp.where(picks[kk], before, 0.0), axis=-1, keepdims=True)
        rank_out = jnp.where(lane == kk, rk.astype(jnp.int32), rank_out)
    rank_ref[...] = rank_out
    carry_ref[...] = carry_ref[...] + jnp.sum(onehot, axis=0, keepdims=True)
    cnt_ref[...] = carry_ref[...].astype(jnp.int32)


def _merge_route(x, og, od, sa, sb, wg, wd, wo, gf, wr, br, counts_in):
    n, d = x.shape
    tm = _row_tile(n, 256)
    row = lambda w: pl.BlockSpec((tm, w), lambda i: (i, 0))
    return pl.pallas_call(
        _merge_route_kernel,
        grid=(n // tm,),
        in_specs=[row(d)] * 5 + [_full(wg.shape), _full(wd.shape), _full(wo.shape), _full(gf.shape),
                                 _full(wr.shape), _full(br.shape), _full(counts_in.shape)],
        out_specs=[row(d), pl.BlockSpec((tm * d // LANES, LANES), lambda i: (i, 0)),
                   row(LANES), row(LANES), row(LANES), _full((1, LANES))],
        out_shape=[jax.ShapeDtypeStruct((n, d), F32), jax.ShapeDtypeStruct((n * d // LANES, LANES), F32),
                   jax.ShapeDtypeStruct((n, LANES), jnp.int32), jax.ShapeDtypeStruct((n, LANES), F32),
                   jax.ShapeDtypeStruct((n, LANES), jnp.int32), jax.ShapeDtypeStruct((1, LANES), jnp.int32)],
        scratch_shapes=[pltpu.VMEM((1, LANES), F32)],
        compiler_params=_cparams(("arbitrary",)),
        name="merge_route",
    )(x, og, od, sa, sb, wg, wd, wo, gf, wr, br, counts_in)


def _pair_shuffle(h):
    a, b = h[:, :LANES], h[:, LANES:]
    even = (lax.broadcasted_iota(jnp.int32, (1, LANES), 1) % 2) == 0
    glu = jnp.where(even, a, pltpu.roll(b, 1, 1))
    lin = jnp.where(even, pltpu.roll(a, LANES - 1, 1), b)
    return glu, lin


def _moe_kernel(be_ref, nu_ref, x_ref, wu_ref, bu_ref, wd_ref, bd_ref, y_ref, xs_ref):
    i = pl.program_id(0)
    bm, d = xs_ref.shape
    n_tiles = d // LANES

    @pl.when(i < nu_ref[0])
    def _():
        for s in range(n_tiles):
            xs_ref[:, s * LANES:(s + 1) * LANES] = x_ref[pl.ds(s, bm, stride=SUBLANES), :].astype(BF16)
        h = _dot(xs_ref[...], wu_ref[0]) + bu_ref[0]
        acts = []
        for c in range(h.shape[1] // (2 * LANES)):
            glu, lin = _pair_shuffle(h[:, 2 * c * LANES:2 * (c + 1) * LANES])
            x_glu = jnp.minimum(glu, SWIGLU_LIMIT)
            x_lin = jnp.clip(lin, -SWIGLU_LIMIT, SWIGLU_LIMIT)
            acts.append((x_glu * _sigmoid(SWIGLU_ALPHA * x_glu) * (x_lin + 1.0)).astype(BF16))
        y = _dot(jnp.concatenate(acts, axis=1), wd_ref[0]) + bd_ref[0]
        for s in range(n_tiles):
            y_ref[pl.ds(s, bm, stride=SUBLANES), :] = y[:, s * LANES:(s + 1) * LANES]

    @pl.when(i >= nu_ref[0])
    def _():
        y_ref[...] = jnp.zeros(y_ref.shape, F32)


def _moe_experts(blk_exp, n_used, xs, wu, bu, wd, bd, d):
    tiles = d // LANES
    n_rows = xs.shape[0] // tiles
    de2 = wu.shape[2]
    wspec = lambda a, b: pl.BlockSpec((1, a, b), lambda i, be, nu: (be[i], 0, 0))
    rows = pl.BlockSpec((MOE_ROWS * tiles, LANES), lambda i, be, nu: (i, 0))
    grid_spec = pltpu.PrefetchScalarGridSpec(
        num_scalar_prefetch=2,
        grid=(n_rows // MOE_ROWS,),
        in_specs=[rows, wspec(d, de2), wspec(1, de2), wspec(de2 // 2, d), wspec(1, d)],
        out_specs=rows,
        scratch_shapes=[pltpu.VMEM((MOE_ROWS, d), BF16)],
    )
    return pl.pallas_call(
        _moe_kernel,
        grid_spec=grid_spec,
        out_shape=jax.ShapeDtypeStruct(xs.shape, F32),
        compiler_params=_cparams(("arbitrary",)),
        name="moe_experts",
    )(blk_exp, n_used, xs, wu, bu, wd, bd)


def _combine_kernel(h_ref, gate_ref, y0_ref, y1_ref, y2_ref, y3_ref, gf_ref, o_ref):
    tm, d = h_ref.shape
    gate = gate_ref[...]
    parts = []
    for s in range(d // LANES):
        acc = h_ref[:, s * LANES:(s + 1) * LANES]
        for kk, y_ref in enumerate((y0_ref, y1_ref, y2_ref, y3_ref)):
            acc = acc + gate[:, kk:kk + 1] * y_ref[0, pl.ds(s, tm, stride=SUBLANES), :]
        parts.append(acc)
    o_ref[...] = _rms(jnp.concatenate(parts, axis=1), gf_ref[...])


def _combine(h, gate, yg, gf, tok0):
    n, d = h.shape
    tm = _row_tile(n, 256)
    assert tok0 % tm == 0
    b0 = tok0 // tm
    tiles = d // LANES
    row = lambda w: pl.BlockSpec((tm, w), lambda i: (i, 0))
    ysp = lambda kk: pl.BlockSpec((1, tm * tiles, LANES), lambda i: (kk, b0 + i, 0))
    return pl.pallas_call(
        _combine_kernel,
        grid=(n // tm,),
        in_specs=[row(d), row(LANES)] + [ysp(kk) for kk in range(TOP_K)] + [_full(gf.shape)],
        out_specs=row(d),
        out_shape=jax.ShapeDtypeStruct((n, d), F32),
        compiler_params=_cparams(("parallel",)),
        name="combine",
    )(h, gate, yg, yg, yg, yg, gf)


def _rope_tables(pos):
    half = DIFF_DH // 2
    inv_freq = ROPE_THETA ** (-jnp.arange(half, dtype=F32) / half)
    ang = pos.astype(F32)[:, None] * inv_freq[None, :]
    cos, sin = jnp.cos(ang), jnp.sin(ang)
    reps = LANES // DIFF_DH
    return jnp.tile(jnp.concatenate([cos, cos], axis=1), (1, reps)), jnp.tile(jnp.concatenate([-sin, sin], axis=1), (1, reps))


def kernel(x_prompt, x_sample, cache_k, cache_v, state_gla, page_table, norm_attn, w_in, gla_gate_up, gla_gate_bias, gla_norm, lam_q1, lam_k1, lam_q2, lam_k2, diff_subln, w_branch_gla, w_branch_diff, w_out, norm_ffn, w_router, b_router, w_up, b_up, w_down, b_down, norm_final):
    depth = w_in.shape[0]
    assert depth == 1, "single-layer step"
    layer = 0
    b_p, t_p, d = x_prompt.shape
    b_s, t_s, _ = x_sample.shape
    n_p, n_s = b_p * t_p, b_s * t_s
    n = n_p + n_s
    n_pool, page = cache_k.shape[1], cache_k.shape[2]
    n_pages = page_table.shape[1]
    past_len = n_pages * page
    kd, vd = GLA_HEADS * GLA_DK, GLA_HEADS * GLA_DV
    tiles = d // LANES
    row2 = lambda a: a.reshape(1, -1)
    assert t_s <= DIFF_HEADS and t_s * DIFF_HEADS <= LANES

    splits = (kd, kd, vd, GLA_RANK, vd, d, d, d, d, d)
    offs = [0]
    for s in splits:
        offs.append(offs[-1] + s)
    wb = w_in.reshape(d, -1).astype(BF16)
    w_gq, w_gk, w_gv, w_glr, w_gr, w_dq, w_dk, w_dv, w_ga, w_gb = [wb[:, offs[j]:offs[j + 1]] for j in range(10)]
    w_glr = jnp.pad(w_glr, ((0, 0), (0, LANES - GLA_RANK)))
    gup = jnp.pad(gla_gate_up.reshape(GLA_RANK, kd).astype(BF16), ((0, LANES - GLA_RANK), (0, 0)))
    g_attn = row2(norm_attn)
    gn = row2(gla_norm)
    lam_init = 0.8 - 0.6 * math.exp(-0.3 * layer)
    lams = (row2(lam_q1), row2(lam_k1), row2(lam_q2), row2(lam_k2))
    subln = row2(diff_subln)
    w_bg = w_branch_gla.reshape(vd, d).astype(BF16)
    w_bd = w_branch_diff.reshape(d, d).astype(BF16)
    w_o = w_out.reshape(d, d).astype(BF16)
    w_r = jnp.pad(w_router.reshape(d, N_EXPERTS).astype(BF16), ((0, 0), (0, LANES - N_EXPERTS)))
    b_r = jnp.pad(row2(b_router), ((0, 0), (0, LANES - N_EXPERTS)))
    g_ffn = row2(norm_ffn)

    def mixers(x, pos, seq):
        gq, gk, gv, gla_la, gr = _gla_proj(x, g_attn, w_gq, w_gk, w_gv, w_glr, w_gr, gup, row2(gla_gate_bias))
        cos, sin_signed = _rope_tables(pos)
        return (gq, gk, gv, gla_la, gr) + tuple(
            _diff_proj(x, g_attn, cos, sin_signed, w_dq, w_dk, w_dv, w_ga, w_gb, seq))

    x_p = x_prompt.reshape(n_p, d)
    gq, gk, gv, gla_la, gr, dq, dk_b, dv_b, kt_p, v_cache_p, sig_a, sig_b = mixers(
        x_p, jnp.tile(jnp.arange(t_p, dtype=jnp.int32), b_p), t_p)
    og_p, s_p = _gla_chunks(gq, gk, gv, gla_la, gr, gn, None, b_p, t_p, GLA_CHUNK if t_p % GLA_CHUNK == 0 else t_p)
    od_p = _diff_prompt(dq, dk_b, dv_b, lams, subln, b_p, t_p, lam_init)
    zero_counts = jnp.zeros((1, LANES), jnp.int32)
    h_p, hn_p, idx_p, gate_p, rank_p, counts_p = _merge_route(
        x_p, og_p, od_p, sig_a, sig_b, w_bg, w_bd, w_o, g_ffn, w_r, b_r, zero_counts)

    x_s = x_sample.reshape(n_s, d)
    gq, gk, gv, gla_la, gr, dq, dk_b, dv_b, kt_s, v_cache_s, sig_a, sig_b = mixers(
        x_s, jnp.tile(past_len + jnp.arange(t_s, dtype=jnp.int32), b_s), n_s)
    t_pad = -(-t_s // GLA_CHUNK) * GLA_CHUNK
    pad_s = lambda a: jnp.pad(a.reshape(b_s, t_s, -1), ((0, 0), (0, t_pad - t_s), (0, 0))).reshape(b_s * t_pad, -1)
    og_s, s_s = _gla_chunks(pad_s(gq), pad_s(gk), pad_s(gv), pad_s(gla_la), pad_s(gr), gn,
                            state_gla.reshape(b_s, GLA_HEADS, GLA_DK, GLA_DV), b_s, t_pad, t_pad)
    og_s = og_s.reshape(b_s, t_pad, vd)[:, :t_s].reshape(n_s, vd)
    q4 = dq.reshape(b_s, t_s, DIFF_QK_HEADS, DIFF_DH).transpose(0, 2, 1, 3)
    q_bd = (q4[:, :, :, None, :] * jnp.eye(DIFF_QK_HEADS, dtype=BF16)[None, :, None, :, None]).reshape(
        b_s, DIFF_QK_HEADS * t_s, d)
    kn = jnp.pad(dk_b.reshape(b_s, t_s, d).transpose(0, 2, 1), ((0, 0), (0, 0), (0, LANES - t_s)))
    vn = jnp.pad(dv_b.reshape(b_s, t_s, DIFF_HEADS, LANES).transpose(0, 2, 1, 3),
                 ((0, 0), (0, 0), (0, LANES - t_s), (0, 0)))
    cache_kt = jnp.transpose(cache_k, (0, 1, 3, 4, 2)).reshape(depth * n_pool, DIFF_QK_HEADS * DIFF_DH, page)
    od_s = _diff_sample(page_table, q_bd, kn, vn, lams, subln, cache_kt,
                        cache_v.reshape(depth * n_pool, page * DIFF_HEADS, LANES), lam_init, t_s)
    od_s = od_s.reshape(b_s, DIFF_HEADS, 2, t_s, LANES)[:, :, 0].transpose(0, 2, 1, 3).reshape(n_s, d).astype(BF16)
    h_s, hn_s, idx_s, gate_s, rank_s, counts = _merge_route(
        x_s, og_s, od_s, sig_a, sig_b, w_bg, w_bd, w_o, g_ffn, w_r, b_r, counts_p)

    counts = counts[0, :N_EXPERTS]
    padded = (counts + MOE_ROWS - 1) // MOE_ROWS * MOE_ROWS
    pad_end = jnp.cumsum(padded)
    pad_start = pad_end - padded
    idx4 = jnp.concatenate([idx_p[:, :TOP_K], idx_s[:, :TOP_K]], axis=0)
    rank4 = jnp.concatenate([rank_p[:, :TOP_K], rank_s[:, :TOP_K]], axis=0)
    dest = pad_start[idx4] + rank4
    n_rows = -(-(n * TOP_K + N_EXPERTS * (MOE_ROWS - 1)) // MOE_ROWS) * MOE_ROWS
    n_blocks = n_rows // MOE_ROWS
    blk_start = jnp.arange(n_blocks, dtype=jnp.int32) * MOE_ROWS
    blk_exp = jnp.minimum(jnp.sum((blk_start[:, None] >= pad_end[None, :]).astype(jnp.int32), axis=1), N_EXPERTS - 1)
    n_used = (pad_end[-1:] // MOE_ROWS).astype(jnp.int32)
    row_tok = jnp.zeros((n_rows,), jnp.int32).at[dest.reshape(-1)].set(
        jnp.repeat(jnp.arange(n, dtype=jnp.int32), TOP_K))
    hn_all = jnp.concatenate([hn_p, hn_s], axis=0).reshape(n, tiles, LANES)
    xs = jnp.take(hn_all, row_tok, axis=0, mode='clip').reshape(n_rows * tiles, LANES)

    de = w_down.shape[2]
    w_u = w_up.reshape(N_EXPERTS, d, 2 * de).astype(BF16)
    b_u = b_up.reshape(N_EXPERTS, 1, 2 * de)
    half = LANES // 2
    w_d = w_down.reshape(N_EXPERTS, de // LANES, 2, half, d).transpose(0, 1, 3, 2, 4).reshape(N_EXPERTS, de, d).astype(BF16)
    y = _moe_experts(blk_exp, n_used, xs, w_u, b_u, w_d, b_down.reshape(N_EXPERTS, 1, d), d)
    yg = jnp.take(y.reshape(n_rows, tiles, LANES), dest.T.reshape(-1), axis=0, mode='clip').reshape(TOP_K, n * tiles, LANES)

    g_fin = row2(norm_final)
    y_prompt = _combine(h_p, gate_p, yg, g_fin, 0).reshape(b_p, t_p, d)
    y_sample = _combine(h_s, gate_s, yg, g_fin, n_p).reshape(b_s, t_s, d)
    k_prompt = kt_p.reshape(1, b_p, DIFF_QK_HEADS, DIFF_DH, t_p).transpose(0, 1, 4, 2, 3)
    v_prompt = v_cache_p.reshape(1, b_p, t_p, DIFF_HEADS, 2 * DIFF_DH)
    k_sample = kt_s.reshape(1, DIFF_QK_HEADS, DIFF_DH, b_s, t_s).transpose(0, 3, 4, 1, 2)
    v_sample = v_cache_s.reshape(1, b_s, t_s, DIFF_HEADS, 2 * DIFF_DH)
    return (y_prompt, y_sample, k_prompt, v_prompt, s_p[None], k_sample, v_sample, s_s[None])
```

```python
import functools
import math

import jax
import jax.numpy as jnp
from jax import lax
from jax.experimental import pallas as pl
from jax.experimental.pallas import tpu as pltpu

F32 = jnp.float32
BF16 = jnp.bfloat16

EPS = 1e-6
GLA_HEADS = 4
GLA_DK = 128
GLA_DV = 256
GLA_RANK = 16
GLA_NORMALIZER = 16.0
GLA_CHUNK = 64
GLA_SAMPLE_CHUNK = 16
DIFF_HEADS = 8
DIFF_QK_HEADS = 2 * DIFF_HEADS
DIFF_DH = 64
ROPE_THETA = 10000.0
N_EXPERTS = 32
TOP_K = 4
SWIGLU_ALPHA = 1.702
SWIGLU_LIMIT = 7.0
LOG2E = math.log2(math.e)

LANES = 128
SUBLANES = 8
MOE_ROWS = 256
ATTN_Q = 1024
ATTN_K = 512
ATTN_SUB = 256
ATTN_LOOKAHEAD = 3
PAGES_PER_STEP = 8
VMEM_LIMIT = 48 * 1024 * 1024
MOE_VMEM_LIMIT = 56 * 1024 * 1024


def _cparams(sem):
    return pltpu.CompilerParams(dimension_semantics=sem, vmem_limit_bytes=VMEM_LIMIT)


def _dot(a, b):
    return jnp.dot(a, b, preferred_element_type=F32)


def _dot_nt(a, b):
    return lax.dot_general(a, b, (((1,), (1,)), ((), ())), preferred_element_type=F32)


def _dot_tn(a, b):
    return lax.dot_general(a, b, (((0,), (0,)), ((), ())), preferred_element_type=F32)


def _rms(x, g):
    ms = jnp.mean(x * x, axis=-1, keepdims=True)
    return x * lax.rsqrt(ms + EPS) * g


def _sigmoid(x):
    return 1.0 / (1.0 + jnp.exp(-x))


def _row_tile(n, cap=512):
    for t in (512, 256, 128, 64, 32, 16, 8):
        if t <= cap and n % t == 0:
            return t
    raise ValueError(f"token count {n} is not a multiple of 8")


def _full(shape):
    return pl.BlockSpec(shape, lambda *_: (0,) * len(shape))


def _gla_proj_kernel(x_ref, g_ref, wq_ref, wk_ref, wv_ref, wlr_ref, wr_ref, gup_ref, gb_ref,
                     q_ref, k_ref, v_ref, la_ref, r_ref):
    xn = _rms(x_ref[...], g_ref[...]).astype(BF16)
    q_ref[...] = _dot(xn, wq_ref[...]) * (GLA_DK ** -0.5)
    k_ref[...] = _dot(xn, wk_ref[...])
    v_ref[...] = _dot(xn, wv_ref[...]).astype(BF16)
    glr = _dot(xn, wlr_ref[...]).astype(BF16)
    z = _dot(glr, gup_ref[...]) + gb_ref[...]
    log_sig = jnp.minimum(z, 0.0) - jnp.log1p(jnp.exp(-jnp.abs(z)))
    la_ref[...] = log_sig * (1.0 / GLA_NORMALIZER)
    r_ref[...] = _dot(xn, wr_ref[...])


def _gla_proj(x, g, wq, wk, wv, wlr, wr, gup, gb):
    n, d = x.shape
    tm = _row_tile(n)
    kd, vd = wq.shape[1], wv.shape[1]
    row = lambda w: pl.BlockSpec((tm, w), lambda i: (i, 0))
    return pl.pallas_call(
        _gla_proj_kernel,
        grid=(n // tm,),
        in_specs=[row(d), _full(g.shape), _full(wq.shape), _full(wk.shape), _full(wv.shape),
                  _full(wlr.shape), _full(wr.shape), _full(gup.shape), _full(gb.shape)],
        out_specs=[row(kd), row(kd), row(vd), row(kd), row(vd)],
        out_shape=[jax.ShapeDtypeStruct((n, kd), F32), jax.ShapeDtypeStruct((n, kd), F32),
                   jax.ShapeDtypeStruct((n, vd), BF16), jax.ShapeDtypeStruct((n, kd), F32),
                   jax.ShapeDtypeStruct((n, vd), F32)],
        compiler_params=_cparams(("parallel",)),
        name="gla_proj",
    )(x, g, wq, wk, wv, wlr, wr, gup, gb)


def _rope_chunk(xc, cos, sin_signed, first_half):
    swapped = jnp.where(first_half, pltpu.roll(xc, LANES - DIFF_DH // 2, 1), pltpu.roll(xc, DIFF_DH // 2, 1))
    return xc * cos + swapped * sin_signed


def _diff_proj_kernel(x_ref, g_ref, cos_ref, sin_ref, wq_ref, wk_ref, wv_ref, wa_ref, wb_ref,
                      q_ref, kb_ref, vb_ref, kt_ref, vc_ref, sa_ref, sb_ref):
    tm = x_ref.shape[0]
    xn = _rms(x_ref[...], g_ref[...]).astype(BF16)
    cos = cos_ref[...]
    sin_signed = sin_ref[...]
    lane = lax.broadcasted_iota(jnp.int32, (1, LANES), 1)
    first_half = (lane % DIFF_DH) < (DIFF_DH // 2)
    n_chunks = q_ref.shape[1] // LANES
    q = _dot(xn, wq_ref[...])
    for c in range(n_chunks):
        sl = slice(c * LANES, (c + 1) * LANES)
        q_ref[:, sl] = (_rope_chunk(q[:, sl], cos, sin_signed, first_half) * (DIFF_DH ** -0.5 * LOG2E)).astype(BF16)
    k = _dot(xn, wk_ref[...])
    for c in range(n_chunks):
        sl = slice(c * LANES, (c + 1) * LANES)
        kr = _rope_chunk(k[:, sl], cos, sin_signed, first_half)
        kb_ref[:, sl] = kr.astype(BF16)
        kt_ref[0, sl, :] = kr.T
    v = _dot(xn, wv_ref[...])
    vb_ref[...] = v.astype(BF16)
    for c in range(n_chunks):
        vc_ref[pl.ds(c, tm, stride=DIFF_HEADS), :] = v[:, c * LANES:(c + 1) * LANES]
    sa_ref[...] = _sigmoid(_dot(xn, wa_ref[...]))
    sb_ref[...] = _sigmoid(_dot(xn, wb_ref[...]))


def _diff_proj(x, g, cos, sin_signed, wq, wk, wv, wa, wb, seq):
    n, d = x.shape
    tm = _row_tile(seq, 256)
    tps = seq // tm
    row = lambda w: pl.BlockSpec((tm, w), lambda i: (i, 0))
    sds = lambda dt: jax.ShapeDtypeStruct((n, d), dt)
    return pl.pallas_call(
        _diff_proj_kernel,
        grid=(n // tm,),
        in_specs=[row(d), _full(g.shape), row(LANES), row(LANES)] + [_full(wq.shape)] * 5,
        out_specs=[row(d), row(d), row(d),
                   pl.BlockSpec((1, d, tm), lambda i: (i // tps, 0, i % tps)),
                   pl.BlockSpec((tm * DIFF_HEADS, LANES), lambda i: (i, 0)),
                   row(d), row(d)],
        out_shape=[sds(BF16), sds(BF16), sds(BF16),
                   jax.ShapeDtypeStruct((n // seq, d, seq), F32),
                   jax.ShapeDtypeStruct((n * DIFF_HEADS, LANES), F32),
                   sds(F32), sds(F32)],
        compiler_params=_cparams(("parallel",)),
        name="diff_proj",
    )(x, g, cos, sin_signed, wq, wk, wv, wa, wb)


def _gla_kernel(has_s0, *refs):
    if has_s0:
        q_ref, k_ref, v_ref, la_ref, r_ref, gn_ref, s0_ref, o_ref, sfin_ref, st_ref = refs
    else:
        q_ref, k_ref, v_ref, la_ref, r_ref, gn_ref, o_ref, sfin_ref, st_ref = refs
    n = pl.program_id(1)
    c = q_ref.shape[0]

    @pl.when(n == 0)
    def _():
        for h in range(GLA_HEADS):
            if has_s0:
                st_ref[h] = s0_ref[0, h].T
            else:
                st_ref[h] = jnp.zeros(st_ref.shape[1:], F32)

    row = lax.broadcasted_iota(jnp.int32, (c, c), 0)
    col = lax.broadcasted_iota(jnp.int32, (c, c), 1)
    causal = row >= col
    tri = jnp.where(causal, 1.0, 0.0).astype(BF16)
    gn = gn_ref[...]
    heads = range(GLA_HEADS)
    ks = [slice(h * GLA_DK, (h + 1) * GLA_DK) for h in heads]
    vs = [slice(h * GLA_DV, (h + 1) * GLA_DV) for h in heads]
    bcum = []
    for h in heads:
        la = la_ref[:, ks[h]]
        hi = la.astype(BF16)
        rem = la - hi.astype(F32)
        mid = rem.astype(BF16)
        lo = (rem - mid.astype(F32)).astype(BF16)
        bcum.append(_dot(tri, hi) + _dot(tri, mid) + _dot(tri, lo))
    q_t = [(q_ref[:, ks[h]] * jnp.exp(bcum[h])).astype(BF16) for h in heads]
    k_t = [(k_ref[:, ks[h]] * jnp.exp(-bcum[h])).astype(BF16) for h in heads]
    b_last = [bcum[h][c - 1:c, :] for h in heads]
    k_dec = [(k_ref[:, ks[h]] * jnp.exp(b_last[h] - bcum[h])).astype(BF16) for h in heads]
    st = [st_ref[h] for h in heads]
    scores = [jnp.where(causal, _dot_nt(q_t[h], k_t[h]), 0.0).astype(BF16) for h in heads]
    o_state = [_dot_nt(q_t[h], st[h].astype(BF16)) for h in heads]
    upd = [_dot_tn(v_ref[:, vs[h]], k_dec[h]) for h in heads]
    o_intra = [_dot(scores[h], v_ref[:, vs[h]]) for h in heads]
    for h in heads:
        st_ref[h] = st[h] * jnp.exp(b_last[h]) + upd[h]
        rh = r_ref[:, vs[h]]
        o_ref[:, vs[h]] = (_rms(o_state[h] + o_intra[h], gn) * (rh * _sigmoid(rh))).astype(BF16)

    @pl.when(n == pl.num_programs(1) - 1)
    def _():
        for h in range(GLA_HEADS):
            sfin_ref[0, h] = st_ref[h].T


def _gla_chunks(q, k, v, la, r, gn, s0, batch, seq, chunk):
    nc = seq // chunk
    kd, vd = q.shape[1], v.shape[1]
    blk = lambda w: pl.BlockSpec((chunk, w), lambda b, n: (b * nc + n, 0))
    state_spec = pl.BlockSpec((1, GLA_HEADS, GLA_DK, GLA_DV), lambda b, n: (b, 0, 0, 0))
    in_specs = [blk(kd), blk(kd), blk(vd), blk(kd), blk(vd), _full(gn.shape)]
    args = [q, k, v, la, r, gn]
    if s0 is not None:
        in_specs.append(state_spec)
        args.append(s0)
    return pl.pallas_call(
        functools.partial(_gla_kernel, s0 is not None),
        grid=(batch, nc),
        in_specs=in_specs,
        out_specs=[blk(vd), state_spec],
        out_shape=[jax.ShapeDtypeStruct((batch * seq, vd), BF16),
                   jax.ShapeDtypeStruct((batch, GLA_HEADS, GLA_DK, GLA_DV), F32)],
        scratch_shapes=[pltpu.VMEM((GLA_HEADS, GLA_DV, GLA_DK), F32)],
        compiler_params=_cparams(("parallel", "arbitrary")),
        name="gla_chunks",
    )(*args)


def _lambda(lq1_ref, lk1_ref, lq2_ref, lk2_ref, lam_init):
    s1 = jnp.sum(lq1_ref[...] * lk1_ref[...], axis=-1, keepdims=True)
    s2 = jnp.sum(lq2_ref[...] * lk2_ref[...], axis=-1, keepdims=True)
    return jnp.exp(s1) - jnp.exp(s2) + lam_init


def _diff_prompt_kernel(lam_init, tk, sub, q_ref, k_ref, v_ref, lq1_ref, lk1_ref, lq2_ref, lk2_ref, sub_ref,
                        o_ref, m_ref, a_ref):
    qi = pl.program_id(2)
    tq = q_ref.shape[0]
    lane = lax.broadcasted_iota(jnp.int32, (1, LANES), 1)
    m_ref[...] = jnp.full(m_ref.shape, -jnp.inf, F32)
    a_ref[...] = jnp.zeros(a_ref.shape, F32)

    def scores(mp, r0, k0, nk, mask):
        q = q_ref[pl.ds(r0, sub), :]
        qm = jnp.where((lane < DIFF_DH) if mp == 0 else (lane >= DIFF_DH), q, jnp.zeros_like(q))
        s = _dot_nt(qm, k_ref[pl.ds(k0, nk), :])
        return s if mask is None else jnp.where(mask, s, -jnp.inf)

    def update(mp, r0, k0, nk, s):
        rows = pl.ds(r0, sub)
        n_chunks = nk // LANES
        mx = s[:, :LANES]
        for c in range(1, n_chunks):
            mx = jnp.maximum(mx, s[:, c * LANES:(c + 1) * LANES])
        m_old = m_ref[mp, rows, :]
        m_new = jnp.maximum(m_old, jnp.max(mx, axis=-1, keepdims=True))
        alpha = jnp.exp2(m_old - m_new)
        p = jnp.concatenate(
            [jnp.exp2(s[:, c * LANES:(c + 1) * LANES] - m_new).astype(BF16) for c in range(n_chunks)], axis=1)
        m_ref[mp, rows, :] = m_new
        vj = v_ref[pl.ds(k0, nk), :]
        v_ones = jnp.concatenate([vj, jnp.ones_like(vj)], axis=1)
        a_ref[mp, rows, :] = jnp.concatenate([alpha, alpha], axis=1) * a_ref[mp, rows, :] + _dot(p, v_ones)

    def run(chains):
        pending = {}
        for i in range(min(ATTN_LOOKAHEAD, len(chains))):
            pending[i] = scores(*chains[i])
        for i, (mp, r0, k0, nk, _) in enumerate(chains):
            if i + ATTN_LOOKAHEAD < len(chains):
                pending[i + ATTN_LOOKAHEAD] = scores(*chains[i + ATTN_LOOKAHEAD])
            update(mp, r0, k0, nk, pending.pop(i))

    n_sub = tq // sub

    def body(j, carry):
        k0 = pl.multiple_of(j * tk, tk)
        run([(mp, si * sub, k0, tk, None) for si in range(n_sub) for mp in range(2)])
        return carry

    lax.fori_loop(0, qi * (tq // tk), body, 0)

    base = pl.multiple_of(qi * tq, tq)
    chains = []
    for si in range(n_sub):
        need = (si + 1) * sub
        for k0 in range(0, need, tk):
            nk = min(tk, need - k0)
            mask = None
            if k0 + nk == need:
                row = lax.broadcasted_iota(jnp.int32, (sub, nk), 0) + si * sub
                col = lax.broadcasted_iota(jnp.int32, (sub, nk), 1) + k0
                mask = row >= col
            chains += [(mp, si * sub, base + k0, nk, mask) for mp in range(2)]
    run(chains)

    lam = _lambda(lq1_ref, lk1_ref, lq2_ref, lk2_ref, lam_init)
    o1 = a_ref[0, :, :LANES] * (1.0 / a_ref[0, :, LANES:])
    o2 = a_ref[1, :, :LANES] * (1.0 / a_ref[1, :, LANES:])
    o_ref[...] = (_rms(o1 - lam * o2, sub_ref[...]) * (1.0 - lam_init)).astype(BF16)


def _diff_prompt(q, k, v, lams, subln, batch, seq, lam_init):
    tq = min(ATTN_Q, seq)
    tk = min(ATTN_K, tq)
    sub = min(ATTN_SUB, tk)
    nq = seq // tq
    vec = _full(lams[0].shape)
    return pl.pallas_call(
        functools.partial(_diff_prompt_kernel, lam_init, tk, sub),
        grid=(batch, DIFF_HEADS, nq),
        in_specs=[pl.BlockSpec((tq, LANES), lambda b, h, i: (b * nq + i, h)),
                  pl.BlockSpec((seq, LANES), lambda b, h, i: (b, h)),
                  pl.BlockSpec((seq, LANES), lambda b, h, i: (b, h)),
                  vec, vec, vec, vec, _full(subln.shape)],
        out_specs=pl.BlockSpec((tq, LANES), lambda b, h, i: (b * nq + i, h)),
        out_shape=jax.ShapeDtypeStruct((batch * seq, DIFF_HEADS * LANES), BF16),
        scratch_shapes=[pltpu.VMEM((2, tq, LANES), F32), pltpu.VMEM((2, tq, 2 * LANES), F32)],
        compiler_params=_cparams(("parallel", "parallel", "arbitrary")),
        name="diff_prompt",
    )(q, k, v, *lams, subln)


def _diff_sample_kernel(lam_init, t_new, pt_ref, q_ref, kn_ref, vn_ref, lq1_ref, lk1_ref, lq2_ref, lk2_ref, sub_ref,
                        *rest):
    npg = PAGES_PER_STEP
    k_refs, v_refs = rest[:npg], rest[npg:2 * npg]
    o_ref, m_ref, l_ref, acc_ref = rest[2 * npg:]
    g = pl.program_id(1)
    rows = q_ref.shape[1]
    grp = rows // DIFF_HEADS
    page = k_refs[0].shape[2]

    @pl.when(g == 0)
    def _():
        m_ref[...] = jnp.full(m_ref.shape, -jnp.inf, F32)
        l_ref[...] = jnp.zeros(l_ref.shape, F32)
        acc_ref[...] = jnp.zeros(acc_ref.shape, F32)

    q = q_ref[0]

    def attend(s, v_of_head):
        m_old = m_ref[...]
        m_new = jnp.maximum(m_old, jnp.max(s, axis=-1, keepdims=True))
        p = jnp.exp2(s - m_new)
        alpha = jnp.exp2(m_old - m_new)
        l_ref[...] = alpha * l_ref[...] + jnp.sum(p, axis=-1, keepdims=True)
        pv = [_dot(p[h * grp:(h + 1) * grp].astype(BF16), v_of_head(h)) for h in range(DIFF_HEADS)]
        acc_ref[...] = alpha * acc_ref[...] + jnp.concatenate(pv, axis=0)
        m_ref[...] = m_new

    kcat = jnp.concatenate([kr[0].astype(BF16) for kr in k_refs], axis=1)
    attend(_dot(q, kcat), lambda h: jnp.concatenate(
        [vr[0, pl.ds(h, page, stride=DIFF_HEADS), :].astype(BF16) for vr in v_refs], axis=0))

    @pl.when(g == pl.num_programs(1) - 1)
    def _():
        s_new = _dot(q, kn_ref[0])
        r = lax.broadcasted_iota(jnp.int32, s_new.shape, 0)
        c = lax.broadcasted_iota(jnp.int32, s_new.shape, 1)
        attend(jnp.where(c <= r % t_new, s_new, -jnp.inf), lambda h: vn_ref[0, h])
        lam = _lambda(lq1_ref, lk1_ref, lq2_ref, lk2_ref, lam_init)
        an = acc_ref[...] * (1.0 / l_ref[...])
        diff = an - lam * pltpu.roll(an, rows - t_new, 0)
        o_ref[0] = _rms(diff, sub_ref[...]) * (1.0 - lam_init)


def _diff_sample(page_table, q, k_new, v_new, lams, subln, cache_kt, cache_v, lam_init, t_new):
    batch, rows, _ = q.shape
    n_pages = page_table.shape[1]
    npg = PAGES_PER_STEP
    assert n_pages % npg == 0
    vec = pl.BlockSpec(lams[0].shape, lambda b, g, pt: (0, 0))
    per_batch = lambda a: pl.BlockSpec((1,) + a.shape[1:], lambda b, g, pt: (b,) + (0,) * (a.ndim - 1))

    def page_spec(a, j):
        return pl.BlockSpec((1,) + a.shape[1:], lambda b, g, pt: (pt[b * n_pages + g * npg + j], 0, 0))

    grid_spec = pltpu.PrefetchScalarGridSpec(
        num_scalar_prefetch=1,
        grid=(batch, n_pages // npg),
        in_specs=[per_batch(q), per_batch(k_new), per_batch(v_new), vec, vec, vec, vec,
                  pl.BlockSpec(subln.shape, lambda b, g, pt: (0, 0))]
        + [page_spec(cache_kt, j) for j in range(npg)] + [page_spec(cache_v, j) for j in range(npg)],
        out_specs=pl.BlockSpec((1, rows, LANES), lambda b, g, pt: (b, 0, 0)),
        scratch_shapes=[pltpu.VMEM((rows, 1), F32), pltpu.VMEM((rows, 1), F32), pltpu.VMEM((rows, LANES), F32)],
    )
    return pl.pallas_call(
        functools.partial(_diff_sample_kernel, lam_init, t_new),
        grid_spec=grid_spec,
        out_shape=jax.ShapeDtypeStruct((batch, rows, LANES), F32),
        compiler_params=_cparams(("parallel", "arbitrary")),
        name="diff_sample",
    )(page_table.reshape(-1), q, k_new, v_new, *lams, subln, *([cache_kt] * npg), *([cache_v] * npg))


def _merge_route_kernel(x_ref, og_ref, od_ref, sa_ref, sb_ref, wg_ref, wd_ref, wo_ref, gf_ref, wr_ref, br_ref,
                        cin_ref, h_ref, hn_ref, idx_ref, gate_ref, rank_ref, cnt_ref, carry_ref):
    i = pl.program_id(0)
    tm = x_ref.shape[0]

    @pl.when(i == 0)
    def _():
        carry_ref[...] = cin_ref[...].astype(F32)

    merged = sa_ref[...] * _dot(og_ref[...], wg_ref[...]) + sb_ref[...] * _dot(od_ref[...], wd_ref[...])
    h = x_ref[...] + _dot(merged.astype(BF16), wo_ref[...])
    h_ref[...] = h
    hn = _rms(h, gf_ref[...]).astype(BF16)
    hn32 = hn.astype(F32)
    for s in range(hn_ref.shape[0] // tm):
        hn_ref[pl.ds(s, tm, stride=SUBLANES), :] = hn32[:, s * LANES:(s + 1) * LANES]
    lane = lax.broadcasted_iota(jnp.int32, (tm, LANES), 1)
    logits = jnp.where(lane < N_EXPERTS, _dot(hn, wr_ref[...]) + br_ref[...], -jnp.inf)
    idx_out = jnp.zeros((tm, LANES), jnp.int32)
    val_out = jnp.zeros((tm, LANES), F32)
    onehot = jnp.zeros((tm, LANES), F32)
    picks = []
    for kk in range(TOP_K):
        mx = jnp.max(logits, axis=-1, keepdims=True)
        pick = jnp.min(jnp.where(logits == mx, lane, LANES), axis=-1, keepdims=True)
        hit = lane == pick
        logits = jnp.where(hit, -jnp.inf, logits)
        onehot = jnp.where(hit, 1.0, onehot)
        idx_out = jnp.where(lane == kk, pick, idx_out)
        val_out = jnp.where(lane == kk, mx, val_out)
        picks.append(hit)
        if kk == 0:
            top = mx
    e = jnp.where(lane < TOP_K, jnp.exp(val_out - top), 0.0)
    gate_ref[...] = e * (1.0 / jnp.sum(e, axis=-1, keepdims=True))
    idx_ref[...] = idx_out
    row = lax.broadcasted_iota(jnp.int32, (tm, tm), 0)
    col = lax.broadcasted_iota(jnp.int32, (tm, tm), 1)
    strict = jnp.where(row > col, 1.0, 0.0).astype(BF16)
    before = _dot(strict, onehot.astype(BF16)) + carry_ref[...]
    rank_out = jnp.zeros((tm, LANES), jnp.int32)
    for kk in range(TOP_K):
        rk = jnp.sum(jnp.where(picks[kk], before, 0.0), axis=-1, keepdims=True)
        rank_out = jnp.where(lane == kk, rk.astype(jnp.int32), rank_out)
    rank_ref[...] = rank_out
    carry_ref[...] = carry_ref[...] + jnp.sum(onehot, axis=0, keepdims=True)
    cnt_ref[...] = carry_ref[...].astype(jnp.int32)


def _merge_route(x, og, od, sa, sb, wg, wd, wo, gf, wr, br, counts_in):
    n, d = x.shape
    tm = _row_tile(n, 256)
    row = lambda w: pl.BlockSpec((tm, w), lambda i: (i, 0))
    return pl.pallas_call(
        _merge_route_kernel,
        grid=(n // tm,),
        in_specs=[row(d)] * 5 + [_full(wg.shape), _full(wd.shape), _full(wo.shape), _full(gf.shape),
                                 _full(wr.shape), _full(br.shape), _full(counts_in.shape)],
        out_specs=[row(d), pl.BlockSpec((tm * d // LANES, LANES), lambda i: (i, 0)),
                   row(LANES), row(LANES), row(LANES), _full((1, LANES))],
        out_shape=[jax.ShapeDtypeStruct((n, d), F32), jax.ShapeDtypeStruct((n * d // LANES, LANES), F32),
                   jax.ShapeDtypeStruct((n, LANES), jnp.int32), jax.ShapeDtypeStruct((n, LANES), F32),
                   jax.ShapeDtypeStruct((n, LANES), jnp.int32), jax.ShapeDtypeStruct((1, LANES), jnp.int32)],
        scratch_shapes=[pltpu.VMEM((1, LANES), F32)],
        compiler_params=_cparams(("arbitrary",)),
        name="merge_route",
    )(x, og, od, sa, sb, wg, wd, wo, gf, wr, br, counts_in)


def _pair_shuffle(h):
    a, b = h[:, :LANES], h[:, LANES:]
    even = (lax.broadcasted_iota(jnp.int32, (1, LANES), 1) % 2) == 0
    glu = jnp.where(even, a, pltpu.roll(b, 1, 1))
    lin = jnp.where(even, pltpu.roll(a, LANES - 1, 1), b)
    return glu, lin


def _moe_kernel(be_ref, first_ref, nu_ref, x_ref, wu_ref, bu_ref, wd_ref, bd_ref, y_ref,
                xs_ref, wub_ref, wdb_ref, wds_ref):
    i = pl.program_id(0)
    bm, d = xs_ref.shape
    de = wdb_ref.shape[0]
    n_tiles = d // LANES
    half = LANES // 2

    @pl.when((i < nu_ref[0]) & (first_ref[i] == 1))
    def _():
        wub_ref[...] = wu_ref[0].astype(BF16)
        for t in range(n_tiles):
            lanes = slice(t * LANES, (t + 1) * LANES)
            for c in range(de // LANES):
                for a in range(2):
                    src = c * LANES + a * half
                    wds_ref[t, pl.ds(c * LANES + a, half, stride=2), :] = wd_ref[0, src:src + half, lanes]
            wdb_ref[:, lanes] = wds_ref[t].astype(BF16)

    @pl.when(i < nu_ref[0])
    def _():
        for s in range(n_tiles):
            xs_ref[:, s * LANES:(s + 1) * LANES] = x_ref[pl.ds(s, bm, stride=SUBLANES), :].astype(BF16)
        h = _dot(xs_ref[...], wub_ref[...]) + bu_ref[0]
        acts = []
        for c in range(h.shape[1] // (2 * LANES)):
            glu, lin = _pair_shuffle(h[:, 2 * c * LANES:2 * (c + 1) * LANES])
            x_glu = jnp.minimum(glu, SWIGLU_LIMIT)
            x_lin = jnp.clip(lin, -SWIGLU_LIMIT, SWIGLU_LIMIT)
            acts.append((x_glu * _sigmoid(SWIGLU_ALPHA * x_glu) * (x_lin + 1.0)).astype(BF16))
        y = _dot(jnp.concatenate(acts, axis=1), wdb_ref[...]) + bd_ref[0]
        for s in range(n_tiles):
            y_ref[pl.ds(s, bm, stride=SUBLANES), :] = y[:, s * LANES:(s + 1) * LANES]

    @pl.when(i >= nu_ref[0])
    def _():
        y_ref[...] = jnp.zeros(y_ref.shape, F32)


def _moe_experts(blk_exp, blk_first, n_used, xs, wu, bu, wd, bd, d):
    tiles = d // LANES
    n_rows = xs.shape[0] // tiles
    de2 = wu.shape[2]
    wspec = lambda a, b: pl.BlockSpec((1, a, b), lambda i, be, bf, nu: (be[i], 0, 0))
    rows = pl.BlockSpec((MOE_ROWS * tiles, LANES), lambda i, be, bf, nu: (i, 0))
    grid_spec = pltpu.PrefetchScalarGridSpec(
        num_scalar_prefetch=3,
        grid=(n_rows // MOE_ROWS,),
        in_specs=[rows, wspec(d, de2), wspec(1, de2), wspec(de2 // 2, d), wspec(1, d)],
        out_specs=rows,
        scratch_shapes=[pltpu.VMEM((MOE_ROWS, d), BF16), pltpu.VMEM((d, de2), BF16),
                        pltpu.VMEM((de2 // 2, d), BF16), pltpu.VMEM((d // LANES, de2 // 2, LANES), F32)],
    )
    return pl.pallas_call(
        _moe_kernel,
        grid_spec=grid_spec,
        out_shape=jax.ShapeDtypeStruct(xs.shape, F32),
        compiler_params=pltpu.CompilerParams(dimension_semantics=("arbitrary",), vmem_limit_bytes=MOE_VMEM_LIMIT),
        name="moe_experts",
    )(blk_exp, blk_first, n_used, xs, wu, bu, wd, bd)


def _combine_kernel(h_ref, gate_ref, y0_ref, y1_ref, y2_ref, y3_ref, gf_ref, o_ref):
    tm, d = h_ref.shape
    gate = gate_ref[...]
    parts = []
    for s in range(d // LANES):
        acc = h_ref[:, s * LANES:(s + 1) * LANES]
        for kk, y_ref in enumerate((y0_ref, y1_ref, y2_ref, y3_ref)):
            acc = acc + gate[:, kk:kk + 1] * y_ref[0, pl.ds(s, tm, stride=SUBLANES), :]
        parts.append(acc)
    o_ref[...] = _rms(jnp.concatenate(parts, axis=1), gf_ref[...])


def _combine(h, gate, yg, gf, tok0):
    n, d = h.shape
    tm = _row_tile(n, 256)
    assert tok0 % tm == 0
    b0 = tok0 // tm
    tiles = d // LANES
    row = lambda w: pl.BlockSpec((tm, w), lambda i: (i, 0))
    ysp = lambda kk: pl.BlockSpec((1, tm * tiles, LANES), lambda i: (kk, b0 + i, 0))
    return pl.pallas_call(
        _combine_kernel,
        grid=(n // tm,),
        in_specs=[row(d), row(LANES)] + [ysp(kk) for kk in range(TOP_K)] + [_full(gf.shape)],
        out_specs=row(d),
        out_shape=jax.ShapeDtypeStruct((n, d), F32),
        compiler_params=_cparams(("parallel",)),
        name="combine",
    )(h, gate, yg, yg, yg, yg, gf)


def _rope_tables(pos):
    half = DIFF_DH // 2
    inv_freq = ROPE_THETA ** (-jnp.arange(half, dtype=F32) / half)
    ang = pos.astype(F32)[:, None] * inv_freq[None, :]
    cos, sin = jnp.cos(ang), jnp.sin(ang)
    reps = LANES // DIFF_DH
    return jnp.tile(jnp.concatenate([cos, cos], axis=1), (1, reps)), jnp.tile(jnp.concatenate([-sin, sin], axis=1), (1, reps))


def kernel(x_prompt, x_sample, cache_k, cache_v, state_gla, page_table, norm_attn, w_in, gla_gate_up, gla_gate_bias, gla_norm, lam_q1, lam_k1, lam_q2, lam_k2, diff_subln, w_branch_gla, w_branch_diff, w_out, norm_ffn, w_router, b_router, w_up, b_up, w_down, b_down, norm_final):
    depth = w_in.shape[0]
    assert depth == 1, "single-layer step"
    layer = 0
    b_p, t_p, d = x_prompt.shape
    b_s, t_s, _ = x_sample.shape
    n_p, n_s = b_p * t_p, b_s * t_s
    n = n_p + n_s
    n_pool, page = cache_k.shape[1], cache_k.shape[2]
    n_pages = page_table.shape[1]
    past_len = n_pages * page
    kd, vd = GLA_HEADS * GLA_DK, GLA_HEADS * GLA_DV
    tiles = d // LANES
    row2 = lambda a: a.reshape(1, -1)
    assert t_s <= DIFF_HEADS and t_s * DIFF_HEADS <= LANES

    splits = (kd, kd, vd, GLA_RANK, vd, d, d, d, d, d)
    offs = [0]
    for s in splits:
        offs.append(offs[-1] + s)
    wb = w_in.reshape(d, -1).astype(BF16)
    w_gq, w_gk, w_gv, w_glr, w_gr, w_dq, w_dk, w_dv, w_ga, w_gb = [wb[:, offs[j]:offs[j + 1]] for j in range(10)]
    w_glr = jnp.pad(w_glr, ((0, 0), (0, LANES - GLA_RANK)))
    gup = jnp.pad(gla_gate_up.reshape(GLA_RANK, kd).astype(BF16), ((0, LANES - GLA_RANK), (0, 0)))
    g_attn = row2(norm_attn)
    gn = row2(gla_norm)
    lam_init = 0.8 - 0.6 * math.exp(-0.3 * layer)
    lams = (row2(lam_q1), row2(lam_k1), row2(lam_q2), row2(lam_k2))
    subln = row2(diff_subln)
    w_bg = w_branch_gla.reshape(vd, d).astype(BF16)
    w_bd = w_branch_diff.reshape(d, d).astype(BF16)
    w_o = w_out.reshape(d, d).astype(BF16)
    w_r = jnp.pad(w_router.reshape(d, N_EXPERTS).astype(BF16), ((0, 0), (0, LANES - N_EXPERTS)))
    b_r = jnp.pad(row2(b_router), ((0, 0), (0, LANES - N_EXPERTS)))
    g_ffn = row2(norm_ffn)

    def mixers(x, pos, seq):
        gq, gk, gv, gla_la, gr = _gla_proj(x, g_attn, w_gq, w_gk, w_gv, w_glr, w_gr, gup, row2(gla_gate_bias))
        cos, sin_signed = _rope_tables(pos)
        return (gq, gk, gv, gla_la, gr) + tuple(
            _diff_proj(x, g_attn, cos, sin_signed, w_dq, w_dk, w_dv, w_ga, w_gb, seq))

    x_p = x_prompt.reshape(n_p, d)
    gq, gk, gv, gla_la, gr, dq, dk_b, dv_b, kt_p, v_cache_p, sig_a, sig_b = mixers(
        x_p, jnp.tile(jnp.arange(t_p, dtype=jnp.int32), b_p), t_p)
    og_p, s_p = _gla_chunks(gq, gk, gv, gla_la, gr, gn, None, b_p, t_p, GLA_CHUNK if t_p % GLA_CHUNK == 0 else t_p)
    od_p = _diff_prompt(dq, dk_b, dv_b, lams, subln, b_p, t_p, lam_init)
    zero_counts = jnp.zeros((1, LANES), jnp.int32)
    h_p, hn_p, idx_p, gate_p, rank_p, counts_p = _merge_route(
        x_p, og_p, od_p, sig_a, sig_b, w_bg, w_bd, w_o, g_ffn, w_r, b_r, zero_counts)

    x_s = x_sample.reshape(n_s, d)
    gq, gk, gv, gla_la, gr, dq, dk_b, dv_b, kt_s, v_cache_s, sig_a, sig_b = mixers(
        x_s, jnp.tile(past_len + jnp.arange(t_s, dtype=jnp.int32), b_s), n_s)
    t_pad = -(-t_s // GLA_SAMPLE_CHUNK) * GLA_SAMPLE_CHUNK
    pad_s = lambda a: jnp.pad(a.reshape(b_s, t_s, -1), ((0, 0), (0, t_pad - t_s), (0, 0))).reshape(b_s * t_pad, -1)
    og_s, s_s = _gla_chunks(pad_s(gq), pad_s(gk), pad_s(gv), pad_s(gla_la), pad_s(gr), gn,
                            state_gla.reshape(b_s, GLA_HEADS, GLA_DK, GLA_DV), b_s, t_pad, t_pad)
    og_s = og_s.reshape(b_s, t_pad, vd)[:, :t_s].reshape(n_s, vd)
    q4 = dq.reshape(b_s, t_s, DIFF_QK_HEADS, DIFF_DH).transpose(0, 2, 1, 3)
    q_bd = (q4[:, :, :, None, :] * jnp.eye(DIFF_QK_HEADS, dtype=BF16)[None, :, None, :, None]).reshape(
        b_s, DIFF_QK_HEADS * t_s, d)
    kn = jnp.pad(dk_b.reshape(b_s, t_s, d).transpose(0, 2, 1), ((0, 0), (0, 0), (0, LANES - t_s)))
    vn = jnp.pad(dv_b.reshape(b_s, t_s, DIFF_HEADS, LANES).transpose(0, 2, 1, 3),
                 ((0, 0), (0, 0), (0, LANES - t_s), (0, 0)))
    cache_kt = jnp.transpose(cache_k, (0, 1, 3, 4, 2)).reshape(depth * n_pool, DIFF_QK_HEADS * DIFF_DH, page)
    od_s = _diff_sample(page_table, q_bd, kn, vn, lams, subln, cache_kt,
                        cache_v.reshape(depth * n_pool, page * DIFF_HEADS, LANES), lam_init, t_s)
    od_s = od_s.reshape(b_s, DIFF_HEADS, 2, t_s, LANES)[:, :, 0].transpose(0, 2, 1, 3).reshape(n_s, d).astype(BF16)
    h_s, hn_s, idx_s, gate_s, rank_s, counts = _merge_route(
        x_s, og_s, od_s, sig_a, sig_b, w_bg, w_bd, w_o, g_ffn, w_r, b_r, counts_p)

    counts = counts[0, :N_EXPERTS]
    padded = (counts + MOE_ROWS - 1) // MOE_ROWS * MOE_ROWS
    pad_end = jnp.cumsum(padded)
    pad_start = pad_end - padded
    idx4 = jnp.concatenate([idx_p[:, :TOP_K], idx_s[:, :TOP_K]], axis=0)
    rank4 = jnp.concatenate([rank_p[:, :TOP_K], rank_s[:, :TOP_K]], axis=0)
    dest = pad_start[idx4] + rank4
    n_rows = -(-(n * TOP_K + N_EXPERTS * (MOE_ROWS - 1)) // MOE_ROWS) * MOE_ROWS
    n_blocks = n_rows // MOE_ROWS
    blk_start = jnp.arange(n_blocks, dtype=jnp.int32) * MOE_ROWS
    blk_exp = jnp.minimum(jnp.sum((blk_start[:, None] >= pad_end[None, :]).astype(jnp.int32), axis=1), N_EXPERTS - 1)
    n_used = (pad_end[-1:] // MOE_ROWS).astype(jnp.int32)
    row_tok = jnp.zeros((n_rows,), jnp.int32).at[dest.reshape(-1)].set(
        jnp.repeat(jnp.arange(n, dtype=jnp.int32), TOP_K))
    hn_all = jnp.concatenate([hn_p, hn_s], axis=0).reshape(n, tiles, LANES)
    xs = jnp.take(hn_all, row_tok, axis=0, mode='clip').reshape(n_rows * tiles, LANES)

    de = w_down.shape[2]
    blk_first = jnp.concatenate([jnp.ones((1,), jnp.int32), (blk_exp[1:] != blk_exp[:-1]).astype(jnp.int32)])
    y = _moe_experts(blk_exp, blk_first, n_used, xs, w_up.reshape(N_EXPERTS, d, 2 * de),
                     b_up.reshape(N_EXPERTS, 1, 2 * de), w_down.reshape(N_EXPERTS, de, d),
                     b_down.reshape(N_EXPERTS, 1, d), d)
    yg = jnp.take(y.reshape(n_rows, tiles, LANES), dest.T.reshape(-1), axis=0, mode='clip').reshape(TOP_K, n * tiles, LANES)

    g_fin = row2(norm_final)
    y_prompt = _combine(h_p, gate_p, yg, g_fin, 0).reshape(b_p, t_p, d)
    y_sample = _combine(h_s, gate_s, yg, g_fin, n_p).reshape(b_s, t_s, d)
    k_prompt = kt_p.reshape(1, b_p, DIFF_QK_HEADS, DIFF_DH, t_p).transpose(0, 1, 4, 2, 3)
    v_prompt = v_cache_p.reshape(1, b_p, t_p, DIFF_HEADS, 2 * DIFF_DH)
    k_sample = kt_s.reshape(1, DIFF_QK_HEADS, DIFF_DH, b_s, t_s).transpose(0, 3, 4, 1, 2)
    v_sample = v_cache_s.reshape(1, b_s, t_s, DIFF_HEADS, 2 * DIFF_DH)
    return (y_prompt, y_sample, k_prompt, v_prompt, s_p[None], k_sample, v_sample, s_s[None])
```

```python
import functools
import math

import jax
import jax.numpy as jnp
from jax import lax
from jax.experimental import pallas as pl
from jax.experimental.pallas import tpu as pltpu

F32 = jnp.float32
BF16 = jnp.bfloat16

EPS = 1e-6
GLA_HEADS = 4
GLA_DK = 128
GLA_DV = 256
GLA_RANK = 16
GLA_NORMALIZER = 16.0
GLA_CHUNK = 64
GLA_SAMPLE_CHUNK = 16
DIFF_HEADS = 8
DIFF_QK_HEADS = 2 * DIFF_HEADS
DIFF_DH = 64
ROPE_THETA = 10000.0
N_EXPERTS = 32
TOP_K = 4
SWIGLU_ALPHA = 1.702
SWIGLU_LIMIT = 7.0
LOG2E = math.log2(math.e)

LANES = 128
SUBLANES = 8
MOE_ROWS = 256
ATTN_Q = 2048
ATTN_K = 512
ATTN_SUB = 256
ATTN_LOOKAHEAD = 3
PAGES_PER_STEP = 8
VMEM_LIMIT = 48 * 1024 * 1024
MOE_VMEM_LIMIT = 56 * 1024 * 1024


def _cparams(sem):
    return pltpu.CompilerParams(dimension_semantics=sem, vmem_limit_bytes=VMEM_LIMIT)


def _dot(a, b):
    return jnp.dot(a, b, preferred_element_type=F32)


def _dot_nt(a, b):
    return lax.dot_general(a, b, (((1,), (1,)), ((), ())), preferred_element_type=F32)


def _dot_tn(a, b):
    return lax.dot_general(a, b, (((0,), (0,)), ((), ())), preferred_element_type=F32)


def _rms(x, g):
    ms = jnp.mean(x * x, axis=-1, keepdims=True)
    return x * lax.rsqrt(ms + EPS) * g


def _sigmoid(x):
    return 1.0 / (1.0 + jnp.exp(-x))


def _row_tile(n, cap=512):
    for t in (512, 256, 128, 64, 32, 16, 8):
        if t <= cap and n % t == 0:
            return t
    raise ValueError(f"token count {n} is not a multiple of 8")


def _full(shape):
    return pl.BlockSpec(shape, lambda *_: (0,) * len(shape))


def _gla_proj_kernel(x_ref, g_ref, wq_ref, wk_ref, wv_ref, wlr_ref, wr_ref, gup_ref, gb_ref,
                     q_ref, k_ref, v_ref, la_ref, r_ref):
    xn = _rms(x_ref[...], g_ref[...]).astype(BF16)
    q_ref[...] = _dot(xn, wq_ref[...]) * (GLA_DK ** -0.5)
    k_ref[...] = _dot(xn, wk_ref[...])
    v_ref[...] = _dot(xn, wv_ref[...]).astype(BF16)
    glr = _dot(xn, wlr_ref[...]).astype(BF16)
    z = _dot(glr, gup_ref[...]) + gb_ref[...]
    log_sig = jnp.minimum(z, 0.0) - jnp.log1p(jnp.exp(-jnp.abs(z)))
    la_ref[...] = log_sig * (1.0 / GLA_NORMALIZER)
    r_ref[...] = _dot(xn, wr_ref[...])


def _gla_proj(x, g, wq, wk, wv, wlr, wr, gup, gb):
    n, d = x.shape
    tm = _row_tile(n)
    kd, vd = wq.shape[1], wv.shape[1]
    row = lambda w: pl.BlockSpec((tm, w), lambda i: (i, 0))
    return pl.pallas_call(
        _gla_proj_kernel,
        grid=(n // tm,),
        in_specs=[row(d), _full(g.shape), _full(wq.shape), _full(wk.shape), _full(wv.shape),
                  _full(wlr.shape), _full(wr.shape), _full(gup.shape), _full(gb.shape)],
        out_specs=[row(kd), row(kd), row(vd), row(kd), row(vd)],
        out_shape=[jax.ShapeDtypeStruct((n, kd), F32), jax.ShapeDtypeStruct((n, kd), F32),
                   jax.ShapeDtypeStruct((n, vd), BF16), jax.ShapeDtypeStruct((n, kd), F32),
                   jax.ShapeDtypeStruct((n, vd), F32)],
        compiler_params=_cparams(("parallel",)),
        name="gla_proj",
    )(x, g, wq, wk, wv, wlr, wr, gup, gb)


def _rope_chunk(xc, cos, sin_signed, first_half):
    swapped = jnp.where(first_half, pltpu.roll(xc, LANES - DIFF_DH // 2, 1), pltpu.roll(xc, DIFF_DH // 2, 1))
    return xc * cos + swapped * sin_signed


def _diff_proj_kernel(x_ref, g_ref, cos_ref, sin_ref, wq_ref, wk_ref, wv_ref, wa_ref, wb_ref,
                      q_ref, kb_ref, vb_ref, kt_ref, vc_ref, sa_ref, sb_ref):
    tm = x_ref.shape[0]
    xn = _rms(x_ref[...], g_ref[...]).astype(BF16)
    cos = cos_ref[...]
    sin_signed = sin_ref[...]
    lane = lax.broadcasted_iota(jnp.int32, (1, LANES), 1)
    first_half = (lane % DIFF_DH) < (DIFF_DH // 2)
    n_chunks = q_ref.shape[1] // LANES
    q = _dot(xn, wq_ref[...])
    for c in range(n_chunks):
        sl = slice(c * LANES, (c + 1) * LANES)
        q_ref[:, sl] = (_rope_chunk(q[:, sl], cos, sin_signed, first_half) * (DIFF_DH ** -0.5 * LOG2E)).astype(BF16)
    k = _dot(xn, wk_ref[...])
    for c in range(n_chunks):
        sl = slice(c * LANES, (c + 1) * LANES)
        kr = _rope_chunk(k[:, sl], cos, sin_signed, first_half)
        kb_ref[:, sl] = kr.astype(BF16)
        kt_ref[0, sl, :] = kr.T
    v = _dot(xn, wv_ref[...])
    vb_ref[...] = v.astype(BF16)
    for c in range(n_chunks):
        vc_ref[pl.ds(c, tm, stride=DIFF_HEADS), :] = v[:, c * LANES:(c + 1) * LANES]
    sa_ref[...] = _sigmoid(_dot(xn, wa_ref[...]))
    sb_ref[...] = _sigmoid(_dot(xn, wb_ref[...]))


def _diff_proj(x, g, cos, sin_signed, wq, wk, wv, wa, wb, seq):
    n, d = x.shape
    tm = _row_tile(seq, 256)
    tps = seq // tm
    row = lambda w: pl.BlockSpec((tm, w), lambda i: (i, 0))
    pos_row = pl.BlockSpec((tm, LANES), lambda i: (i % (cos.shape[0] // tm), 0))
    sds = lambda dt: jax.ShapeDtypeStruct((n, d), dt)
    return pl.pallas_call(
        _diff_proj_kernel,
        grid=(n // tm,),
        in_specs=[row(d), _full(g.shape), pos_row, pos_row] + [_full(wq.shape)] * 5,
        out_specs=[row(d), row(d), row(d),
                   pl.BlockSpec((1, d, tm), lambda i: (i // tps, 0, i % tps)),
                   pl.BlockSpec((tm * DIFF_HEADS, LANES), lambda i: (i, 0)),
                   row(d), row(d)],
        out_shape=[sds(BF16), sds(BF16), sds(BF16),
                   jax.ShapeDtypeStruct((n // seq, d, seq), F32),
                   jax.ShapeDtypeStruct((n * DIFF_HEADS, LANES), F32),
                   sds(F32), sds(F32)],
        compiler_params=_cparams(("parallel",)),
        name="diff_proj",
    )(x, g, cos, sin_signed, wq, wk, wv, wa, wb)


def _gla_kernel(has_s0, *refs):
    if has_s0:
        q_ref, k_ref, v_ref, la_ref, r_ref, gn_ref, s0_ref, o_ref, sfin_ref, st_ref = refs
    else:
        q_ref, k_ref, v_ref, la_ref, r_ref, gn_ref, o_ref, sfin_ref, st_ref = refs
    n = pl.program_id(1)
    c = q_ref.shape[0]

    @pl.when(n == 0)
    def _():
        for h in range(GLA_HEADS):
            if has_s0:
                st_ref[h] = s0_ref[0, h].T
            else:
                st_ref[h] = jnp.zeros(st_ref.shape[1:], F32)

    row = lax.broadcasted_iota(jnp.int32, (c, c), 0)
    col = lax.broadcasted_iota(jnp.int32, (c, c), 1)
    causal = row >= col
    tri = jnp.where(causal, 1.0, 0.0).astype(BF16)
    gn = gn_ref[...]
    heads = range(GLA_HEADS)
    ks = [slice(h * GLA_DK, (h + 1) * GLA_DK) for h in heads]
    vs = [slice(h * GLA_DV, (h + 1) * GLA_DV) for h in heads]
    bcum = []
    for h in heads:
        la = la_ref[:, ks[h]]
        hi = la.astype(BF16)
        rem = la - hi.astype(F32)
        mid = rem.astype(BF16)
        lo = (rem - mid.astype(F32)).astype(BF16)
        bcum.append(_dot(tri, hi) + _dot(tri, mid) + _dot(tri, lo))
    q_t = [(q_ref[:, ks[h]] * jnp.exp(bcum[h])).astype(BF16) for h in heads]
    k_t = [(k_ref[:, ks[h]] * jnp.exp(-bcum[h])).astype(BF16) for h in heads]
    b_last = [bcum[h][c - 1:c, :] for h in heads]
    k_dec = [(k_ref[:, ks[h]] * jnp.exp(b_last[h] - bcum[h])).astype(BF16) for h in heads]
    st = [st_ref[h] for h in heads]
    scores = [jnp.where(causal, _dot_nt(q_t[h], k_t[h]), 0.0).astype(BF16) for h in heads]
    o_state = [_dot_nt(q_t[h], st[h].astype(BF16)) for h in heads]
    upd = [_dot_tn(v_ref[:, vs[h]], k_dec[h]) for h in heads]
    o_intra = [_dot(scores[h], v_ref[:, vs[h]]) for h in heads]
    for h in heads:
        st_ref[h] = st[h] * jnp.exp(b_last[h]) + upd[h]
        rh = r_ref[:, vs[h]]
        o_ref[:, vs[h]] = (_rms(o_state[h] + o_intra[h], gn) * (rh * _sigmoid(rh))).astype(BF16)

    @pl.when(n == pl.num_programs(1) - 1)
    def _():
        for h in range(GLA_HEADS):
            sfin_ref[0, h] = st_ref[h].T


def _gla_chunks(q, k, v, la, r, gn, s0, batch, seq, chunk):
    nc = seq // chunk
    kd, vd = q.shape[1], v.shape[1]
    blk = lambda w: pl.BlockSpec((chunk, w), lambda b, n: (b * nc + n, 0))
    state_spec = pl.BlockSpec((1, GLA_HEADS, GLA_DK, GLA_DV), lambda b, n: (b, 0, 0, 0))
    in_specs = [blk(kd), blk(kd), blk(vd), blk(kd), blk(vd), _full(gn.shape)]
    args = [q, k, v, la, r, gn]
    if s0 is not None:
        in_specs.append(state_spec)
        args.append(s0)
    return pl.pallas_call(
        functools.partial(_gla_kernel, s0 is not None),
        grid=(batch, nc),
        in_specs=in_specs,
        out_specs=[blk(vd), state_spec],
        out_shape=[jax.ShapeDtypeStruct((batch * seq, vd), BF16),
                   jax.ShapeDtypeStruct((batch, GLA_HEADS, GLA_DK, GLA_DV), F32)],
        scratch_shapes=[pltpu.VMEM((GLA_HEADS, GLA_DV, GLA_DK), F32)],
        compiler_params=_cparams(("parallel", "arbitrary")),
        name="gla_chunks",
    )(*args)


def _lambda(lq1_ref, lk1_ref, lq2_ref, lk2_ref, lam_init):
    s1 = jnp.sum(lq1_ref[...] * lk1_ref[...], axis=-1, keepdims=True)
    s2 = jnp.sum(lq2_ref[...] * lk2_ref[...], axis=-1, keepdims=True)
    return jnp.exp(s1) - jnp.exp(s2) + lam_init


def _diff_prompt_kernel(lam_init, tk, sub, q_ref, k_ref, v_ref, lq1_ref, lk1_ref, lq2_ref, lk2_ref, sub_ref,
                        o_ref, m_ref, a_ref):
    qi = pl.program_id(2)
    tq = q_ref.shape[0]
    lane = lax.broadcasted_iota(jnp.int32, (1, LANES), 1)
    m_ref[...] = jnp.full(m_ref.shape, -jnp.inf, F32)
    a_ref[...] = jnp.zeros(a_ref.shape, F32)

    def scores(mp, r0, k0, nk, mask):
        q = q_ref[pl.ds(r0, sub), :]
        qm = jnp.where((lane < DIFF_DH) if mp == 0 else (lane >= DIFF_DH), q, jnp.zeros_like(q))
        s = _dot_nt(qm, k_ref[pl.ds(k0, nk), :])
        return s if mask is None else jnp.where(mask, s, -jnp.inf)

    def update(mp, r0, k0, nk, s):
        rows = pl.ds(r0, sub)
        n_chunks = nk // LANES
        mx = s[:, :LANES]
        for c in range(1, n_chunks):
            mx = jnp.maximum(mx, s[:, c * LANES:(c + 1) * LANES])
        m_old = m_ref[mp, rows, :]
        m_new = jnp.maximum(m_old, jnp.max(mx, axis=-1, keepdims=True))
        alpha = jnp.exp2(m_old - m_new)
        p = jnp.concatenate(
            [jnp.exp2(s[:, c * LANES:(c + 1) * LANES] - m_new).astype(BF16) for c in range(n_chunks)], axis=1)
        m_ref[mp, rows, :] = m_new
        vj = v_ref[pl.ds(k0, nk), :]
        v_ones = jnp.concatenate([vj, jnp.ones_like(vj)], axis=1)
        a_ref[mp, rows, :] = jnp.concatenate([alpha, alpha], axis=1) * a_ref[mp, rows, :] + _dot(p, v_ones)

    def run(chains):
        pending = {}
        for i in range(min(ATTN_LOOKAHEAD, len(chains))):
            pending[i] = scores(*chains[i])
        for i, (mp, r0, k0, nk, _) in enumerate(chains):
            if i + ATTN_LOOKAHEAD < len(chains):
                pending[i + ATTN_LOOKAHEAD] = scores(*chains[i + ATTN_LOOKAHEAD])
            update(mp, r0, k0, nk, pending.pop(i))

    n_sub = tq // sub

    def body(j, carry):
        k0 = pl.multiple_of(j * tk, tk)
        run([(mp, si * sub, k0, tk, None) for si in range(n_sub) for mp in range(2)])
        return carry

    lax.fori_loop(0, qi * (tq // tk), body, 0)

    base = pl.multiple_of(qi * tq, tq)
    chains = []
    for si in range(n_sub):
        need = (si + 1) * sub
        for k0 in range(0, need, tk):
            nk = min(tk, need - k0)
            mask = None
            if k0 + nk == need:
                row = lax.broadcasted_iota(jnp.int32, (sub, nk), 0) + si * sub
                col = lax.broadcasted_iota(jnp.int32, (sub, nk), 1) + k0
                mask = row >= col
            chains += [(mp, si * sub, base + k0, nk, mask) for mp in range(2)]
    run(chains)

    lam = _lambda(lq1_ref, lk1_ref, lq2_ref, lk2_ref, lam_init)
    o1 = a_ref[0, :, :LANES] * (1.0 / a_ref[0, :, LANES:])
    o2 = a_ref[1, :, :LANES] * (1.0 / a_ref[1, :, LANES:])
    o_ref[...] = (_rms(o1 - lam * o2, sub_ref[...]) * (1.0 - lam_init)).astype(BF16)


def _diff_prompt(q, k, v, lams, subln, batch, seq, lam_init):
    tq = min(ATTN_Q, seq)
    tk = min(ATTN_K, tq)
    sub = min(ATTN_SUB, tk)
    nq = seq // tq
    vec = _full(lams[0].shape)
    return pl.pallas_call(
        functools.partial(_diff_prompt_kernel, lam_init, tk, sub),
        grid=(batch, DIFF_HEADS, nq),
        in_specs=[pl.BlockSpec((tq, LANES), lambda b, h, i: (b * nq + i, h)),
                  pl.BlockSpec((seq, LANES), lambda b, h, i: (b, h)),
                  pl.BlockSpec((seq, LANES), lambda b, h, i: (b, h)),
                  vec, vec, vec, vec, _full(subln.shape)],
        out_specs=pl.BlockSpec((tq, LANES), lambda b, h, i: (b * nq + i, h)),
        out_shape=jax.ShapeDtypeStruct((batch * seq, DIFF_HEADS * LANES), BF16),
        scratch_shapes=[pltpu.VMEM((2, tq, LANES), F32), pltpu.VMEM((2, tq, 2 * LANES), F32)],
        compiler_params=_cparams(("parallel", "parallel", "arbitrary")),
        name="diff_prompt",
    )(q, k, v, *lams, subln)


def _diff_sample_kernel(lam_init, t_new, pt_ref, q_ref, kn_ref, vn_ref, lq1_ref, lk1_ref, lq2_ref, lk2_ref, sub_ref,
                        *rest):
    npg = PAGES_PER_STEP
    k_refs, v_refs = rest[:npg], rest[npg:2 * npg]
    o_ref, m_ref, l_ref, acc_ref = rest[2 * npg:]
    g = pl.program_id(1)
    rows = q_ref.shape[1]
    grp = rows // DIFF_HEADS
    page = k_refs[0].shape[2]

    @pl.when(g == 0)
    def _():
        m_ref[...] = jnp.full(m_ref.shape, -jnp.inf, F32)
        l_ref[...] = jnp.zeros(l_ref.shape, F32)
        acc_ref[...] = jnp.zeros(acc_ref.shape, F32)

    q = q_ref[0]

    def attend(s, v_of_head):
        m_old = m_ref[...]
        m_new = jnp.maximum(m_old, jnp.max(s, axis=-1, keepdims=True))
        p = jnp.exp2(s - m_new)
        alpha = jnp.exp2(m_old - m_new)
        l_ref[...] = alpha * l_ref[...] + jnp.sum(p, axis=-1, keepdims=True)
        pv = [_dot(p[h * grp:(h + 1) * grp].astype(BF16), v_of_head(h)) for h in range(DIFF_HEADS)]
        acc_ref[...] = alpha * acc_ref[...] + jnp.concatenate(pv, axis=0)
        m_ref[...] = m_new

    kcat = jnp.concatenate([kr[0].astype(BF16) for kr in k_refs], axis=1)
    attend(_dot(q, kcat), lambda h: jnp.concatenate(
        [vr[0, pl.ds(h, page, stride=DIFF_HEADS), :].astype(BF16) for vr in v_refs], axis=0))

    @pl.when(g == pl.num_programs(1) - 1)
    def _():
        s_new = _dot(q, kn_ref[0])
        r = lax.broadcasted_iota(jnp.int32, s_new.shape, 0)
        c = lax.broadcasted_iota(jnp.int32, s_new.shape, 1)
        attend(jnp.where(c <= r % t_new, s_new, -jnp.inf), lambda h: vn_ref[0, h])
        lam = _lambda(lq1_ref, lk1_ref, lq2_ref, lk2_ref, lam_init)
        an = acc_ref[...] * (1.0 / l_ref[...])
        diff = an - lam * pltpu.roll(an, rows - t_new, 0)
        o_ref[0] = _rms(diff, sub_ref[...]) * (1.0 - lam_init)


def _diff_sample(page_table, q, k_new, v_new, lams, subln, cache_kt, cache_v, lam_init, t_new):
    batch, rows, _ = q.shape
    n_pages = page_table.shape[1]
    npg = PAGES_PER_STEP
    assert n_pages % npg == 0
    vec = pl.BlockSpec(lams[0].shape, lambda b, g, pt: (0, 0))
    per_batch = lambda a: pl.BlockSpec((1,) + a.shape[1:], lambda b, g, pt: (b,) + (0,) * (a.ndim - 1))

    def page_spec(a, j):
        return pl.BlockSpec((1,) + a.shape[1:], lambda b, g, pt: (pt[b * n_pages + g * npg + j], 0, 0))

    grid_spec = pltpu.PrefetchScalarGridSpec(
        num_scalar_prefetch=1,
        grid=(batch, n_pages // npg),
        in_specs=[per_batch(q), per_batch(k_new), per_batch(v_new), vec, vec, vec, vec,
                  pl.BlockSpec(subln.shape, lambda b, g, pt: (0, 0))]
        + [page_spec(cache_kt, j) for j in range(npg)] + [page_spec(cache_v, j) for j in range(npg)],
        out_specs=pl.BlockSpec((1, rows, LANES), lambda b, g, pt: (b, 0, 0)),
        scratch_shapes=[pltpu.VMEM((rows, 1), F32), pltpu.VMEM((rows, 1), F32), pltpu.VMEM((rows, LANES), F32)],
    )
    return pl.pallas_call(
        functools.partial(_diff_sample_kernel, lam_init, t_new),
        grid_spec=grid_spec,
        out_shape=jax.ShapeDtypeStruct((batch, rows, LANES), F32),
        compiler_params=_cparams(("parallel", "arbitrary")),
        name="diff_sample",
    )(page_table.reshape(-1), q, k_new, v_new, *lams, subln, *([cache_kt] * npg), *([cache_v] * npg))


def _merge_route_kernel(x_ref, og_ref, od_ref, sa_ref, sb_ref, wg_ref, wd_ref, wo_ref, gf_ref, wr_ref, br_ref,
                        cin_ref, h_ref, hn_ref, idx_ref, gate_ref, rank_ref, cnt_ref, carry_ref):
    i = pl.program_id(0)
    tm = x_ref.shape[0]

    @pl.when(i == 0)
    def _():
        carry_ref[...] = cin_ref[...].astype(F32)

    merged = sa_ref[...] * _dot(og_ref[...], wg_ref[...]) + sb_ref[...] * _dot(od_ref[...], wd_ref[...])
    h = x_ref[...] + _dot(merged.astype(BF16), wo_ref[...])
    h_ref[...] = h
    hn = _rms(h, gf_ref[...]).astype(BF16)
    hn32 = hn.astype(F32)
    for s in range(hn_ref.shape[0] // tm):
        hn_ref[pl.ds(s, tm, stride=SUBLANES), :] = hn32[:, s * LANES:(s + 1) * LANES]
    lane = lax.broadcasted_iota(jnp.int32, (tm, LANES), 1)
    logits = jnp.where(lane < N_EXPERTS, _dot(hn, wr_ref[...]) + br_ref[...], -jnp.inf)
    idx_out = jnp.zeros((tm, LANES), jnp.int32)
    val_out = jnp.zeros((tm, LANES), F32)
    onehot = jnp.zeros((tm, LANES), F32)
    picks = []
    for kk in range(TOP_K):
        mx = jnp.max(logits, axis=-1, keepdims=True)
        pick = jnp.min(jnp.where(logits == mx, lane, LANES), axis=-1, keepdims=True)
        hit = lane == pick
        logits = jnp.where(hit, -jnp.inf, logits)
        onehot = jnp.where(hit, 1.0, onehot)
        idx_out = jnp.where(lane == kk, pick, idx_out)
        val_out = jnp.where(lane == kk, mx, val_out)
        picks.append(hit)
        if kk == 0:
            top = mx
    e = jnp.where(lane < TOP_K, jnp.exp(val_out - top), 0.0)
    gate_ref[...] = e * (1.0 / jnp.sum(e, axis=-1, keepdims=True))
    idx_ref[...] = idx_out
    row = lax.broadcasted_iota(jnp.int32, (tm, tm), 0)
    col = lax.broadcasted_iota(jnp.int32, (tm, tm), 1)
    strict = jnp.where(row > col, 1.0, 0.0).astype(BF16)
    before = _dot(strict, onehot.astype(BF16)) + carry_ref[...]
    rank_out = jnp.zeros((tm, LANES), jnp.int32)
    for kk in range(TOP_K):
        rk = jnp.sum(jnp.where(picks[kk], before, 0.0), axis=-1, keepdims=True)
        rank_out = jnp.where(lane == kk, rk.astype(jnp.int32), rank_out)
    rank_ref[...] = rank_out
    carry_ref[...] = carry_ref[...] + jnp.sum(onehot, axis=0, keepdims=True)
    cnt_ref[...] = carry_ref[...].astype(jnp.int32)


def _merge_route(x, og, od, sa, sb, wg, wd, wo, gf, wr, br, counts_in):
    n, d = x.shape
    tm = _row_tile(n, 256)
    row = lambda w: pl.BlockSpec((tm, w), lambda i: (i, 0))
    return pl.pallas_call(
        _merge_route_kernel,
        grid=(n // tm,),
        in_specs=[row(d)] * 5 + [_full(wg.shape), _full(wd.shape), _full(wo.shape), _full(gf.shape),
                                 _full(wr.shape), _full(br.shape), _full(counts_in.shape)],
        out_specs=[row(d), pl.BlockSpec((tm * d // LANES, LANES), lambda i: (i, 0)),
                   row(LANES), row(LANES), row(LANES), _full((1, LANES))],
        out_shape=[jax.ShapeDtypeStruct((n, d), F32), jax.ShapeDtypeStruct((n * d // LANES, LANES), F32),
                   jax.ShapeDtypeStruct((n, LANES), jnp.int32), jax.ShapeDtypeStruct((n, LANES), F32),
                   jax.ShapeDtypeStruct((n, LANES), jnp.int32), jax.ShapeDtypeStruct((1, LANES), jnp.int32)],
        scratch_shapes=[pltpu.VMEM((1, LANES), F32)],
        compiler_params=_cparams(("arbitrary",)),
        name="merge_route",
    )(x, og, od, sa, sb, wg, wd, wo, gf, wr, br, counts_in)


def _pair_shuffle(h):
    a, b = h[:, :LANES], h[:, LANES:]
    even = (lax.broadcasted_iota(jnp.int32, (1, LANES), 1) % 2) == 0
    glu = jnp.where(even, a, pltpu.roll(b, 1, 1))
    lin = jnp.where(even, pltpu.roll(a, LANES - 1, 1), b)
    return glu, lin


def _moe_kernel(be_ref, first_ref, nu_ref, x_ref, wu_ref, bu_ref, wd_ref, bd_ref, y_ref,
                xs_ref, wub_ref, wdb_ref, wds_ref):
    i = pl.program_id(0)
    bm, d = xs_ref.shape
    de = wdb_ref.shape[0]
    n_tiles = d // LANES
    half = LANES // 2

    @pl.when((i < nu_ref[0]) & (first_ref[i] == 1))
    def _():
        wub_ref[...] = wu_ref[0].astype(BF16)
        for t in range(n_tiles):
            lanes = slice(t * LANES, (t + 1) * LANES)
            for c in range(de // LANES):
                for a in range(2):
                    src = c * LANES + a * half
                    wds_ref[t, pl.ds(c * LANES + a, half, stride=2), :] = wd_ref[0, src:src + half, lanes]
            wdb_ref[:, lanes] = wds_ref[t].astype(BF16)

    @pl.when(i < nu_ref[0])
    def _():
        for s in range(n_tiles):
            xs_ref[:, s * LANES:(s + 1) * LANES] = x_ref[pl.ds(s, bm, stride=SUBLANES), :].astype(BF16)
        h = _dot(xs_ref[...], wub_ref[...]) + bu_ref[0]
        acts = []
        for c in range(h.shape[1] // (2 * LANES)):
            glu, lin = _pair_shuffle(h[:, 2 * c * LANES:2 * (c + 1) * LANES])
            x_glu = jnp.minimum(glu, SWIGLU_LIMIT)
            x_lin = jnp.clip(lin, -SWIGLU_LIMIT, SWIGLU_LIMIT)
            acts.append((x_glu * _sigmoid(SWIGLU_ALPHA * x_glu) * (x_lin + 1.0)).astype(BF16))
        y = _dot(jnp.concatenate(acts, axis=1), wdb_ref[...]) + bd_ref[0]
        for s in range(n_tiles):
            y_ref[pl.ds(s, bm, stride=SUBLANES), :] = y[:, s * LANES:(s + 1) * LANES]

    @pl.when(i >= nu_ref[0])
    def _():
        y_ref[...] = jnp.zeros(y_ref.shape, F32)


def _moe_experts(blk_exp, blk_first, n_used, xs, wu, bu, wd, bd, d):
    tiles = d // LANES
    n_rows = xs.shape[0] // tiles
    de2 = wu.shape[2]
    wspec = lambda a, b: pl.BlockSpec((1, a, b), lambda i, be, bf, nu: (be[i], 0, 0))
    rows = pl.BlockSpec((MOE_ROWS * tiles, LANES), lambda i, be, bf, nu: (i, 0))
    grid_spec = pltpu.PrefetchScalarGridSpec(
        num_scalar_prefetch=3,
        grid=(n_rows // MOE_ROWS,),
        in_specs=[rows, wspec(d, de2), wspec(1, de2), wspec(de2 // 2, d), wspec(1, d)],
        out_specs=rows,
        scratch_shapes=[pltpu.VMEM((MOE_ROWS, d), BF16), pltpu.VMEM((d, de2), BF16),
                        pltpu.VMEM((de2 // 2, d), BF16), pltpu.VMEM((d // LANES, de2 // 2, LANES), F32)],
    )
    return pl.pallas_call(
        _moe_kernel,
        grid_spec=grid_spec,
        out_shape=jax.ShapeDtypeStruct(xs.shape, F32),
        compiler_params=pltpu.CompilerParams(dimension_semantics=("arbitrary",), vmem_limit_bytes=MOE_VMEM_LIMIT),
        name="moe_experts",
    )(blk_exp, blk_first, n_used, xs, wu, bu, wd, bd)


def _combine_kernel(h_ref, gate_ref, y0_ref, y1_ref, y2_ref, y3_ref, gf_ref, o_ref):
    tm, d = h_ref.shape
    gate = gate_ref[...]
    parts = []
    for s in range(d // LANES):
        acc = h_ref[:, s * LANES:(s + 1) * LANES]
        for kk, y_ref in enumerate((y0_ref, y1_ref, y2_ref, y3_ref)):
            acc = acc + gate[:, kk:kk + 1] * y_ref[0, pl.ds(s, tm, stride=SUBLANES), :]
        parts.append(acc)
    o_ref[...] = _rms(jnp.concatenate(parts, axis=1), gf_ref[...])


def _combine(h, gate, yg, gf, tok0):
    n, d = h.shape
    tm = _row_tile(n, 256)
    assert tok0 % tm == 0
    b0 = tok0 // tm
    tiles = d // LANES
    row = lambda w: pl.BlockSpec((tm, w), lambda i: (i, 0))
    ysp = lambda kk: pl.BlockSpec((1, tm * tiles, LANES), lambda i: (kk, b0 + i, 0))
    return pl.pallas_call(
        _combine_kernel,
        grid=(n // tm,),
        in_specs=[row(d), row(LANES)] + [ysp(kk) for kk in range(TOP_K)] + [_full(gf.shape)],
        out_specs=row(d),
        out_shape=jax.ShapeDtypeStruct((n, d), F32),
        compiler_params=_cparams(("parallel",)),
        name="combine",
    )(h, gate, yg, yg, yg, yg, gf)


def _rope_tables(pos):
    half = DIFF_DH // 2
    inv_freq = ROPE_THETA ** (-jnp.arange(half, dtype=F32) / half)
    ang = pos.astype(F32)[:, None] * inv_freq[None, :]
    cos, sin = jnp.cos(ang), jnp.sin(ang)
    reps = LANES // DIFF_DH
    return jnp.tile(jnp.concatenate([cos, cos], axis=1), (1, reps)), jnp.tile(jnp.concatenate([-sin, sin], axis=1), (1, reps))


def kernel(x_prompt, x_sample, cache_k, cache_v, state_gla, page_table, norm_attn, w_in, gla_gate_up, gla_gate_bias, gla_norm, lam_q1, lam_k1, lam_q2, lam_k2, diff_subln, w_branch_gla, w_branch_diff, w_out, norm_ffn, w_router, b_router, w_up, b_up, w_down, b_down, norm_final):
    depth = w_in.shape[0]
    assert depth == 1, "single-layer step"
    layer = 0
    b_p, t_p, d = x_prompt.shape
    b_s, t_s, _ = x_sample.shape
    n_p, n_s = b_p * t_p, b_s * t_s
    n = n_p + n_s
    n_pool, page = cache_k.shape[1], cache_k.shape[2]
    n_pages = page_table.shape[1]
    past_len = n_pages * page
    kd, vd = GLA_HEADS * GLA_DK, GLA_HEADS * GLA_DV
    tiles = d // LANES
    row2 = lambda a: a.reshape(1, -1)
    assert t_s <= DIFF_HEADS and t_s * DIFF_HEADS <= LANES

    splits = (kd, kd, vd, GLA_RANK, vd, d, d, d, d, d)
    offs = [0]
    for s in splits:
        offs.append(offs[-1] + s)
    wb = w_in.reshape(d, -1).astype(BF16)
    w_gq, w_gk, w_gv, w_glr, w_gr, w_dq, w_dk, w_dv, w_ga, w_gb = [wb[:, offs[j]:offs[j + 1]] for j in range(10)]
    w_glr = jnp.pad(w_glr, ((0, 0), (0, LANES - GLA_RANK)))
    gup = jnp.pad(gla_gate_up.reshape(GLA_RANK, kd).astype(BF16), ((0, LANES - GLA_RANK), (0, 0)))
    g_attn = row2(norm_attn)
    gn = row2(gla_norm)
    lam_init = 0.8 - 0.6 * math.exp(-0.3 * layer)
    lams = (row2(lam_q1), row2(lam_k1), row2(lam_q2), row2(lam_k2))
    subln = row2(diff_subln)
    w_bg = w_branch_gla.reshape(vd, d).astype(BF16)
    w_bd = w_branch_diff.reshape(d, d).astype(BF16)
    w_o = w_out.reshape(d, d).astype(BF16)
    w_r = jnp.pad(w_router.reshape(d, N_EXPERTS).astype(BF16), ((0, 0), (0, LANES - N_EXPERTS)))
    b_r = jnp.pad(row2(b_router), ((0, 0), (0, LANES - N_EXPERTS)))
    g_ffn = row2(norm_ffn)

    def mixers(x, pos, seq):
        gq, gk, gv, gla_la, gr = _gla_proj(x, g_attn, w_gq, w_gk, w_gv, w_glr, w_gr, gup, row2(gla_gate_bias))
        cos, sin_signed = _rope_tables(pos)
        return (gq, gk, gv, gla_la, gr) + tuple(
            _diff_proj(x, g_attn, cos, sin_signed, w_dq, w_dk, w_dv, w_ga, w_gb, seq))

    x_p = x_prompt.reshape(n_p, d)
    gq, gk, gv, gla_la, gr, dq, dk_b, dv_b, kt_p, v_cache_p, sig_a, sig_b = mixers(
        x_p, jnp.arange(t_p, dtype=jnp.int32), t_p)
    og_p, s_p = _gla_chunks(gq, gk, gv, gla_la, gr, gn, None, b_p, t_p, GLA_CHUNK if t_p % GLA_CHUNK == 0 else t_p)
    od_p = _diff_prompt(dq, dk_b, dv_b, lams, subln, b_p, t_p, lam_init)
    zero_counts = jnp.zeros((1, LANES), jnp.int32)
    h_p, hn_p, idx_p, gate_p, rank_p, counts_p = _merge_route(
        x_p, og_p, od_p, sig_a, sig_b, w_bg, w_bd, w_o, g_ffn, w_r, b_r, zero_counts)

    x_s = x_sample.reshape(n_s, d)
    gq, gk, gv, gla_la, gr, dq, dk_b, dv_b, kt_s, v_cache_s, sig_a, sig_b = mixers(
        x_s, jnp.tile(past_len + jnp.arange(t_s, dtype=jnp.int32), b_s), n_s)
    t_pad = -(-t_s // GLA_SAMPLE_CHUNK) * GLA_SAMPLE_CHUNK
    pad_s = lambda a: jnp.pad(a.reshape(b_s, t_s, -1), ((0, 0), (0, t_pad - t_s), (0, 0))).reshape(b_s * t_pad, -1)
    og_s, s_s = _gla_chunks(pad_s(gq), pad_s(gk), pad_s(gv), pad_s(gla_la), pad_s(gr), gn,
                            state_gla.reshape(b_s, GLA_HEADS, GLA_DK, GLA_DV), b_s, t_pad, t_pad)
    og_s = og_s.reshape(b_s, t_pad, vd)[:, :t_s].reshape(n_s, vd)
    q4 = dq.reshape(b_s, t_s, DIFF_QK_HEADS, DIFF_DH).transpose(0, 2, 1, 3)
    q_bd = (q4[:, :, :, None, :] * jnp.eye(DIFF_QK_HEADS, dtype=BF16)[None, :, None, :, None]).reshape(
        b_s, DIFF_QK_HEADS * t_s, d)
    kn = jnp.pad(dk_b.reshape(b_s, t_s, d).transpose(0, 2, 1), ((0, 0), (0, 0), (0, LANES - t_s)))
    vn = jnp.pad(dv_b.reshape(b_s, t_s, DIFF_HEADS, LANES).transpose(0, 2, 1, 3),
                 ((0, 0), (0, 0), (0, LANES - t_s), (0, 0)))
    cache_kt = jnp.transpose(cache_k, (0, 1, 3, 4, 2)).reshape(depth * n_pool, DIFF_QK_HEADS * DIFF_DH, page)
    od_s = _diff_sample(page_table, q_bd, kn, vn, lams, subln, cache_kt,
                        cache_v.reshape(depth * n_pool, page * DIFF_HEADS, LANES), lam_init, t_s)
    od_s = od_s.reshape(b_s, DIFF_HEADS, 2, t_s, LANES)[:, :, 0].transpose(0, 2, 1, 3).reshape(n_s, d).astype(BF16)
    h_s, hn_s, idx_s, gate_s, rank_s, counts = _merge_route(
        x_s, og_s, od_s, sig_a, sig_b, w_bg, w_bd, w_o, g_ffn, w_r, b_r, counts_p)

    counts = counts[0, :N_EXPERTS]
    padded = (counts + MOE_ROWS - 1) // MOE_ROWS * MOE_ROWS
    pad_end = jnp.cumsum(padded)
    pad_start = pad_end - padded
    idx4 = jnp.concatenate([idx_p[:, :TOP_K], idx_s[:, :TOP_K]], axis=0)
    rank4 = jnp.concatenate([rank_p[:, :TOP_K], rank_s[:, :TOP_K]], axis=0)
    dest = pad_start[idx4] + rank4
    n_rows = -(-(n * TOP_K + N_EXPERTS * (MOE_ROWS - 1)) // MOE_ROWS) * MOE_ROWS
    n_blocks = n_rows // MOE_ROWS
    blk_start = jnp.arange(n_blocks, dtype=jnp.int32) * MOE_ROWS
    blk_exp = jnp.minimum(jnp.sum((blk_start[:, None] >= pad_end[None, :]).astype(jnp.int32), axis=1), N_EXPERTS - 1)
    n_used = (pad_end[-1:] // MOE_ROWS).astype(jnp.int32)
    row_tok = (jnp.arange(n_rows, dtype=jnp.int32) % n).at[dest.reshape(-1)].set(
        jnp.repeat(jnp.arange(n, dtype=jnp.int32), TOP_K))
    hn_all = jnp.concatenate([hn_p, hn_s], axis=0).reshape(n, tiles, LANES)
    xs = jnp.take(hn_all, row_tok, axis=0, mode='clip').reshape(n_rows * tiles, LANES)

    de = w_down.shape[2]
    blk_first = jnp.concatenate([jnp.ones((1,), jnp.int32), (blk_exp[1:] != blk_exp[:-1]).astype(jnp.int32)])
    y = _moe_experts(blk_exp, blk_first, n_used, xs, w_up.reshape(N_EXPERTS, d, 2 * de),
                     b_up.reshape(N_EXPERTS, 1, 2 * de), w_down.reshape(N_EXPERTS, de, d),
                     b_down.reshape(N_EXPERTS, 1, d), d)
    yg = jnp.take(y.reshape(n_rows, tiles, LANES), dest.T.reshape(-1), axis=0, mode='clip').reshape(TOP_K, n * tiles, LANES)

    g_fin = row2(norm_final)
    y_prompt = _combine(h_p, gate_p, yg, g_fin, 0).reshape(b_p, t_p, d)
    y_sample = _combine(h_s, gate_s, yg, g_fin, n_p).reshape(b_s, t_s, d)
    k_prompt = kt_p.reshape(1, b_p, DIFF_QK_HEADS, DIFF_DH, t_p).transpose(0, 1, 4, 2, 3)
    v_prompt = v_cache_p.reshape(1, b_p, t_p, DIFF_HEADS, 2 * DIFF_DH)
    k_sample = kt_s.reshape(1, DIFF_QK_HEADS, DIFF_DH, b_s, t_s).transpose(0, 3, 4, 1, 2)
    v_sample = v_cache_s.reshape(1, b_s, t_s, DIFF_HEADS, 2 * DIFF_DH)
    return (y_prompt, y_sample, k_prompt, v_prompt, s_p[None], k_sample, v_sample, s_s[None])
```

```python
import functools
import math

import jax
import jax.numpy as jnp
from jax import lax
from jax.experimental import pallas as pl
from jax.experimental.pallas import tpu as pltpu

F32 = jnp.float32
BF16 = jnp.bfloat16

EPS = 1e-6
GLA_HEADS = 4
GLA_DK = 128
GLA_DV = 256
GLA_RANK = 16
GLA_NORMALIZER = 16.0
GLA_CHUNK = 64
GLA_SAMPLE_CHUNK = 16
GLA_PROMPT_SEQS = 4
GLA_SAMPLE_SEQS = 8
DIFF_HEADS = 8
DIFF_QK_HEADS = 2 * DIFF_HEADS
DIFF_DH = 64
ROPE_THETA = 10000.0
N_EXPERTS = 32
TOP_K = 4
SWIGLU_ALPHA = 1.702
SWIGLU_LIMIT = 7.0
LOG2E = math.log2(math.e)

LANES = 128
SUBLANES = 8
ROUTE_SUB = 256
MOE_ROWS = 512
ATTN_Q = 2048
ATTN_K = 512
ATTN_SUB = 256
ATTN_LOOKAHEAD = 3
PAGES_PER_STEP = 8
VMEM_LIMIT = 48 * 1024 * 1024
MOE_VMEM_LIMIT = 56 * 1024 * 1024


def _cparams(sem):
    return pltpu.CompilerParams(dimension_semantics=sem, vmem_limit_bytes=VMEM_LIMIT)


def _dot(a, b):
    return jnp.dot(a, b, preferred_element_type=F32)


def _dot_nt(a, b):
    return lax.dot_general(a, b, (((1,), (1,)), ((), ())), preferred_element_type=F32)


def _dot_tn(a, b):
    return lax.dot_general(a, b, (((0,), (0,)), ((), ())), preferred_element_type=F32)


def _rms(x, g):
    ms = jnp.mean(x * x, axis=-1, keepdims=True)
    return x * lax.rsqrt(ms + EPS) * g


def _sigmoid(x):
    return 1.0 / (1.0 + jnp.exp(-x))


def _row_tile(n, cap=512):
    for t in (512, 256, 128, 64, 32, 16, 8):
        if t <= cap and n % t == 0:
            return t
    raise ValueError(f"token count {n} is not a multiple of 8")


def _largest_divisor(n, cap):
    return max(t for t in range(1, cap + 1) if n % t == 0)


def _full(shape):
    return pl.BlockSpec(shape, lambda *_: (0,) * len(shape))


def _gla_proj_kernel(x_ref, g_ref, wq_ref, wk_ref, wv_ref, wlr_ref, wr_ref, gup_ref, gb_ref,
                     q_ref, k_ref, v_ref, la_ref, r_ref):
    xn = _rms(x_ref[...], g_ref[...]).astype(BF16)
    q_ref[...] = _dot(xn, wq_ref[...]) * (GLA_DK ** -0.5)
    k_ref[...] = _dot(xn, wk_ref[...])
    v_ref[...] = _dot(xn, wv_ref[...]).astype(BF16)
    glr = _dot(xn, wlr_ref[...]).astype(BF16)
    z = _dot(glr, gup_ref[...]) + gb_ref[...]
    log_sig = jnp.minimum(z, 0.0) - jnp.log1p(jnp.exp(-jnp.abs(z)))
    la_ref[...] = log_sig * (1.0 / GLA_NORMALIZER)
    r_ref[...] = _dot(xn, wr_ref[...])


def _gla_proj(x, g, wq, wk, wv, wlr, wr, gup, gb):
    n, d = x.shape
    tm = _row_tile(n)
    kd, vd = wq.shape[1], wv.shape[1]
    row = lambda w: pl.BlockSpec((tm, w), lambda i: (i, 0))
    return pl.pallas_call(
        _gla_proj_kernel,
        grid=(n // tm,),
        in_specs=[row(d), _full(g.shape), _full(wq.shape), _full(wk.shape), _full(wv.shape),
                  _full(wlr.shape), _full(wr.shape), _full(gup.shape), _full(gb.shape)],
        out_specs=[row(kd), row(kd), row(vd), row(kd), row(vd)],
        out_shape=[jax.ShapeDtypeStruct((n, kd), F32), jax.ShapeDtypeStruct((n, kd), F32),
                   jax.ShapeDtypeStruct((n, vd), BF16), jax.ShapeDtypeStruct((n, kd), F32),
                   jax.ShapeDtypeStruct((n, vd), F32)],
        compiler_params=_cparams(("parallel",)),
        name="gla_proj",
    )(x, g, wq, wk, wv, wlr, wr, gup, gb)


def _rope_chunk(xc, cos, sin_signed, first_half):
    swapped = jnp.where(first_half, pltpu.roll(xc, LANES - DIFF_DH // 2, 1), pltpu.roll(xc, DIFF_DH // 2, 1))
    return xc * cos + swapped * sin_signed


def _diff_proj_kernel(x_ref, g_ref, cos_ref, sin_ref, wq_ref, wk_ref, wv_ref, wa_ref, wb_ref,
                      q_ref, kb_ref, vb_ref, kt_ref, vc_ref, sa_ref, sb_ref):
    tm = x_ref.shape[0]
    xn = _rms(x_ref[...], g_ref[...]).astype(BF16)
    cos = cos_ref[...]
    sin_signed = sin_ref[...]
    lane = lax.broadcasted_iota(jnp.int32, (1, LANES), 1)
    first_half = (lane % DIFF_DH) < (DIFF_DH // 2)
    n_chunks = q_ref.shape[1] // LANES
    q = _dot(xn, wq_ref[...])
    for c in range(n_chunks):
        sl = slice(c * LANES, (c + 1) * LANES)
        q_ref[:, sl] = (_rope_chunk(q[:, sl], cos, sin_signed, first_half) * (DIFF_DH ** -0.5 * LOG2E)).astype(BF16)
    k = _dot(xn, wk_ref[...])
    for c in range(n_chunks):
        sl = slice(c * LANES, (c + 1) * LANES)
        kr = _rope_chunk(k[:, sl], cos, sin_signed, first_half)
        kb_ref[:, sl] = kr.astype(BF16)
        kt_ref[0, sl, :] = kr.T
    v = _dot(xn, wv_ref[...])
    vb_ref[...] = v.astype(BF16)
    for c in range(n_chunks):
        vc_ref[pl.ds(c, tm, stride=DIFF_HEADS), :] = v[:, c * LANES:(c + 1) * LANES]
    sa_ref[...] = _sigmoid(_dot(xn, wa_ref[...]))
    sb_ref[...] = _sigmoid(_dot(xn, wb_ref[...]))


def _diff_proj(x, g, cos, sin_signed, wq, wk, wv, wa, wb, seq):
    n, d = x.shape
    tm = _row_tile(seq, 256)
    tps = seq // tm
    row = lambda w: pl.BlockSpec((tm, w), lambda i: (i, 0))
    pos_row = pl.BlockSpec((tm, LANES), lambda i: (i % (cos.shape[0] // tm), 0))
    sds = lambda dt: jax.ShapeDtypeStruct((n, d), dt)
    return pl.pallas_call(
        _diff_proj_kernel,
        grid=(n // tm,),
        in_specs=[row(d), _full(g.shape), pos_row, pos_row] + [_full(wq.shape)] * 5,
        out_specs=[row(d), row(d), row(d),
                   pl.BlockSpec((1, d, tm), lambda i: (i // tps, 0, i % tps)),
                   pl.BlockSpec((tm * DIFF_HEADS, LANES), lambda i: (i, 0)),
                   row(d), row(d)],
        out_shape=[sds(BF16), sds(BF16), sds(BF16),
                   jax.ShapeDtypeStruct((n // seq, d, seq), F32),
                   jax.ShapeDtypeStruct((n * DIFF_HEADS, LANES), F32),
                   sds(F32), sds(F32)],
        compiler_params=_cparams(("parallel",)),
        name="diff_proj",
    )(x, g, cos, sin_signed, wq, wk, wv, wa, wb)


def _gla_kernel(has_s0, *refs):
    if has_s0:
        q_ref, k_ref, v_ref, la_ref, r_ref, gn_ref, s0_ref, o_ref, sfin_ref, st_ref = refs
    else:
        q_ref, k_ref, v_ref, la_ref, r_ref, gn_ref, o_ref, sfin_ref, st_ref = refs
    n = pl.program_id(1)
    nb, c, _ = q_ref.shape
    pairs = [(b, h) for b in range(nb) for h in range(GLA_HEADS)]
    slot = lambda b, h: b * GLA_HEADS + h

    @pl.when(n == 0)
    def _():
        for b, h in pairs:
            if has_s0:
                st_ref[slot(b, h)] = s0_ref[b, h]
            else:
                st_ref[slot(b, h)] = jnp.zeros(st_ref.shape[1:], F32)

    row = lax.broadcasted_iota(jnp.int32, (c, c), 0)
    col = lax.broadcasted_iota(jnp.int32, (c, c), 1)
    causal = row >= col
    tri = jnp.where(causal, 1.0, 0.0).astype(BF16)
    gn = gn_ref[...]
    ks = lambda h: slice(h * GLA_DK, (h + 1) * GLA_DK)
    vs = lambda h: slice(h * GLA_DV, (h + 1) * GLA_DV)
    ones = jnp.ones((c, GLA_DK), BF16)
    bcum, decay = [], []
    for b, h in pairs:
        la = la_ref[b, :, ks(h)]
        hi = la.astype(BF16)
        rem = la - hi.astype(F32)
        mid = rem.astype(BF16)
        lo = (rem - mid.astype(F32)).astype(BF16)
        bcum.append(_dot(tri, hi) + _dot(tri, mid) + _dot(tri, lo))
        total = _dot_tn(hi, ones) + _dot_tn(mid, ones) + _dot_tn(lo, ones)
        decay.append(jnp.exp(jnp.concatenate([total] * (GLA_DV // GLA_DK), axis=1)))
    idx = range(len(pairs))
    q_t = [(q_ref[b, :, ks(h)] * jnp.exp(bcum[i])).astype(BF16) for i, (b, h) in enumerate(pairs)]
    k_t = [(k_ref[b, :, ks(h)] * jnp.exp(-bcum[i])).astype(BF16) for i, (b, h) in enumerate(pairs)]
    k_dec = [(k_ref[b, :, ks(h)] * jnp.exp(bcum[i][c - 1:c, :] - bcum[i])).astype(BF16)
             for i, (b, h) in enumerate(pairs)]
    st = [st_ref[slot(b, h)] for b, h in pairs]
    scores = [jnp.where(causal, _dot_nt(q_t[i], k_t[i]), 0.0).astype(BF16) for i in idx]
    o_state = [_dot(q_t[i], st[i].astype(BF16)) for i in idx]
    upd = [_dot_tn(k_dec[i], v_ref[b, :, vs(h)]) for i, (b, h) in enumerate(pairs)]
    o_intra = [_dot(scores[i], v_ref[b, :, vs(h)]) for i, (b, h) in enumerate(pairs)]
    for i, (b, h) in enumerate(pairs):
        st_ref[slot(b, h)] = st[i] * decay[i] + upd[i]
        rh = r_ref[b, :, vs(h)]
        o_ref[b, :, vs(h)] = (_rms(o_state[i] + o_intra[i], gn) * (rh * _sigmoid(rh))).astype(BF16)

    @pl.when(n == pl.num_programs(1) - 1)
    def _():
        for b, h in pairs:
            sfin_ref[b, h] = st_ref[slot(b, h)]


def _gla_chunks(q, k, v, la, r, gn, s0, chunk, nb):
    batch, seq, kd = q.shape
    vd = v.shape[2]
    nc = seq // chunk
    blk = lambda w: pl.BlockSpec((nb, chunk, w), lambda b, n: (b, n, 0))
    state_spec = pl.BlockSpec((nb, GLA_HEADS, GLA_DK, GLA_DV), lambda b, n: (b, 0, 0, 0))
    in_specs = [blk(kd), blk(kd), blk(vd), blk(kd), blk(vd), _full(gn.shape)]
    args = [q, k, v, la, r, gn]
    if s0 is not None:
        in_specs.append(state_spec)
        args.append(s0)
    return pl.pallas_call(
        functools.partial(_gla_kernel, s0 is not None),
        grid=(batch // nb, nc),
        in_specs=in_specs,
        out_specs=[blk(vd), state_spec],
        out_shape=[jax.ShapeDtypeStruct((batch, seq, vd), BF16),
                   jax.ShapeDtypeStruct((batch, GLA_HEADS, GLA_DK, GLA_DV), F32)],
        scratch_shapes=[pltpu.VMEM((nb * GLA_HEADS, GLA_DK, GLA_DV), F32)],
        compiler_params=_cparams(("parallel", "arbitrary")),
        name="gla_chunks",
    )(*args)


def _lambda(lq1_ref, lk1_ref, lq2_ref, lk2_ref, lam_init):
    s1 = jnp.sum(lq1_ref[...] * lk1_ref[...], axis=-1, keepdims=True)
    s2 = jnp.sum(lq2_ref[...] * lk2_ref[...], axis=-1, keepdims=True)
    return jnp.exp(s1) - jnp.exp(s2) + lam_init


def _diff_prompt_kernel(lam_init, tk, sub, q_ref, k_ref, v_ref, lq1_ref, lk1_ref, lq2_ref, lk2_ref, sub_ref,
                        o_ref, m_ref, a_ref):
    qi = pl.program_id(2)
    tq = q_ref.shape[0]
    lane = lax.broadcasted_iota(jnp.int32, (1, LANES), 1)
    m_ref[...] = jnp.full(m_ref.shape, -jnp.inf, F32)
    a_ref[...] = jnp.zeros(a_ref.shape, F32)

    def scores(mp, r0, k0, nk, mask):
        q = q_ref[pl.ds(r0, sub), :]
        qm = jnp.where((lane < DIFF_DH) if mp == 0 else (lane >= DIFF_DH), q, jnp.zeros_like(q))
        s = _dot_nt(qm, k_ref[pl.ds(k0, nk), :])
        return s if mask is None else jnp.where(mask, s, -jnp.inf)

    def update(mp, r0, k0, nk, s):
        rows = pl.ds(r0, sub)
        n_chunks = nk // LANES
        mx = s[:, :LANES]
        for c in range(1, n_chunks):
            mx = jnp.maximum(mx, s[:, c * LANES:(c + 1) * LANES])
        m_old = m_ref[mp, rows, :]
        m_new = jnp.maximum(m_old, jnp.max(mx, axis=-1, keepdims=True))
        alpha = jnp.exp2(m_old - m_new)
        p = jnp.concatenate(
            [jnp.exp2(s[:, c * LANES:(c + 1) * LANES] - m_new).astype(BF16) for c in range(n_chunks)], axis=1)
        m_ref[mp, rows, :] = m_new
        vj = v_ref[pl.ds(k0, nk), :]
        v_ones = jnp.concatenate([vj, jnp.ones_like(vj)], axis=1)
        a_ref[mp, rows, :] = jnp.concatenate([alpha, alpha], axis=1) * a_ref[mp, rows, :] + _dot(p, v_ones)

    def run(chains):
        pending = {}
        for i in range(min(ATTN_LOOKAHEAD, len(chains))):
            pending[i] = scores(*chains[i])
        for i, (mp, r0, k0, nk, _) in enumerate(chains):
            if i + ATTN_LOOKAHEAD < len(chains):
                pending[i + ATTN_LOOKAHEAD] = scores(*chains[i + ATTN_LOOKAHEAD])
            update(mp, r0, k0, nk, pending.pop(i))

    n_sub = tq // sub

    def body(j, carry):
        k0 = pl.multiple_of(j * tk, tk)
        run([(mp, si * sub, k0, tk, None) for si in range(n_sub) for mp in range(2)])
        return carry

    lax.fori_loop(0, qi * (tq // tk), body, 0)

    base = pl.multiple_of(qi * tq, tq)
    chains = []
    for si in range(n_sub):
        need = (si + 1) * sub
        for k0 in range(0, need, tk):
            nk = min(tk, need - k0)
            mask = None
            if k0 + nk == need:
                row = lax.broadcasted_iota(jnp.int32, (sub, nk), 0) + si * sub
                col = lax.broadcasted_iota(jnp.int32, (sub, nk), 1) + k0
                mask = row >= col
            chains += [(mp, si * sub, base + k0, nk, mask) for mp in range(2)]
    run(chains)

    lam = _lambda(lq1_ref, lk1_ref, lq2_ref, lk2_ref, lam_init)
    o1 = a_ref[0, :, :LANES] * (1.0 / a_ref[0, :, LANES:])
    o2 = a_ref[1, :, :LANES] * (1.0 / a_ref[1, :, LANES:])
    o_ref[...] = (_rms(o1 - lam * o2, sub_ref[...]) * (1.0 - lam_init)).astype(BF16)


def _diff_prompt(q, k, v, lams, subln, batch, seq, lam_init):
    tq = min(ATTN_Q, seq)
    tk = min(ATTN_K, tq)
    sub = min(ATTN_SUB, tk)
    nq = seq // tq
    vec = _full(lams[0].shape)
    return pl.pallas_call(
        functools.partial(_diff_prompt_kernel, lam_init, tk, sub),
        grid=(batch, DIFF_HEADS, nq),
        in_specs=[pl.BlockSpec((tq, LANES), lambda b, h, i: (b * nq + i, h)),
                  pl.BlockSpec((seq, LANES), lambda b, h, i: (b, h)),
                  pl.BlockSpec((seq, LANES), lambda b, h, i: (b, h)),
                  vec, vec, vec, vec, _full(subln.shape)],
        out_specs=pl.BlockSpec((tq, LANES), lambda b, h, i: (b * nq + i, h)),
        out_shape=jax.ShapeDtypeStruct((batch * seq, DIFF_HEADS * LANES), BF16),
        scratch_shapes=[pltpu.VMEM((2, tq, LANES), F32), pltpu.VMEM((2, tq, 2 * LANES), F32)],
        compiler_params=_cparams(("parallel", "parallel", "arbitrary")),
        name="diff_prompt",
    )(q, k, v, *lams, subln)


def _diff_sample_kernel(lam_init, t_new, pt_ref, q_ref, kn_ref, vn_ref, lq1_ref, lk1_ref, lq2_ref, lk2_ref, sub_ref,
                        *rest):
    npg = PAGES_PER_STEP
    k_refs, v_refs = rest[:npg], rest[npg:2 * npg]
    o_ref, m_ref, l_ref, acc_ref = rest[2 * npg:]
    g = pl.program_id(1)
    rows = q_ref.shape[1]
    grp = rows // DIFF_HEADS
    page = k_refs[0].shape[2]

    @pl.when(g == 0)
    def _():
        m_ref[...] = jnp.full(m_ref.shape, -jnp.inf, F32)
        l_ref[...] = jnp.zeros(l_ref.shape, F32)
        acc_ref[...] = jnp.zeros(acc_ref.shape, F32)

    q = q_ref[0]

    def attend(s, v_of_head):
        m_old = m_ref[...]
        m_new = jnp.maximum(m_old, jnp.max(s, axis=-1, keepdims=True))
        p = jnp.exp2(s - m_new)
        alpha = jnp.exp2(m_old - m_new)
        l_ref[...] = alpha * l_ref[...] + jnp.sum(p, axis=-1, keepdims=True)
        pv = [_dot(p[h * grp:(h + 1) * grp].astype(BF16), v_of_head(h)) for h in range(DIFF_HEADS)]
        acc_ref[...] = alpha * acc_ref[...] + jnp.concatenate(pv, axis=0)
        m_ref[...] = m_new

    kcat = jnp.concatenate([kr[0].astype(BF16) for kr in k_refs], axis=1)
    attend(_dot(q, kcat), lambda h: jnp.concatenate(
        [vr[0, pl.ds(h, page, stride=DIFF_HEADS), :].astype(BF16) for vr in v_refs], axis=0))

    @pl.when(g == pl.num_programs(1) - 1)
    def _():
        s_new = _dot(q, kn_ref[0])
        r = lax.broadcasted_iota(jnp.int32, s_new.shape, 0)
        c = lax.broadcasted_iota(jnp.int32, s_new.shape, 1)
        attend(jnp.where(c <= r % t_new, s_new, -jnp.inf), lambda h: vn_ref[0, h])
        lam = _lambda(lq1_ref, lk1_ref, lq2_ref, lk2_ref, lam_init)
        an = acc_ref[...] * (1.0 / l_ref[...])
        diff = an - lam * pltpu.roll(an, rows - t_new, 0)
        o_ref[0] = _rms(diff, sub_ref[...]) * (1.0 - lam_init)


def _diff_sample(page_table, q, k_new, v_new, lams, subln, cache_kt, cache_v, lam_init, t_new):
    batch, rows, _ = q.shape
    n_pages = page_table.shape[1]
    npg = PAGES_PER_STEP
    assert n_pages % npg == 0
    vec = pl.BlockSpec(lams[0].shape, lambda b, g, pt: (0, 0))
    per_batch = lambda a: pl.BlockSpec((1,) + a.shape[1:], lambda b, g, pt: (b,) + (0,) * (a.ndim - 1))

    def page_spec(a, j):
        return pl.BlockSpec((1,) + a.shape[1:], lambda b, g, pt: (pt[b * n_pages + g * npg + j], 0, 0))

    grid_spec = pltpu.PrefetchScalarGridSpec(
        num_scalar_prefetch=1,
        grid=(batch, n_pages // npg),
        in_specs=[per_batch(q), per_batch(k_new), per_batch(v_new), vec, vec, vec, vec,
                  pl.BlockSpec(subln.shape, lambda b, g, pt: (0, 0))]
        + [page_spec(cache_kt, j) for j in range(npg)] + [page_spec(cache_v, j) for j in range(npg)],
        out_specs=pl.BlockSpec((1, rows, LANES), lambda b, g, pt: (b, 0, 0)),
        scratch_shapes=[pltpu.VMEM((rows, 1), F32), pltpu.VMEM((rows, 1), F32), pltpu.VMEM((rows, LANES), F32)],
    )
    return pl.pallas_call(
        functools.partial(_diff_sample_kernel, lam_init, t_new),
        grid_spec=grid_spec,
        out_shape=jax.ShapeDtypeStruct((batch, rows, LANES), F32),
        compiler_params=_cparams(("parallel", "arbitrary")),
        name="diff_sample",
    )(page_table.reshape(-1), q, k_new, v_new, *lams, subln, *([cache_kt] * npg), *([cache_v] * npg))


def _merge_route_kernel(x_ref, og_ref, od_ref, sa_ref, sb_ref, wg_ref, wd_ref, wo_ref, gf_ref, wr_ref, br_ref,
                        cin_ref, h_ref, hn_ref, idx_ref, gate_ref, rank_ref, cnt_ref, carry_ref):
    i = pl.program_id(0)
    tm = x_ref.shape[0]

    @pl.when(i == 0)
    def _():
        carry_ref[...] = cin_ref[...].astype(F32)

    sub = min(tm, ROUTE_SUB)
    tiles = hn_ref.shape[0] // tm
    hns = []
    for j in range(tm // sub):
        rows = pl.ds(j * sub, sub)
        merged = sa_ref[rows, :] * _dot(og_ref[rows, :], wg_ref[...]) + sb_ref[rows, :] * _dot(od_ref[rows, :], wd_ref[...])
        h = x_ref[rows, :] + _dot(merged.astype(BF16), wo_ref[...])
        h_ref[rows, :] = h
        hn = _rms(h, gf_ref[...]).astype(BF16)
        hn32 = hn.astype(F32)
        for s in range(tiles):
            hn_ref[pl.ds(j * sub * tiles + s, sub, stride=tiles), :] = hn32[:, s * LANES:(s + 1) * LANES]
        hns.append(hn)

    lane = lax.broadcasted_iota(jnp.int32, (sub, LANES), 1)
    row = lax.broadcasted_iota(jnp.int32, (sub, sub), 0)
    col = lax.broadcasted_iota(jnp.int32, (sub, sub), 1)
    strict = jnp.where(row > col, 1.0, 0.0).astype(BF16)
    n_sub = len(hns)
    logits = [jnp.where(lane < N_EXPERTS, _dot(hn, wr_ref[...]) + br_ref[...], -jnp.inf) for hn in hns]
    idx_out = [jnp.zeros((sub, LANES), jnp.int32) for _ in hns]
    val_out = [jnp.zeros((sub, LANES), F32) for _ in hns]
    onehot = [jnp.zeros((sub, LANES), F32) for _ in hns]
    picks = [[] for _ in hns]
    for kk in range(TOP_K):
        for j in range(n_sub):
            mx = jnp.max(logits[j], axis=-1, keepdims=True)
            pick = jnp.min(jnp.where(logits[j] == mx, lane, LANES), axis=-1, keepdims=True)
            hit = lane == pick
            logits[j] = jnp.where(hit, -jnp.inf, logits[j])
            onehot[j] = jnp.where(hit, 1.0, onehot[j])
            idx_out[j] = jnp.where(lane == kk, pick, idx_out[j])
            val_out[j] = jnp.where(lane == kk, mx, val_out[j])
            picks[j].append(hit)
    carry = carry_ref[...]
    for j in range(n_sub):
        rows = pl.ds(j * sub, sub)
        e = jnp.where(lane < TOP_K, jnp.exp(val_out[j] - val_out[j][:, :1]), 0.0)
        gate_ref[rows, :] = e * (1.0 / jnp.sum(e, axis=-1, keepdims=True))
        idx_ref[rows, :] = idx_out[j]
        before = _dot(strict, onehot[j].astype(BF16)) + carry
        rank_out = jnp.zeros((sub, LANES), jnp.int32)
        for kk in range(TOP_K):
            rk = jnp.sum(jnp.where(picks[j][kk], before, 0.0), axis=-1, keepdims=True)
            rank_out = jnp.where(lane == kk, rk.astype(jnp.int32), rank_out)
        rank_ref[rows, :] = rank_out
        carry = carry + jnp.sum(onehot[j], axis=0, keepdims=True)
    carry_ref[...] = carry
    cnt_ref[...] = carry.astype(jnp.int32)


def _merge_route(x, og, od, sa, sb, wg, wd, wo, gf, wr, br, counts_in):
    n, d = x.shape
    tm = _row_tile(n, 512)
    row = lambda w: pl.BlockSpec((tm, w), lambda i: (i, 0))
    return pl.pallas_call(
        _merge_route_kernel,
        grid=(n // tm,),
        in_specs=[row(d)] * 5 + [_full(wg.shape), _full(wd.shape), _full(wo.shape), _full(gf.shape),
                                 _full(wr.shape), _full(br.shape), _full(counts_in.shape)],
        out_specs=[row(d), pl.BlockSpec((tm * d // LANES, LANES), lambda i: (i, 0)),
                   row(LANES), row(LANES), row(LANES), _full((1, LANES))],
        out_shape=[jax.ShapeDtypeStruct((n, d), F32), jax.ShapeDtypeStruct((n * d // LANES, LANES), F32),
                   jax.ShapeDtypeStruct((n, LANES), jnp.int32), jax.ShapeDtypeStruct((n, LANES), F32),
                   jax.ShapeDtypeStruct((n, LANES), jnp.int32), jax.ShapeDtypeStruct((1, LANES), jnp.int32)],
        scratch_shapes=[pltpu.VMEM((1, LANES), F32)],
        compiler_params=_cparams(("arbitrary",)),
        name="merge_route",
    )(x, og, od, sa, sb, wg, wd, wo, gf, wr, br, counts_in)


def _pair_shuffle(h):
    a, b = h[:, :LANES], h[:, LANES:]
    even = (lax.broadcasted_iota(jnp.int32, (1, LANES), 1) % 2) == 0
    glu = jnp.where(even, a, pltpu.roll(b, 1, 1))
    lin = jnp.where(even, pltpu.roll(a, LANES - 1, 1), b)
    return glu, lin


def _moe_kernel(be_ref, first_ref, nu_ref, x_ref, wu_ref, bu_ref, wd_ref, bd_ref, y_ref,
                xs_ref, wub_ref, wdb_ref, wds_ref):
    i = pl.program_id(0)
    bm, d = xs_ref.shape
    de = wdb_ref.shape[0]
    n_tiles = d // LANES
    half = LANES // 2

    @pl.when((i < nu_ref[0]) & (first_ref[i] == 1))
    def _():
        wub_ref[...] = wu_ref[0].astype(BF16)
        for t in range(n_tiles):
            lanes = slice(t * LANES, (t + 1) * LANES)
            for c in range(de // LANES):
                for a in range(2):
                    src = c * LANES + a * half
                    wds_ref[t, pl.ds(c * LANES + a, half, stride=2), :] = wd_ref[0, src:src + half, lanes]
            wdb_ref[:, lanes] = wds_ref[t].astype(BF16)

    @pl.when(i < nu_ref[0])
    def _():
        for s in range(n_tiles):
            xs_ref[:, s * LANES:(s + 1) * LANES] = x_ref[pl.ds(s, bm, stride=SUBLANES), :].astype(BF16)
        h = _dot(xs_ref[...], wub_ref[...]) + bu_ref[0]
        acts = []
        for c in range(h.shape[1] // (2 * LANES)):
            glu, lin = _pair_shuffle(h[:, 2 * c * LANES:2 * (c + 1) * LANES])
            x_glu = jnp.minimum(glu, SWIGLU_LIMIT)
            x_lin = jnp.clip(lin, -SWIGLU_LIMIT, SWIGLU_LIMIT)
            acts.append((x_glu * _sigmoid(SWIGLU_ALPHA * x_glu) * (x_lin + 1.0)).astype(BF16))
        y = _dot(jnp.concatenate(acts, axis=1), wdb_ref[...]) + bd_ref[0]
        for s in range(n_tiles):
            y_ref[pl.ds(s, bm, stride=SUBLANES), :] = y[:, s * LANES:(s + 1) * LANES]

    @pl.when(i >= nu_ref[0])
    def _():
        y_ref[...] = jnp.zeros(y_ref.shape, F32)


def _moe_experts(blk_exp, blk_first, n_used, xs, wu, bu, wd, bd, d):
    tiles = d // LANES
    n_rows = xs.shape[0] // tiles
    de2 = wu.shape[2]
    wspec = lambda a, b: pl.BlockSpec((1, a, b), lambda i, be, bf, nu: (be[i], 0, 0))
    rows = pl.BlockSpec((MOE_ROWS * tiles, LANES), lambda i, be, bf, nu: (i, 0))
    grid_spec = pltpu.PrefetchScalarGridSpec(
        num_scalar_prefetch=3,
        grid=(n_rows // MOE_ROWS,),
        in_specs=[rows, wspec(d, de2), wspec(1, de2), wspec(de2 // 2, d), wspec(1, d)],
        out_specs=rows,
        scratch_shapes=[pltpu.VMEM((MOE_ROWS, d), BF16), pltpu.VMEM((d, de2), BF16),
                        pltpu.VMEM((de2 // 2, d), BF16), pltpu.VMEM((d // LANES, de2 // 2, LANES), F32)],
    )
    return pl.pallas_call(
        _moe_kernel,
        grid_spec=grid_spec,
        out_shape=jax.ShapeDtypeStruct(xs.shape, F32),
        compiler_params=pltpu.CompilerParams(dimension_semantics=("arbitrary",), vmem_limit_bytes=MOE_VMEM_LIMIT),
        name="moe_experts",
    )(blk_exp, blk_first, n_used, xs, wu, bu, wd, bd)


def _combine_kernel(h_ref, gate_ref, y0_ref, y1_ref, y2_ref, y3_ref, gf_ref, o_ref):
    tm, d = h_ref.shape
    gate = gate_ref[...]
    parts = []
    for s in range(d // LANES):
        acc = h_ref[:, s * LANES:(s + 1) * LANES]
        for kk, y_ref in enumerate((y0_ref, y1_ref, y2_ref, y3_ref)):
            acc = acc + gate[:, kk:kk + 1] * y_ref[0, pl.ds(s, tm, stride=SUBLANES), :]
        parts.append(acc)
    o_ref[...] = _rms(jnp.concatenate(parts, axis=1), gf_ref[...])


def _combine(h, gate, yg, gf, tok0):
    n, d = h.shape
    tm = _row_tile(n, 256)
    assert tok0 % tm == 0
    b0 = tok0 // tm
    tiles = d // LANES
    row = lambda w: pl.BlockSpec((tm, w), lambda i: (i, 0))
    ysp = lambda kk: pl.BlockSpec((1, tm * tiles, LANES), lambda i: (kk, b0 + i, 0))
    return pl.pallas_call(
        _combine_kernel,
        grid=(n // tm,),
        in_specs=[row(d), row(LANES)] + [ysp(kk) for kk in range(TOP_K)] + [_full(gf.shape)],
        out_specs=row(d),
        out_shape=jax.ShapeDtypeStruct((n, d), F32),
        compiler_params=_cparams(("parallel",)),
        name="combine",
    )(h, gate, yg, yg, yg, yg, gf)


def _rope_tables(pos):
    half = DIFF_DH // 2
    inv_freq = ROPE_THETA ** (-jnp.arange(half, dtype=F32) / half)
    ang = pos.astype(F32)[:, None] * inv_freq[None, :]
    cos, sin = jnp.cos(ang), jnp.sin(ang)
    reps = LANES // DIFF_DH
    return jnp.tile(jnp.concatenate([cos, cos], axis=1), (1, reps)), jnp.tile(jnp.concatenate([-sin, sin], axis=1), (1, reps))


def kernel(x_prompt, x_sample, cache_k, cache_v, state_gla, page_table, norm_attn, w_in, gla_gate_up, gla_gate_bias, gla_norm, lam_q1, lam_k1, lam_q2, lam_k2, diff_subln, w_branch_gla, w_branch_diff, w_out, norm_ffn, w_router, b_router, w_up, b_up, w_down, b_down, norm_final):
    depth = w_in.shape[0]
    assert depth == 1, "single-layer step"
    layer = 0
    b_p, t_p, d = x_prompt.shape
    b_s, t_s, _ = x_sample.shape
    n_p, n_s = b_p * t_p, b_s * t_s
    n = n_p + n_s
    n_pool, page = cache_k.shape[1], cache_k.shape[2]
    n_pages = page_table.shape[1]
    past_len = n_pages * page
    kd, vd = GLA_HEADS * GLA_DK, GLA_HEADS * GLA_DV
    tiles = d // LANES
    row2 = lambda a: a.reshape(1, -1)
    assert t_s <= DIFF_HEADS and t_s * DIFF_HEADS <= LANES

    splits = (kd, kd, vd, GLA_RANK, vd, d, d, d, d, d)
    offs = [0]
    for s in splits:
        offs.append(offs[-1] + s)
    wb = w_in.reshape(d, -1).astype(BF16)
    w_gq, w_gk, w_gv, w_glr, w_gr, w_dq, w_dk, w_dv, w_ga, w_gb = [wb[:, offs[j]:offs[j + 1]] for j in range(10)]
    w_glr = jnp.pad(w_glr, ((0, 0), (0, LANES - GLA_RANK)))
    gup = jnp.pad(gla_gate_up.reshape(GLA_RANK, kd).astype(BF16), ((0, LANES - GLA_RANK), (0, 0)))
    g_attn = row2(norm_attn)
    gn = row2(gla_norm)
    lam_init = 0.8 - 0.6 * math.exp(-0.3 * layer)
    lams = (row2(lam_q1), row2(lam_k1), row2(lam_q2), row2(lam_k2))
    subln = row2(diff_subln)
    w_bg = w_branch_gla.reshape(vd, d).astype(BF16)
    w_bd = w_branch_diff.reshape(d, d).astype(BF16)
    w_o = w_out.reshape(d, d).astype(BF16)
    w_r = jnp.pad(w_router.reshape(d, N_EXPERTS).astype(BF16), ((0, 0), (0, LANES - N_EXPERTS)))
    b_r = jnp.pad(row2(b_router), ((0, 0), (0, LANES - N_EXPERTS)))
    g_ffn = row2(norm_ffn)

    def mixers(x, pos, seq):
        gq, gk, gv, gla_la, gr = _gla_proj(x, g_attn, w_gq, w_gk, w_gv, w_glr, w_gr, gup, row2(gla_gate_bias))
        cos, sin_signed = _rope_tables(pos)
        return (gq, gk, gv, gla_la, gr) + tuple(
            _diff_proj(x, g_attn, cos, sin_signed, w_dq, w_dk, w_dv, w_ga, w_gb, seq))

    x_p = x_prompt.reshape(n_p, d)
    gq, gk, gv, gla_la, gr, dq, dk_b, dv_b, kt_p, v_cache_p, sig_a, sig_b = mixers(
        x_p, jnp.arange(t_p, dtype=jnp.int32), t_p)
    seqs = lambda a: a.reshape(b_p, t_p, -1)
    og_p, s_p = _gla_chunks(seqs(gq), seqs(gk), seqs(gv), seqs(gla_la), seqs(gr), gn, None,
                            GLA_CHUNK if t_p % GLA_CHUNK == 0 else t_p, _largest_divisor(b_p, GLA_PROMPT_SEQS))
    og_p = og_p.reshape(n_p, vd)
    od_p = _diff_prompt(dq, dk_b, dv_b, lams, subln, b_p, t_p, lam_init)
    zero_counts = jnp.zeros((1, LANES), jnp.int32)
    h_p, hn_p, idx_p, gate_p, rank_p, counts_p = _merge_route(
        x_p, og_p, od_p, sig_a, sig_b, w_bg, w_bd, w_o, g_ffn, w_r, b_r, zero_counts)

    x_s = x_sample.reshape(n_s, d)
    gq, gk, gv, gla_la, gr, dq, dk_b, dv_b, kt_s, v_cache_s, sig_a, sig_b = mixers(
        x_s, jnp.tile(past_len + jnp.arange(t_s, dtype=jnp.int32), b_s), n_s)
    t_pad = -(-t_s // GLA_SAMPLE_CHUNK) * GLA_SAMPLE_CHUNK
    pad_s = lambda a: jnp.pad(a.reshape(b_s, t_s, -1), ((0, 0), (0, t_pad - t_s), (0, 0)))
    og_s, s_s = _gla_chunks(pad_s(gq), pad_s(gk), pad_s(gv), pad_s(gla_la), pad_s(gr), gn,
                            state_gla.reshape(b_s, GLA_HEADS, GLA_DK, GLA_DV), t_pad,
                            _largest_divisor(b_s, GLA_SAMPLE_SEQS))
    og_s = og_s[:, :t_s].reshape(n_s, vd)
    q4 = dq.reshape(b_s, t_s, DIFF_QK_HEADS, DIFF_DH).transpose(0, 2, 1, 3)
    q_bd = (q4[:, :, :, None, :] * jnp.eye(DIFF_QK_HEADS, dtype=BF16)[None, :, None, :, None]).reshape(
        b_s, DIFF_QK_HEADS * t_s, d)
    kn = jnp.pad(dk_b.reshape(b_s, t_s, d).transpose(0, 2, 1), ((0, 0), (0, 0), (0, LANES - t_s)))
    vn = jnp.pad(dv_b.reshape(b_s, t_s, DIFF_HEADS, LANES).transpose(0, 2, 1, 3),
                 ((0, 0), (0, 0), (0, LANES - t_s), (0, 0)))
    cache_kt = jnp.transpose(cache_k, (0, 1, 3, 4, 2)).reshape(depth * n_pool, DIFF_QK_HEADS * DIFF_DH, page)
    od_s = _diff_sample(page_table, q_bd, kn, vn, lams, subln, cache_kt,
                        cache_v.reshape(depth * n_pool, page * DIFF_HEADS, LANES), lam_init, t_s)
    od_s = od_s.reshape(b_s, DIFF_HEADS, 2, t_s, LANES)[:, :, 0].transpose(0, 2, 1, 3).reshape(n_s, d).astype(BF16)
    h_s, hn_s, idx_s, gate_s, rank_s, counts = _merge_route(
        x_s, og_s, od_s, sig_a, sig_b, w_bg, w_bd, w_o, g_ffn, w_r, b_r, counts_p)

    counts = counts[0, :N_EXPERTS]
    padded = (counts + MOE_ROWS - 1) // MOE_ROWS * MOE_ROWS
    pad_end = jnp.cumsum(padded)
    pad_start = pad_end - padded
    idx4 = jnp.concatenate([idx_p[:, :TOP_K], idx_s[:, :TOP_K]], axis=0)
    rank4 = jnp.concatenate([rank_p[:, :TOP_K], rank_s[:, :TOP_K]], axis=0)
    dest = pad_start[idx4] + rank4
    n_rows = -(-(n * TOP_K + N_EXPERTS * (MOE_ROWS - 1)) // MOE_ROWS) * MOE_ROWS
    n_blocks = n_rows // MOE_ROWS
    blk_start = jnp.arange(n_blocks, dtype=jnp.int32) * MOE_ROWS
    blk_exp = jnp.minimum(jnp.sum((blk_start[:, None] >= pad_end[None, :]).astype(jnp.int32), axis=1), N_EXPERTS - 1)
    n_used = (pad_end[-1:] // MOE_ROWS).astype(jnp.int32)
    row_tok = (jnp.arange(n_rows, dtype=jnp.int32) % n).at[dest.reshape(-1)].set(
        jnp.repeat(jnp.arange(n, dtype=jnp.int32), TOP_K))
    hn_all = jnp.concatenate([hn_p, hn_s], axis=0).reshape(n, tiles, LANES)
    xs = jnp.take(hn_all, row_tok, axis=0, mode='clip').reshape(n_rows * tiles, LANES)

    de = w_down.shape[2]
    blk_first = jnp.concatenate([jnp.ones((1,), jnp.int32), (blk_exp[1:] != blk_exp[:-1]).astype(jnp.int32)])
    y = _moe_experts(blk_exp, blk_first, n_used, xs, w_up.reshape(N_EXPERTS, d, 2 * de),
                     b_up.reshape(N_EXPERTS, 1, 2 * de), w_down.reshape(N_EXPERTS, de, d),
                     b_down.reshape(N_EXPERTS, 1, d), d)
    yg = jnp.take(y.reshape(n_rows, tiles, LANES), dest.T.reshape(-1), axis=0, mode='clip').reshape(TOP_K, n * tiles, LANES)

    g_fin = row2(norm_final)
    y_prompt = _combine(h_p, gate_p, yg, g_fin, 0).reshape(b_p, t_p, d)
    y_sample = _combine(h_s, gate_s, yg, g_fin, n_p).reshape(b_s, t_s, d)
    k_prompt = kt_p.reshape(1, b_p, DIFF_QK_HEADS, DIFF_DH, t_p).transpose(0, 1, 4, 2, 3)
    v_prompt = v_cache_p.reshape(1, b_p, t_p, DIFF_HEADS, 2 * DIFF_DH)
    k_sample = kt_s.reshape(1, DIFF_QK_HEADS, DIFF_DH, b_s, t_s).transpose(0, 3, 4, 1, 2)
    v_sample = v_cache_s.reshape(1, b_s, t_s, DIFF_HEADS, 2 * DIFF_DH)
    return (y_prompt, y_sample, k_prompt, v_prompt, s_p[None], k_sample, v_sample, s_s[None])
```

```python
import functools
import math

import jax
import jax.numpy as jnp
from jax import lax
from jax.experimental import pallas as pl
from jax.experimental.pallas import tpu as pltpu

F32 = jnp.float32
BF16 = jnp.bfloat16

EPS = 1e-6
GLA_HEADS = 4
GLA_DK = 128
GLA_DV = 256
GLA_RANK = 16
GLA_NORMALIZER = 16.0
GLA_CHUNK = 64
GLA_SAMPLE_CHUNK = 16
GLA_PROMPT_SEQS = 4
GLA_SAMPLE_SEQS = 8
DIFF_HEADS = 8
DIFF_QK_HEADS = 2 * DIFF_HEADS
DIFF_DH = 64
ROPE_THETA = 10000.0
N_EXPERTS = 32
TOP_K = 4
SWIGLU_ALPHA = 1.702
SWIGLU_LIMIT = 7.0
LOG2E = math.log2(math.e)

LANES = 128
SUBLANES = 8
ROUTE_SUB = 256
MOE_SAMPLE_ROWS = 128
MOE_ROWS = 512
ATTN_Q = 2048
ATTN_K = 512
ATTN_SUB = 256
ATTN_LOOKAHEAD = 3
PAGES_PER_STEP = 16
VMEM_LIMIT = 48 * 1024 * 1024
MOE_VMEM_LIMIT = 56 * 1024 * 1024


def _cparams(sem):
    return pltpu.CompilerParams(dimension_semantics=sem, vmem_limit_bytes=VMEM_LIMIT)


def _dot(a, b):
    return jnp.dot(a, b, preferred_element_type=F32)


def _dot_nt(a, b):
    return lax.dot_general(a, b, (((1,), (1,)), ((), ())), preferred_element_type=F32)


def _dot_tn(a, b):
    return lax.dot_general(a, b, (((0,), (0,)), ((), ())), preferred_element_type=F32)


def _rms(x, g):
    ms = jnp.mean(x * x, axis=-1, keepdims=True)
    return x * lax.rsqrt(ms + EPS) * g


def _sigmoid(x):
    return 1.0 / (1.0 + jnp.exp(-x))


def _row_tile(n, cap=512):
    for t in (512, 256, 128, 64, 32, 16, 8):
        if t <= cap and n % t == 0:
            return t
    raise ValueError(f"token count {n} is not a multiple of 8")


def _largest_divisor(n, cap):
    return max(t for t in range(1, cap + 1) if n % t == 0)


def _full(shape):
    return pl.BlockSpec(shape, lambda *_: (0,) * len(shape))


def _gla_proj_kernel(x_ref, g_ref, wq_ref, wk_ref, wv_ref, wlr_ref, wr_ref, gup_ref, gb_ref,
                     q_ref, k_ref, v_ref, la_ref, r_ref):
    xn = _rms(x_ref[...], g_ref[...]).astype(BF16)
    q_ref[...] = _dot(xn, wq_ref[...]) * (GLA_DK ** -0.5)
    k_ref[...] = _dot(xn, wk_ref[...])
    v_ref[...] = _dot(xn, wv_ref[...]).astype(BF16)
    glr = _dot(xn, wlr_ref[...]).astype(BF16)
    z = _dot(glr, gup_ref[...]) + gb_ref[...]
    log_sig = jnp.minimum(z, 0.0) - jnp.log1p(jnp.exp(-jnp.abs(z)))
    la_ref[...] = log_sig * (1.0 / GLA_NORMALIZER)
    r_ref[...] = _dot(xn, wr_ref[...])


def _gla_proj(x, g, wq, wk, wv, wlr, wr, gup, gb):
    n, d = x.shape
    tm = _row_tile(n)
    kd, vd = wq.shape[1], wv.shape[1]
    row = lambda w: pl.BlockSpec((tm, w), lambda i: (i, 0))
    return pl.pallas_call(
        _gla_proj_kernel,
        grid=(n // tm,),
        in_specs=[row(d), _full(g.shape), _full(wq.shape), _full(wk.shape), _full(wv.shape),
                  _full(wlr.shape), _full(wr.shape), _full(gup.shape), _full(gb.shape)],
        out_specs=[row(kd), row(kd), row(vd), row(kd), row(vd)],
        out_shape=[jax.ShapeDtypeStruct((n, kd), F32), jax.ShapeDtypeStruct((n, kd), F32),
                   jax.ShapeDtypeStruct((n, vd), BF16), jax.ShapeDtypeStruct((n, kd), F32),
                   jax.ShapeDtypeStruct((n, vd), F32)],
        compiler_params=_cparams(("parallel",)),
        name="gla_proj",
    )(x, g, wq, wk, wv, wlr, wr, gup, gb)


def _rope_chunk(xc, cos, sin_signed, first_half):
    swapped = jnp.where(first_half, pltpu.roll(xc, LANES - DIFF_DH // 2, 1), pltpu.roll(xc, DIFF_DH // 2, 1))
    return xc * cos + swapped * sin_signed


def _diff_proj_kernel(x_ref, g_ref, cos_ref, sin_ref, wq_ref, wk_ref, wv_ref, wa_ref, wb_ref,
                      q_ref, kb_ref, vb_ref, kt_ref, vc_ref, sa_ref, sb_ref):
    tm = x_ref.shape[0]
    xn = _rms(x_ref[...], g_ref[...]).astype(BF16)
    cos = cos_ref[...]
    sin_signed = sin_ref[...]
    lane = lax.broadcasted_iota(jnp.int32, (1, LANES), 1)
    first_half = (lane % DIFF_DH) < (DIFF_DH // 2)
    n_chunks = q_ref.shape[1] // LANES
    q = _dot(xn, wq_ref[...])
    for c in range(n_chunks):
        sl = slice(c * LANES, (c + 1) * LANES)
        q_ref[:, sl] = (_rope_chunk(q[:, sl], cos, sin_signed, first_half) * (DIFF_DH ** -0.5 * LOG2E)).astype(BF16)
    k = _dot(xn, wk_ref[...])
    for c in range(n_chunks):
        sl = slice(c * LANES, (c + 1) * LANES)
        kr = _rope_chunk(k[:, sl], cos, sin_signed, first_half)
        kb_ref[:, sl] = kr.astype(BF16)
        kt_ref[0, sl, :] = kr.T
    v = _dot(xn, wv_ref[...])
    vb_ref[...] = v.astype(BF16)
    for c in range(n_chunks):
        vc_ref[pl.ds(c, tm, stride=DIFF_HEADS), :] = v[:, c * LANES:(c + 1) * LANES]
    sa_ref[...] = _sigmoid(_dot(xn, wa_ref[...]))
    sb_ref[...] = _sigmoid(_dot(xn, wb_ref[...]))


def _diff_proj(x, g, cos, sin_signed, wq, wk, wv, wa, wb, seq):
    n, d = x.shape
    tm = _row_tile(seq, 256)
    tps = seq // tm
    row = lambda w: pl.BlockSpec((tm, w), lambda i: (i, 0))
    pos_row = pl.BlockSpec((tm, LANES), lambda i: (i % (cos.shape[0] // tm), 0))
    sds = lambda dt: jax.ShapeDtypeStruct((n, d), dt)
    return pl.pallas_call(
        _diff_proj_kernel,
        grid=(n // tm,),
        in_specs=[row(d), _full(g.shape), pos_row, pos_row] + [_full(wq.shape)] * 5,
        out_specs=[row(d), row(d), row(d),
                   pl.BlockSpec((1, d, tm), lambda i: (i // tps, 0, i % tps)),
                   pl.BlockSpec((tm * DIFF_HEADS, LANES), lambda i: (i, 0)),
                   row(d), row(d)],
        out_shape=[sds(BF16), sds(BF16), sds(BF16),
                   jax.ShapeDtypeStruct((n // seq, d, seq), F32),
                   jax.ShapeDtypeStruct((n * DIFF_HEADS, LANES), F32),
                   sds(F32), sds(F32)],
        compiler_params=_cparams(("parallel",)),
        name="diff_proj",
    )(x, g, cos, sin_signed, wq, wk, wv, wa, wb)


def _gla_kernel(has_s0, *refs):
    if has_s0:
        q_ref, k_ref, v_ref, la_ref, r_ref, gn_ref, s0_ref, o_ref, sfin_ref, st_ref = refs
    else:
        q_ref, k_ref, v_ref, la_ref, r_ref, gn_ref, o_ref, sfin_ref, st_ref = refs
    n = pl.program_id(1)
    nb, c, _ = q_ref.shape
    pairs = [(b, h) for b in range(nb) for h in range(GLA_HEADS)]
    slot = lambda b, h: b * GLA_HEADS + h

    @pl.when(n == 0)
    def _():
        for b, h in pairs:
            if has_s0:
                st_ref[slot(b, h)] = s0_ref[b, h]
            else:
                st_ref[slot(b, h)] = jnp.zeros(st_ref.shape[1:], F32)

    row = lax.broadcasted_iota(jnp.int32, (c, c), 0)
    col = lax.broadcasted_iota(jnp.int32, (c, c), 1)
    causal = row >= col
    tri = jnp.where(causal, 1.0, 0.0).astype(BF16)
    gn = gn_ref[...]
    ks = lambda h: slice(h * GLA_DK, (h + 1) * GLA_DK)
    vs = lambda h: slice(h * GLA_DV, (h + 1) * GLA_DV)
    ones = jnp.ones((c, GLA_DK), BF16)
    bcum, decay = [], []
    for b, h in pairs:
        la = la_ref[b, :, ks(h)]
        hi = la.astype(BF16)
        rem = la - hi.astype(F32)
        mid = rem.astype(BF16)
        lo = (rem - mid.astype(F32)).astype(BF16)
        bcum.append(_dot(tri, hi) + _dot(tri, mid) + _dot(tri, lo))
        total = _dot_tn(hi, ones) + _dot_tn(mid, ones) + _dot_tn(lo, ones)
        decay.append(jnp.exp(jnp.concatenate([total] * (GLA_DV // GLA_DK), axis=1)))
    idx = range(len(pairs))
    q_t = [(q_ref[b, :, ks(h)] * jnp.exp(bcum[i])).astype(BF16) for i, (b, h) in enumerate(pairs)]
    k_t = [(k_ref[b, :, ks(h)] * jnp.exp(-bcum[i])).astype(BF16) for i, (b, h) in enumerate(pairs)]
    k_dec = [(k_ref[b, :, ks(h)] * jnp.exp(bcum[i][c - 1:c, :] - bcum[i])).astype(BF16)
             for i, (b, h) in enumerate(pairs)]
    st = [st_ref[slot(b, h)] for b, h in pairs]
    scores = [jnp.where(causal, _dot_nt(q_t[i], k_t[i]), 0.0).astype(BF16) for i in idx]
    o_state = [_dot(q_t[i], st[i].astype(BF16)) for i in idx]
    upd = [_dot_tn(k_dec[i], v_ref[b, :, vs(h)]) for i, (b, h) in enumerate(pairs)]
    o_intra = [_dot(scores[i], v_ref[b, :, vs(h)]) for i, (b, h) in enumerate(pairs)]
    for i, (b, h) in enumerate(pairs):
        st_ref[slot(b, h)] = st[i] * decay[i] + upd[i]
        rh = r_ref[b, :, vs(h)]
        o_ref[b, :, vs(h)] = (_rms(o_state[i] + o_intra[i], gn) * (rh * _sigmoid(rh))).astype(BF16)

    @pl.when(n == pl.num_programs(1) - 1)
    def _():
        for b, h in pairs:
            sfin_ref[b, h] = st_ref[slot(b, h)]


def _gla_chunks(q, k, v, la, r, gn, s0, chunk, nb):
    batch, seq, kd = q.shape
    vd = v.shape[2]
    nc = seq // chunk
    blk = lambda w: pl.BlockSpec((nb, chunk, w), lambda b, n: (b, n, 0))
    state_spec = pl.BlockSpec((nb, GLA_HEADS, GLA_DK, GLA_DV), lambda b, n: (b, 0, 0, 0))
    in_specs = [blk(kd), blk(kd), blk(vd), blk(kd), blk(vd), _full(gn.shape)]
    args = [q, k, v, la, r, gn]
    if s0 is not None:
        in_specs.append(state_spec)
        args.append(s0)
    return pl.pallas_call(
        functools.partial(_gla_kernel, s0 is not None),
        grid=(batch // nb, nc),
        in_specs=in_specs,
        out_specs=[blk(vd), state_spec],
        out_shape=[jax.ShapeDtypeStruct((batch, seq, vd), BF16),
                   jax.ShapeDtypeStruct((batch, GLA_HEADS, GLA_DK, GLA_DV), F32)],
        scratch_shapes=[pltpu.VMEM((nb * GLA_HEADS, GLA_DK, GLA_DV), F32)],
        compiler_params=_cparams(("parallel", "arbitrary")),
        name="gla_chunks",
    )(*args)


def _lambda(lq1_ref, lk1_ref, lq2_ref, lk2_ref, lam_init):
    s1 = jnp.sum(lq1_ref[...] * lk1_ref[...], axis=-1, keepdims=True)
    s2 = jnp.sum(lq2_ref[...] * lk2_ref[...], axis=-1, keepdims=True)
    return jnp.exp(s1) - jnp.exp(s2) + lam_init


def _diff_prompt_kernel(lam_init, tk, sub, q_ref, k_ref, v_ref, lq1_ref, lk1_ref, lq2_ref, lk2_ref, sub_ref,
                        o_ref, m_ref, a_ref):
    qi = pl.program_id(2)
    tq = q_ref.shape[0]
    lane = lax.broadcasted_iota(jnp.int32, (1, LANES), 1)
    m_ref[...] = jnp.full(m_ref.shape, -jnp.inf, F32)
    a_ref[...] = jnp.zeros(a_ref.shape, F32)

    def scores(mp, r0, k0, nk, mask):
        q = q_ref[pl.ds(r0, sub), :]
        qm = jnp.where((lane < DIFF_DH) if mp == 0 else (lane >= DIFF_DH), q, jnp.zeros_like(q))
        s = _dot_nt(qm, k_ref[pl.ds(k0, nk), :])
        return s if mask is None else jnp.where(mask, s, -jnp.inf)

    def update(mp, r0, k0, nk, s):
        rows = pl.ds(r0, sub)
        n_chunks = nk // LANES
        mx = s[:, :LANES]
        for c in range(1, n_chunks):
            mx = jnp.maximum(mx, s[:, c * LANES:(c + 1) * LANES])
        m_old = m_ref[mp, rows, :]
        m_new = jnp.maximum(m_old, jnp.max(mx, axis=-1, keepdims=True))
        alpha = jnp.exp2(m_old - m_new)
        p = jnp.concatenate(
            [jnp.exp2(s[:, c * LANES:(c + 1) * LANES] - m_new).astype(BF16) for c in range(n_chunks)], axis=1)
        m_ref[mp, rows, :] = m_new
        vj = v_ref[pl.ds(k0, nk), :]
        v_ones = jnp.concatenate([vj, jnp.ones_like(vj)], axis=1)
        a_ref[mp, rows, :] = jnp.concatenate([alpha, alpha], axis=1) * a_ref[mp, rows, :] + _dot(p, v_ones)

    def run(chains):
        pending = {}
        for i in range(min(ATTN_LOOKAHEAD, len(chains))):
            pending[i] = scores(*chains[i])
        for i, (mp, r0, k0, nk, _) in enumerate(chains):
            if i + ATTN_LOOKAHEAD < len(chains):
                pending[i + ATTN_LOOKAHEAD] = scores(*chains[i + ATTN_LOOKAHEAD])
            update(mp, r0, k0, nk, pending.pop(i))

    n_sub = tq // sub

    def body(j, carry):
        k0 = pl.multiple_of(j * tk, tk)
        run([(mp, si * sub, k0, tk, None) for si in range(n_sub) for mp in range(2)])
        return carry

    lax.fori_loop(0, qi * (tq // tk), body, 0)

    base = pl.multiple_of(qi * tq, tq)
    chains = []
    for si in range(n_sub):
        need = (si + 1) * sub
        for k0 in range(0, need, tk):
            nk = min(tk, need - k0)
            mask = None
            if k0 + nk == need:
                row = lax.broadcasted_iota(jnp.int32, (sub, nk), 0) + si * sub
                col = lax.broadcasted_iota(jnp.int32, (sub, nk), 1) + k0
                mask = row >= col
            chains += [(mp, si * sub, base + k0, nk, mask) for mp in range(2)]
    run(chains)

    lam = _lambda(lq1_ref, lk1_ref, lq2_ref, lk2_ref, lam_init)
    o1 = a_ref[0, :, :LANES] * (1.0 / a_ref[0, :, LANES:])
    o2 = a_ref[1, :, :LANES] * (1.0 / a_ref[1, :, LANES:])
    o_ref[...] = (_rms(o1 - lam * o2, sub_ref[...]) * (1.0 - lam_init)).astype(BF16)


def _diff_prompt(q, k, v, lams, subln, batch, seq, lam_init):
    tq = min(ATTN_Q, seq)
    tk = min(ATTN_K, tq)
    sub = min(ATTN_SUB, tk)
    nq = seq // tq
    vec = _full(lams[0].shape)
    return pl.pallas_call(
        functools.partial(_diff_prompt_kernel, lam_init, tk, sub),
        grid=(batch, DIFF_HEADS, nq),
        in_specs=[pl.BlockSpec((tq, LANES), lambda b, h, i: (b * nq + i, h)),
                  pl.BlockSpec((seq, LANES), lambda b, h, i: (b, h)),
                  pl.BlockSpec((seq, LANES), lambda b, h, i: (b, h)),
                  vec, vec, vec, vec, _full(subln.shape)],
        out_specs=pl.BlockSpec((tq, LANES), lambda b, h, i: (b * nq + i, h)),
        out_shape=jax.ShapeDtypeStruct((batch * seq, DIFF_HEADS * LANES), BF16),
        scratch_shapes=[pltpu.VMEM((2, tq, LANES), F32), pltpu.VMEM((2, tq, 2 * LANES), F32)],
        compiler_params=_cparams(("parallel", "parallel", "arbitrary")),
        name="diff_prompt",
    )(q, k, v, *lams, subln)


def _diff_sample_kernel(lam_init, t_new, npg, pt_ref, q_ref, kn_ref, vn_ref, lq1_ref, lk1_ref, lq2_ref, lk2_ref, sub_ref,
                        *rest):
    k_refs, v_refs = rest[:npg], rest[npg:2 * npg]
    o_ref, m_ref, l_ref, acc_ref = rest[2 * npg:]
    g = pl.program_id(1)
    rows = q_ref.shape[1]
    grp = rows // DIFF_HEADS
    page = k_refs[0].shape[2]

    @pl.when(g == 0)
    def _():
        m_ref[...] = jnp.full(m_ref.shape, -jnp.inf, F32)
        l_ref[...] = jnp.zeros(l_ref.shape, F32)
        acc_ref[...] = jnp.zeros(acc_ref.shape, F32)

    q = q_ref[0]

    def attend(s, v_of_head):
        m_old = m_ref[...]
        m_new = jnp.maximum(m_old, jnp.max(s, axis=-1, keepdims=True))
        p = jnp.exp2(s - m_new)
        alpha = jnp.exp2(m_old - m_new)
        l_ref[...] = alpha * l_ref[...] + jnp.sum(p, axis=-1, keepdims=True)
        pv = [_dot(p[h * grp:(h + 1) * grp].astype(BF16), v_of_head(h)) for h in range(DIFF_HEADS)]
        acc_ref[...] = alpha * acc_ref[...] + jnp.concatenate(pv, axis=0)
        m_ref[...] = m_new

    kcat = jnp.concatenate([kr[0].astype(BF16) for kr in k_refs], axis=1)
    attend(_dot(q, kcat), lambda h: jnp.concatenate(
        [vr[0, pl.ds(h, page, stride=DIFF_HEADS), :].astype(BF16) for vr in v_refs], axis=0))

    @pl.when(g == pl.num_programs(1) - 1)
    def _():
        s_new = _dot(q, kn_ref[0])
        r = lax.broadcasted_iota(jnp.int32, s_new.shape, 0)
        c = lax.broadcasted_iota(jnp.int32, s_new.shape, 1)
        attend(jnp.where(c <= r % t_new, s_new, -jnp.inf), lambda h: vn_ref[0, h])
        lam = _lambda(lq1_ref, lk1_ref, lq2_ref, lk2_ref, lam_init)
        an = acc_ref[...] * (1.0 / l_ref[...])
        diff = an - lam * pltpu.roll(an, rows - t_new, 0)
        o_ref[0] = _rms(diff, sub_ref[...]) * (1.0 - lam_init)


def _diff_sample(page_table, q, k_new, v_new, lams, subln, cache_kt, cache_v, lam_init, t_new):
    batch, rows, _ = q.shape
    n_pages = page_table.shape[1]
    npg = _largest_divisor(n_pages, PAGES_PER_STEP)
    vec = pl.BlockSpec(lams[0].shape, lambda b, g, pt: (0, 0))
    per_batch = lambda a: pl.BlockSpec((1,) + a.shape[1:], lambda b, g, pt: (b,) + (0,) * (a.ndim - 1))

    def page_spec(a, j):
        return pl.BlockSpec((1,) + a.shape[1:], lambda b, g, pt: (pt[b * n_pages + g * npg + j], 0, 0))

    grid_spec = pltpu.PrefetchScalarGridSpec(
        num_scalar_prefetch=1,
        grid=(batch, n_pages // npg),
        in_specs=[per_batch(q), per_batch(k_new), per_batch(v_new), vec, vec, vec, vec,
                  pl.BlockSpec(subln.shape, lambda b, g, pt: (0, 0))]
        + [page_spec(cache_kt, j) for j in range(npg)] + [page_spec(cache_v, j) for j in range(npg)],
        out_specs=pl.BlockSpec((1, rows, LANES), lambda b, g, pt: (b, 0, 0)),
        scratch_shapes=[pltpu.VMEM((rows, 1), F32), pltpu.VMEM((rows, 1), F32), pltpu.VMEM((rows, LANES), F32)],
    )
    return pl.pallas_call(
        functools.partial(_diff_sample_kernel, lam_init, t_new, npg),
        grid_spec=grid_spec,
        out_shape=jax.ShapeDtypeStruct((batch, rows, LANES), F32),
        compiler_params=pltpu.CompilerParams(dimension_semantics=("parallel", "arbitrary"),
                                             vmem_limit_bytes=MOE_VMEM_LIMIT),
        name="diff_sample",
    )(page_table.reshape(-1), q, k_new, v_new, *lams, subln, *([cache_kt] * npg), *([cache_v] * npg))


def _merge_route_kernel(x_ref, og_ref, od_ref, sa_ref, sb_ref, wg_ref, wd_ref, wo_ref, gf_ref, wr_ref, br_ref,
                        cin_ref, h_ref, hn_ref, idx_ref, gate_ref, rank_ref, cnt_ref, carry_ref):
    i = pl.program_id(0)
    tm = x_ref.shape[0]

    @pl.when(i == 0)
    def _():
        carry_ref[...] = cin_ref[...].astype(F32)

    sub = min(tm, ROUTE_SUB)
    tiles = hn_ref.shape[0] // tm
    hns = []
    for j in range(tm // sub):
        rows = pl.ds(j * sub, sub)
        merged = sa_ref[rows, :] * _dot(og_ref[rows, :], wg_ref[...]) + sb_ref[rows, :] * _dot(od_ref[rows, :], wd_ref[...])
        h = x_ref[rows, :] + _dot(merged.astype(BF16), wo_ref[...])
        h_ref[rows, :] = h
        hn = _rms(h, gf_ref[...]).astype(BF16)
        hn32 = hn.astype(F32)
        for s in range(tiles):
            hn_ref[pl.ds(j * sub * tiles + s, sub, stride=tiles), :] = hn32[:, s * LANES:(s + 1) * LANES]
        hns.append(hn)

    lane = lax.broadcasted_iota(jnp.int32, (sub, LANES), 1)
    row = lax.broadcasted_iota(jnp.int32, (sub, sub), 0)
    col = lax.broadcasted_iota(jnp.int32, (sub, sub), 1)
    strict = jnp.where(row > col, 1.0, 0.0).astype(BF16)
    n_sub = len(hns)
    logits = [jnp.where(lane < N_EXPERTS, _dot(hn, wr_ref[...]) + br_ref[...], -jnp.inf) for hn in hns]
    idx_out = [jnp.zeros((sub, LANES), jnp.int32) for _ in hns]
    val_out = [jnp.zeros((sub, LANES), F32) for _ in hns]
    onehot = [jnp.zeros((sub, LANES), F32) for _ in hns]
    picks = [[] for _ in hns]
    for kk in range(TOP_K):
        for j in range(n_sub):
            mx = jnp.max(logits[j], axis=-1, keepdims=True)
            pick = jnp.min(jnp.where(logits[j] == mx, lane, LANES), axis=-1, keepdims=True)
            hit = lane == pick
            logits[j] = jnp.where(hit, -jnp.inf, logits[j])
            onehot[j] = jnp.where(hit, 1.0, onehot[j])
            idx_out[j] = jnp.where(lane == kk, pick, idx_out[j])
            val_out[j] = jnp.where(lane == kk, mx, val_out[j])
            picks[j].append(hit)
    carry = carry_ref[...]
    for j in range(n_sub):
        rows = pl.ds(j * sub, sub)
        e = jnp.where(lane < TOP_K, jnp.exp(val_out[j] - val_out[j][:, :1]), 0.0)
        gate_ref[rows, :] = e * (1.0 / jnp.sum(e, axis=-1, keepdims=True))
        idx_ref[rows, :] = idx_out[j]
        before = _dot(strict, onehot[j].astype(BF16)) + carry
        rank_out = jnp.zeros((sub, LANES), jnp.int32)
        for kk in range(TOP_K):
            rk = jnp.sum(jnp.where(picks[j][kk], before, 0.0), axis=-1, keepdims=True)
            rank_out = jnp.where(lane == kk, rk.astype(jnp.int32), rank_out)
        rank_ref[rows, :] = rank_out
        carry = carry + jnp.sum(onehot[j], axis=0, keepdims=True)
    carry_ref[...] = carry
    cnt_ref[...] = carry.astype(jnp.int32)


def _merge_route(x, og, od, sa, sb, wg, wd, wo, gf, wr, br, counts_in):
    n, d = x.shape
    tm = _row_tile(n, 512)
    row = lambda w: pl.BlockSpec((tm, w), lambda i: (i, 0))
    return pl.pallas_call(
        _merge_route_kernel,
        grid=(n // tm,),
        in_specs=[row(d)] * 5 + [_full(wg.shape), _full(wd.shape), _full(wo.shape), _full(gf.shape),
                                 _full(wr.shape), _full(br.shape), _full(counts_in.shape)],
        out_specs=[row(d), pl.BlockSpec((tm * d // LANES, LANES), lambda i: (i, 0)),
                   row(LANES), row(LANES), row(LANES), _full((1, LANES))],
        out_shape=[jax.ShapeDtypeStruct((n, d), F32), jax.ShapeDtypeStruct((n * d // LANES, LANES), F32),
                   jax.ShapeDtypeStruct((n, LANES), jnp.int32), jax.ShapeDtypeStruct((n, LANES), F32),
                   jax.ShapeDtypeStruct((n, LANES), jnp.int32), jax.ShapeDtypeStruct((1, LANES), jnp.int32)],
        scratch_shapes=[pltpu.VMEM((1, LANES), F32)],
        compiler_params=_cparams(("arbitrary",)),
        name="merge_route",
    )(x, og, od, sa, sb, wg, wd, wo, gf, wr, br, counts_in)


def _pair_shuffle(h):
    a, b = h[:, :LANES], h[:, LANES:]
    even = (lax.broadcasted_iota(jnp.int32, (1, LANES), 1) % 2) == 0
    glu = jnp.where(even, a, pltpu.roll(b, 1, 1))
    lin = jnp.where(even, pltpu.roll(a, LANES - 1, 1), b)
    return glu, lin


def _moe_kernel(be_ref, first_ref, nu_ref, x_ref, wu_ref, bu_ref, wd_ref, bd_ref, y_ref,
                xs_ref, wub_ref, wdb_ref, wds_ref):
    i = pl.program_id(0)
    bm, d = xs_ref.shape
    de = wdb_ref.shape[0]
    n_tiles = d // LANES
    half = LANES // 2

    @pl.when((i < nu_ref[0]) & (first_ref[i] == 1))
    def _():
        wub_ref[...] = wu_ref[0].astype(BF16)
        for t in range(n_tiles):
            lanes = slice(t * LANES, (t + 1) * LANES)
            for c in range(de // LANES):
                for a in range(2):
                    src = c * LANES + a * half
                    wds_ref[t, pl.ds(c * LANES + a, half, stride=2), :] = wd_ref[0, src:src + half, lanes]
            wdb_ref[:, lanes] = wds_ref[t].astype(BF16)

    @pl.when(i < nu_ref[0])
    def _():
        for s in range(n_tiles):
            xs_ref[:, s * LANES:(s + 1) * LANES] = x_ref[pl.ds(s, bm, stride=SUBLANES), :].astype(BF16)
        h = _dot(xs_ref[...], wub_ref[...]) + bu_ref[0]
        acts = []
        for c in range(h.shape[1] // (2 * LANES)):
            glu, lin = _pair_shuffle(h[:, 2 * c * LANES:2 * (c + 1) * LANES])
            x_glu = jnp.minimum(glu, SWIGLU_LIMIT)
            x_lin = jnp.clip(lin, -SWIGLU_LIMIT, SWIGLU_LIMIT)
            acts.append((x_glu * _sigmoid(SWIGLU_ALPHA * x_glu) * (x_lin + 1.0)).astype(BF16))
        y = _dot(jnp.concatenate(acts, axis=1), wdb_ref[...]) + bd_ref[0]
        for s in range(n_tiles):
            y_ref[pl.ds(s, bm, stride=SUBLANES), :] = y[:, s * LANES:(s + 1) * LANES]

    @pl.when(i >= nu_ref[0])
    def _():
        y_ref[...] = jnp.zeros(y_ref.shape, F32)


def _moe_experts(blk_exp, blk_first, n_used, xs, wu, bu, wd, bd, d, block_rows):
    tiles = d // LANES
    n_rows = xs.shape[0] // tiles
    de2 = wu.shape[2]
    wspec = lambda a, b: pl.BlockSpec((1, a, b), lambda i, be, bf, nu: (be[i], 0, 0))
    rows = pl.BlockSpec((block_rows * tiles, LANES), lambda i, be, bf, nu: (i, 0))
    grid_spec = pltpu.PrefetchScalarGridSpec(
        num_scalar_prefetch=3,
        grid=(n_rows // block_rows,),
        in_specs=[rows, wspec(d, de2), wspec(1, de2), wspec(de2 // 2, d), wspec(1, d)],
        out_specs=rows,
        scratch_shapes=[pltpu.VMEM((block_rows, d), BF16), pltpu.VMEM((d, de2), BF16),
                        pltpu.VMEM((de2 // 2, d), BF16), pltpu.VMEM((d // LANES, de2 // 2, LANES), F32)],
    )
    return pl.pallas_call(
        _moe_kernel,
        grid_spec=grid_spec,
        out_shape=jax.ShapeDtypeStruct(xs.shape, F32),
        compiler_params=pltpu.CompilerParams(dimension_semantics=("arbitrary",), vmem_limit_bytes=MOE_VMEM_LIMIT),
        name="moe_experts",
    )(blk_exp, blk_first, n_used, xs, wu, bu, wd, bd)


def _combine_kernel(h_ref, gate_ref, y0_ref, y1_ref, y2_ref, y3_ref, gf_ref, o_ref):
    tm, d = h_ref.shape
    gate = gate_ref[...]
    parts = []
    for s in range(d // LANES):
        acc = h_ref[:, s * LANES:(s + 1) * LANES]
        for kk, y_ref in enumerate((y0_ref, y1_ref, y2_ref, y3_ref)):
            acc = acc + gate[:, kk:kk + 1] * y_ref[0, pl.ds(s, tm, stride=SUBLANES), :]
        parts.append(acc)
    o_ref[...] = _rms(jnp.concatenate(parts, axis=1), gf_ref[...])


def _combine(h, gate, yg, gf, tok0):
    n, d = h.shape
    tm = _row_tile(n, 256)
    assert tok0 % tm == 0
    b0 = tok0 // tm
    tiles = d // LANES
    row = lambda w: pl.BlockSpec((tm, w), lambda i: (i, 0))
    ysp = lambda kk: pl.BlockSpec((1, tm * tiles, LANES), lambda i: (kk, b0 + i, 0))
    return pl.pallas_call(
        _combine_kernel,
        grid=(n // tm,),
        in_specs=[row(d), row(LANES)] + [ysp(kk) for kk in range(TOP_K)] + [_full(gf.shape)],
        out_specs=row(d),
        out_shape=jax.ShapeDtypeStruct((n, d), F32),
        compiler_params=_cparams(("parallel",)),
        name="combine",
    )(h, gate, yg, yg, yg, yg, gf)


def _rope_tables(pos):
    half = DIFF_DH // 2
    inv_freq = ROPE_THETA ** (-jnp.arange(half, dtype=F32) / half)
    ang = pos.astype(F32)[:, None] * inv_freq[None, :]
    cos, sin = jnp.cos(ang), jnp.sin(ang)
    reps = LANES // DIFF_DH
    return jnp.tile(jnp.concatenate([cos, cos], axis=1), (1, reps)), jnp.tile(jnp.concatenate([-sin, sin], axis=1), (1, reps))


def kernel(x_prompt, x_sample, cache_k, cache_v, state_gla, page_table, norm_attn, w_in, gla_gate_up, gla_gate_bias, gla_norm, lam_q1, lam_k1, lam_q2, lam_k2, diff_subln, w_branch_gla, w_branch_diff, w_out, norm_ffn, w_router, b_router, w_up, b_up, w_down, b_down, norm_final):
    depth = w_in.shape[0]
    assert depth == 1, "single-layer step"
    layer = 0
    b_p, t_p, d = x_prompt.shape
    b_s, t_s, _ = x_sample.shape
    n_p, n_s = b_p * t_p, b_s * t_s
    n = n_p + n_s
    n_pool, page = cache_k.shape[1], cache_k.shape[2]
    n_pages = page_table.shape[1]
    past_len = n_pages * page
    kd, vd = GLA_HEADS * GLA_DK, GLA_HEADS * GLA_DV
    tiles = d // LANES
    row2 = lambda a: a.reshape(1, -1)
    assert t_s <= DIFF_HEADS and t_s * DIFF_HEADS <= LANES

    splits = (kd, kd, vd, GLA_RANK, vd, d, d, d, d, d)
    offs = [0]
    for s in splits:
        offs.append(offs[-1] + s)
    wb = w_in.reshape(d, -1).astype(BF16)
    w_gq, w_gk, w_gv, w_glr, w_gr, w_dq, w_dk, w_dv, w_ga, w_gb = [wb[:, offs[j]:offs[j + 1]] for j in range(10)]
    w_glr = jnp.pad(w_glr, ((0, 0), (0, LANES - GLA_RANK)))
    gup = jnp.pad(gla_gate_up.reshape(GLA_RANK, kd).astype(BF16), ((0, LANES - GLA_RANK), (0, 0)))
    g_attn = row2(norm_attn)
    gn = row2(gla_norm)
    lam_init = 0.8 - 0.6 * math.exp(-0.3 * layer)
    lams = (row2(lam_q1), row2(lam_k1), row2(lam_q2), row2(lam_k2))
    subln = row2(diff_subln)
    w_bg = w_branch_gla.reshape(vd, d).astype(BF16)
    w_bd = w_branch_diff.reshape(d, d).astype(BF16)
    w_o = w_out.reshape(d, d).astype(BF16)
    w_r = jnp.pad(w_router.reshape(d, N_EXPERTS).astype(BF16), ((0, 0), (0, LANES - N_EXPERTS)))
    b_r = jnp.pad(row2(b_router), ((0, 0), (0, LANES - N_EXPERTS)))
    g_ffn = row2(norm_ffn)

    def mixers(x, pos, seq):
        gq, gk, gv, gla_la, gr = _gla_proj(x, g_attn, w_gq, w_gk, w_gv, w_glr, w_gr, gup, row2(gla_gate_bias))
        cos, sin_signed = _rope_tables(pos)
        return (gq, gk, gv, gla_la, gr) + tuple(
            _diff_proj(x, g_attn, cos, sin_signed, w_dq, w_dk, w_dv, w_ga, w_gb, seq))

    de = w_down.shape[2]
    w_u, b_u = w_up.reshape(N_EXPERTS, d, 2 * de), b_up.reshape(N_EXPERTS, 1, 2 * de)
    w_d, b_d = w_down.reshape(N_EXPERTS, de, d), b_down.reshape(N_EXPERTS, 1, d)
    g_fin = row2(norm_final)

    def dispatch(hn, idx, rank, counts, block_rows):
        m = idx.shape[0]
        counts = counts[0, :N_EXPERTS]
        padded = (counts + block_rows - 1) // block_rows * block_rows
        pad_end = jnp.cumsum(padded)
        dest = (pad_end - padded)[idx[:, :TOP_K]] + rank[:, :TOP_K]
        n_rows = -(-(m * TOP_K + N_EXPERTS * (block_rows - 1)) // block_rows) * block_rows
        blk_start = jnp.arange(n_rows // block_rows, dtype=jnp.int32) * block_rows
        blk_exp = jnp.minimum(jnp.sum((blk_start[:, None] >= pad_end[None, :]).astype(jnp.int32), axis=1),
                              N_EXPERTS - 1)
        blk_first = jnp.concatenate([jnp.ones((1,), jnp.int32), (blk_exp[1:] != blk_exp[:-1]).astype(jnp.int32)])
        n_used = (pad_end[-1:] // block_rows).astype(jnp.int32)
        row_tok = (jnp.arange(n_rows, dtype=jnp.int32) % m).at[dest.reshape(-1)].set(
            jnp.repeat(jnp.arange(m, dtype=jnp.int32), TOP_K))
        xs = jnp.take(hn.reshape(m, tiles, LANES), row_tok, axis=0, mode='clip').reshape(n_rows * tiles, LANES)
        return xs, dest, blk_exp, blk_first, n_used

    def experts(xs, dest, blk_exp, blk_first, n_used, block_rows):
        m = dest.shape[0]
        y = _moe_experts(blk_exp, blk_first, n_used, xs, w_u, b_u, w_d, b_d, d, block_rows)
        return jnp.take(y.reshape(-1, tiles, LANES), dest.T.reshape(-1), axis=0, mode='clip').reshape(
            TOP_K, m * tiles, LANES)

    x_p = x_prompt.reshape(n_p, d)
    gq, gk, gv, gla_la, gr, dq, dk_b, dv_b, kt_p, v_cache_p, sig_a, sig_b = mixers(
        x_p, jnp.arange(t_p, dtype=jnp.int32), t_p)
    seqs = lambda a: a.reshape(b_p, t_p, -1)
    og_p, s_p = _gla_chunks(seqs(gq), seqs(gk), seqs(gv), seqs(gla_la), seqs(gr), gn, None,
                            GLA_CHUNK if t_p % GLA_CHUNK == 0 else t_p, _largest_divisor(b_p, GLA_PROMPT_SEQS))
    og_p = og_p.reshape(n_p, vd)
    od_p = _diff_prompt(dq, dk_b, dv_b, lams, subln, b_p, t_p, lam_init)
    zero_counts = jnp.zeros((1, LANES), jnp.int32)
    h_p, hn_p, idx_p, gate_p, rank_p, counts_p = _merge_route(
        x_p, og_p, od_p, sig_a, sig_b, w_bg, w_bd, w_o, g_ffn, w_r, b_r, zero_counts)
    disp_p = dispatch(hn_p, idx_p, rank_p, counts_p, MOE_ROWS)

    x_s = x_sample.reshape(n_s, d)
    gq, gk, gv, gla_la, gr, dq, dk_b, dv_b, kt_s, v_cache_s, sig_a, sig_b = mixers(
        x_s, jnp.tile(past_len + jnp.arange(t_s, dtype=jnp.int32), b_s), n_s)
    t_pad = -(-t_s // GLA_SAMPLE_CHUNK) * GLA_SAMPLE_CHUNK
    pad_s = lambda a: jnp.pad(a.reshape(b_s, t_s, -1), ((0, 0), (0, t_pad - t_s), (0, 0)))
    og_s, s_s = _gla_chunks(pad_s(gq), pad_s(gk), pad_s(gv), pad_s(gla_la), pad_s(gr), gn,
                            state_gla.reshape(b_s, GLA_HEADS, GLA_DK, GLA_DV), t_pad,
                            _largest_divisor(b_s, GLA_SAMPLE_SEQS))
    og_s = og_s[:, :t_s].reshape(n_s, vd)
    q4 = dq.reshape(b_s, t_s, DIFF_QK_HEADS, DIFF_DH).transpose(0, 2, 1, 3)
    q_bd = (q4[:, :, :, None, :] * jnp.eye(DIFF_QK_HEADS, dtype=BF16)[None, :, None, :, None]).reshape(
        b_s, DIFF_QK_HEADS * t_s, d)
    kn = jnp.pad(dk_b.reshape(b_s, t_s, d).transpose(0, 2, 1), ((0, 0), (0, 0), (0, LANES - t_s)))
    vn = jnp.pad(dv_b.reshape(b_s, t_s, DIFF_HEADS, LANES).transpose(0, 2, 1, 3),
                 ((0, 0), (0, 0), (0, LANES - t_s), (0, 0)))
    cache_kt = jnp.transpose(cache_k, (0, 1, 3, 4, 2)).reshape(depth * n_pool, DIFF_QK_HEADS * DIFF_DH, page)
    od_s = _diff_sample(page_table, q_bd, kn, vn, lams, subln, cache_kt,
                        cache_v.reshape(depth * n_pool, page * DIFF_HEADS, LANES), lam_init, t_s)
    od_s = od_s.reshape(b_s, DIFF_HEADS, 2, t_s, LANES)[:, :, 0].transpose(0, 2, 1, 3).reshape(n_s, d).astype(BF16)
    h_s, hn_s, idx_s, gate_s, rank_s, counts_s = _merge_route(
        x_s, og_s, od_s, sig_a, sig_b, w_bg, w_bd, w_o, g_ffn, w_r, b_r, zero_counts)

    yg_p = experts(*disp_p, MOE_ROWS)
    yg_s = experts(*dispatch(hn_s, idx_s, rank_s, counts_s, MOE_SAMPLE_ROWS), MOE_SAMPLE_ROWS)
    y_prompt = _combine(h_p, gate_p, yg_p, g_fin, 0).reshape(b_p, t_p, d)
    y_sample = _combine(h_s, gate_s, yg_s, g_fin, 0).reshape(b_s, t_s, d)
    k_prompt = kt_p.reshape(1, b_p, DIFF_QK_HEADS, DIFF_DH, t_p).transpose(0, 1, 4, 2, 3)
    v_prompt = v_cache_p.reshape(1, b_p, t_p, DIFF_HEADS, 2 * DIFF_DH)
    k_sample = kt_s.reshape(1, DIFF_QK_HEADS, DIFF_DH, b_s, t_s).transpose(0, 3, 4, 1, 2)
    v_sample = v_cache_s.reshape(1, b_s, t_s, DIFF_HEADS, 2 * DIFF_DH)
    return (y_prompt, y_sample, k_prompt, v_prompt, s_p[None], k_sample, v_sample, s_s[None])
```

```python
import functools
import math

import jax
import jax.numpy as jnp
from jax import lax
from jax.experimental import pallas as pl
from jax.experimental.pallas import tpu as pltpu

F32 = jnp.float32
BF16 = jnp.bfloat16

EPS = 1e-6
GLA_HEADS = 4
GLA_DK = 128
GLA_DV = 256
GLA_RANK = 16
GLA_NORMALIZER = 16.0
GLA_CHUNK = 64
GLA_SAMPLE_CHUNK = 16
GLA_PROMPT_SEQS = 4
GLA_SAMPLE_SEQS = 8
DIFF_HEADS = 8
DIFF_QK_HEADS = 2 * DIFF_HEADS
DIFF_DH = 64
ROPE_THETA = 10000.0
N_EXPERTS = 32
TOP_K = 4
SWIGLU_ALPHA = 1.702
SWIGLU_LIMIT = 7.0
LOG2E = math.log2(math.e)

LANES = 128
SUBLANES = 8
ROUTE_SUB = 256
MOE_ROWS = 512
ATTN_Q = 2048
ATTN_K = 512
ATTN_SUB = 256
ATTN_LOOKAHEAD = 3
PAGES_PER_STEP = 16
VMEM_LIMIT = 48 * 1024 * 1024
BIG_VMEM_LIMIT = 56 * 1024 * 1024


def _cparams(sem):
    return pltpu.CompilerParams(dimension_semantics=sem, vmem_limit_bytes=VMEM_LIMIT)


def _dot(a, b):
    return jnp.dot(a, b, preferred_element_type=F32)


def _dot_nt(a, b):
    return lax.dot_general(a, b, (((1,), (1,)), ((), ())), preferred_element_type=F32)


def _dot_tn(a, b):
    return lax.dot_general(a, b, (((0,), (0,)), ((), ())), preferred_element_type=F32)


def _rms(x, g):
    ms = jnp.mean(x * x, axis=-1, keepdims=True)
    return x * lax.rsqrt(ms + EPS) * g


def _sigmoid(x):
    return 1.0 / (1.0 + jnp.exp(-x))


def _row_tile(n, cap=512):
    for t in (512, 256, 128, 64, 32, 16, 8):
        if t <= cap and n % t == 0:
            return t
    raise ValueError(f"token count {n} is not a multiple of 8")


def _largest_divisor(n, cap):
    return max(t for t in range(1, cap + 1) if n % t == 0)


def _full(shape):
    return pl.BlockSpec(shape, lambda *_: (0,) * len(shape))


def _gla_proj_kernel(x_ref, g_ref, wq_ref, wk_ref, wv_ref, wlr_ref, wr_ref, gup_ref, gb_ref,
                     q_ref, k_ref, v_ref, la_ref, r_ref):
    xn = _rms(x_ref[...], g_ref[...]).astype(BF16)
    q_ref[...] = _dot(xn, wq_ref[...]) * (GLA_DK ** -0.5)
    k_ref[...] = _dot(xn, wk_ref[...])
    v_ref[...] = _dot(xn, wv_ref[...]).astype(BF16)
    glr = _dot(xn, wlr_ref[...]).astype(BF16)
    z = _dot(glr, gup_ref[...]) + gb_ref[...]
    log_sig = jnp.minimum(z, 0.0) - jnp.log1p(jnp.exp(-jnp.abs(z)))
    la_ref[...] = log_sig * (1.0 / GLA_NORMALIZER)
    r_ref[...] = _dot(xn, wr_ref[...])


def _gla_proj(x, g, wq, wk, wv, wlr, wr, gup, gb):
    n, d = x.shape
    tm = _row_tile(n)
    kd, vd = wq.shape[1], wv.shape[1]
    row = lambda w: pl.BlockSpec((tm, w), lambda i: (i, 0))
    return pl.pallas_call(
        _gla_proj_kernel,
        grid=(n // tm,),
        in_specs=[row(d), _full(g.shape), _full(wq.shape), _full(wk.shape), _full(wv.shape),
                  _full(wlr.shape), _full(wr.shape), _full(gup.shape), _full(gb.shape)],
        out_specs=[row(kd), row(kd), row(vd), row(kd), row(vd)],
        out_shape=[jax.ShapeDtypeStruct((n, kd), F32), jax.ShapeDtypeStruct((n, kd), F32),
                   jax.ShapeDtypeStruct((n, vd), BF16), jax.ShapeDtypeStruct((n, kd), F32),
                   jax.ShapeDtypeStruct((n, vd), F32)],
        compiler_params=_cparams(("parallel",)),
        name="gla_proj",
    )(x, g, wq, wk, wv, wlr, wr, gup, gb)


def _rope_chunk(xc, cos, sin_signed, first_half):
    swapped = jnp.where(first_half, pltpu.roll(xc, LANES - DIFF_DH // 2, 1), pltpu.roll(xc, DIFF_DH // 2, 1))
    return xc * cos + swapped * sin_signed


def _diff_proj_kernel(x_ref, g_ref, cos_ref, sin_ref, wq_ref, wk_ref, wv_ref, wa_ref, wb_ref,
                      q_ref, kb_ref, vb_ref, kt_ref, vc_ref, sa_ref, sb_ref):
    tm = x_ref.shape[0]
    xn = _rms(x_ref[...], g_ref[...]).astype(BF16)
    cos = cos_ref[...]
    sin_signed = sin_ref[...]
    lane = lax.broadcasted_iota(jnp.int32, (1, LANES), 1)
    first_half = (lane % DIFF_DH) < (DIFF_DH // 2)
    n_chunks = q_ref.shape[1] // LANES
    q = _dot(xn, wq_ref[...])
    for c in range(n_chunks):
        sl = slice(c * LANES, (c + 1) * LANES)
        q_ref[:, sl] = (_rope_chunk(q[:, sl], cos, sin_signed, first_half) * (DIFF_DH ** -0.5 * LOG2E)).astype(BF16)
    k = _dot(xn, wk_ref[...])
    for c in range(n_chunks):
        sl = slice(c * LANES, (c + 1) * LANES)
        kr = _rope_chunk(k[:, sl], cos, sin_signed, first_half)
        kb_ref[:, sl] = kr.astype(BF16)
        kt_ref[0, sl, :] = kr.T
    v = _dot(xn, wv_ref[...])
    vb_ref[...] = v.astype(BF16)
    for c in range(n_chunks):
        vc_ref[pl.ds(c, tm, stride=DIFF_HEADS), :] = v[:, c * LANES:(c + 1) * LANES]
    sa_ref[...] = _sigmoid(_dot(xn, wa_ref[...]))
    sb_ref[...] = _sigmoid(_dot(xn, wb_ref[...]))


def _diff_proj(x, g, cos, sin_signed, wq, wk, wv, wa, wb, seq):
    n, d = x.shape
    tm = _row_tile(seq, 256)
    tps = seq // tm
    row = lambda w: pl.BlockSpec((tm, w), lambda i: (i, 0))
    pos_row = pl.BlockSpec((tm, LANES), lambda i: (i % (cos.shape[0] // tm), 0))
    sds = lambda dt: jax.ShapeDtypeStruct((n, d), dt)
    return pl.pallas_call(
        _diff_proj_kernel,
        grid=(n // tm,),
        in_specs=[row(d), _full(g.shape), pos_row, pos_row] + [_full(wq.shape)] * 5,
        out_specs=[row(d), row(d), row(d),
                   pl.BlockSpec((1, d, tm), lambda i: (i // tps, 0, i % tps)),
                   pl.BlockSpec((tm * DIFF_HEADS, LANES), lambda i: (i, 0)),
                   row(d), row(d)],
        out_shape=[sds(BF16), sds(BF16), sds(BF16),
                   jax.ShapeDtypeStruct((n // seq, d, seq), F32),
                   jax.ShapeDtypeStruct((n * DIFF_HEADS, LANES), F32),
                   sds(F32), sds(F32)],
        compiler_params=_cparams(("parallel",)),
        name="diff_proj",
    )(x, g, cos, sin_signed, wq, wk, wv, wa, wb)


def _gla_kernel(has_s0, *refs):
    if has_s0:
        q_ref, k_ref, v_ref, la_ref, r_ref, gn_ref, s0_ref, o_ref, sfin_ref, st_ref = refs
    else:
        q_ref, k_ref, v_ref, la_ref, r_ref, gn_ref, o_ref, sfin_ref, st_ref = refs
    n = pl.program_id(1)
    nb, c, _ = q_ref.shape
    pairs = [(b, h) for b in range(nb) for h in range(GLA_HEADS)]
    slot = lambda b, h: b * GLA_HEADS + h

    @pl.when(n == 0)
    def _():
        for b, h in pairs:
            if has_s0:
                st_ref[slot(b, h)] = s0_ref[b, h]
            else:
                st_ref[slot(b, h)] = jnp.zeros(st_ref.shape[1:], F32)

    row = lax.broadcasted_iota(jnp.int32, (c, c), 0)
    col = lax.broadcasted_iota(jnp.int32, (c, c), 1)
    causal = row >= col
    tri = jnp.where(causal, 1.0, 0.0).astype(BF16)
    gn = gn_ref[...]
    ks = lambda h: slice(h * GLA_DK, (h + 1) * GLA_DK)
    vs = lambda h: slice(h * GLA_DV, (h + 1) * GLA_DV)
    ones = jnp.ones((c, GLA_DK), BF16)
    bcum, decay = [], []
    for b, h in pairs:
        la = la_ref[b, :, ks(h)]
        hi = la.astype(BF16)
        rem = la - hi.astype(F32)
        mid = rem.astype(BF16)
        lo = (rem - mid.astype(F32)).astype(BF16)
        bcum.append(_dot(tri, hi) + _dot(tri, mid) + _dot(tri, lo))
        total = _dot_tn(hi, ones) + _dot_tn(mid, ones) + _dot_tn(lo, ones)
        decay.append(jnp.exp(jnp.concatenate([total] * (GLA_DV // GLA_DK), axis=1)))
    idx = range(len(pairs))
    q_t = [(q_ref[b, :, ks(h)] * jnp.exp(bcum[i])).astype(BF16) for i, (b, h) in enumerate(pairs)]
    k_t = [(k_ref[b, :, ks(h)] * jnp.exp(-bcum[i])).astype(BF16) for i, (b, h) in enumerate(pairs)]
    k_dec = [(k_ref[b, :, ks(h)] * jnp.exp(bcum[i][c - 1:c, :] - bcum[i])).astype(BF16)
             for i, (b, h) in enumerate(pairs)]
    st = [st_ref[slot(b, h)] for b, h in pairs]
    scores = [jnp.where(causal, _dot_nt(q_t[i], k_t[i]), 0.0).astype(BF16) for i in idx]
    o_state = [_dot(q_t[i], st[i].astype(BF16)) for i in idx]
    upd = [_dot_tn(k_dec[i], v_ref[b, :, vs(h)]) for i, (b, h) in enumerate(pairs)]
    o_intra = [_dot(scores[i], v_ref[b, :, vs(h)]) for i, (b, h) in enumerate(pairs)]
    for i, (b, h) in enumerate(pairs):
        st_ref[slot(b, h)] = st[i] * decay[i] + upd[i]
        rh = r_ref[b, :, vs(h)]
        o_ref[b, :, vs(h)] = (_rms(o_state[i] + o_intra[i], gn) * (rh * _sigmoid(rh))).astype(BF16)

    @pl.when(n == pl.num_programs(1) - 1)
    def _():
        for b, h in pairs:
            sfin_ref[b, h] = st_ref[slot(b, h)]


def _gla_chunks(q, k, v, la, r, gn, s0, chunk, nb):
    batch, seq, kd = q.shape
    vd = v.shape[2]
    nc = seq // chunk
    blk = lambda w: pl.BlockSpec((nb, chunk, w), lambda b, n: (b, n, 0))
    state_spec = pl.BlockSpec((nb, GLA_HEADS, GLA_DK, GLA_DV), lambda b, n: (b, 0, 0, 0))
    in_specs = [blk(kd), blk(kd), blk(vd), blk(kd), blk(vd), _full(gn.shape)]
    args = [q, k, v, la, r, gn]
    if s0 is not None:
        in_specs.append(state_spec)
        args.append(s0)
    return pl.pallas_call(
        functools.partial(_gla_kernel, s0 is not None),
        grid=(batch // nb, nc),
        in_specs=in_specs,
        out_specs=[blk(vd), state_spec],
        out_shape=[jax.ShapeDtypeStruct((batch, seq, vd), BF16),
                   jax.ShapeDtypeStruct((batch, GLA_HEADS, GLA_DK, GLA_DV), F32)],
        scratch_shapes=[pltpu.VMEM((nb * GLA_HEADS, GLA_DK, GLA_DV), F32)],
        compiler_params=_cparams(("parallel", "arbitrary")),
        name="gla_chunks",
    )(*args)


def _lambda(lq1_ref, lk1_ref, lq2_ref, lk2_ref, lam_init):
    s1 = jnp.sum(lq1_ref[...] * lk1_ref[...], axis=-1, keepdims=True)
    s2 = jnp.sum(lq2_ref[...] * lk2_ref[...], axis=-1, keepdims=True)
    return jnp.exp(s1) - jnp.exp(s2) + lam_init


def _diff_prompt_kernel(lam_init, tk, sub, q_ref, k_ref, v_ref, lq1_ref, lk1_ref, lq2_ref, lk2_ref, sub_ref,
                        o_ref, m_ref, a_ref):
    qi = pl.program_id(2)
    tq = q_ref.shape[0]
    lane = lax.broadcasted_iota(jnp.int32, (1, LANES), 1)
    m_ref[...] = jnp.full(m_ref.shape, -jnp.inf, F32)
    a_ref[...] = jnp.zeros(a_ref.shape, F32)

    def scores(mp, r0, k0, nk, mask):
        q = q_ref[pl.ds(r0, sub), :]
        qm = jnp.where((lane < DIFF_DH) if mp == 0 else (lane >= DIFF_DH), q, jnp.zeros_like(q))
        s = _dot_nt(qm, k_ref[pl.ds(k0, nk), :])
        return s if mask is None else jnp.where(mask, s, -jnp.inf)

    def update(mp, r0, k0, nk, s):
        rows = pl.ds(r0, sub)
        n_chunks = nk // LANES
        mx = s[:, :LANES]
        for c in range(1, n_chunks):
            mx = jnp.maximum(mx, s[:, c * LANES:(c + 1) * LANES])
        m_old = m_ref[mp, rows, :]
        m_new = jnp.maximum(m_old, jnp.max(mx, axis=-1, keepdims=True))
        alpha = jnp.exp2(m_old - m_new)
        p = jnp.concatenate(
            [jnp.exp2(s[:, c * LANES:(c + 1) * LANES] - m_new).astype(BF16) for c in range(n_chunks)], axis=1)
        m_ref[mp, rows, :] = m_new
        vj = v_ref[pl.ds(k0, nk), :]
        v_ones = jnp.concatenate([vj, jnp.ones_like(vj)], axis=1)
        a_ref[mp, rows, :] = jnp.concatenate([alpha, alpha], axis=1) * a_ref[mp, rows, :] + _dot(p, v_ones)

    def run(chains):
        pending = {}
        for i in range(min(ATTN_LOOKAHEAD, len(chains))):
            pending[i] = scores(*chains[i])
        for i, (mp, r0, k0, nk, _) in enumerate(chains):
            if i + ATTN_LOOKAHEAD < len(chains):
                pending[i + ATTN_LOOKAHEAD] = scores(*chains[i + ATTN_LOOKAHEAD])
            update(mp, r0, k0, nk, pending.pop(i))

    n_sub = tq // sub

    def body(j, carry):
        k0 = pl.multiple_of(j * tk, tk)
        run([(mp, si * sub, k0, tk, None) for si in range(n_sub) for mp in range(2)])
        return carry

    lax.fori_loop(0, qi * (tq // tk), body, 0)

    base = pl.multiple_of(qi * tq, tq)
    chains = []
    for si in range(n_sub):
        need = (si + 1) * sub
        for k0 in range(0, need, tk):
            nk = min(tk, need - k0)
            mask = None
            if k0 + nk == need:
                row = lax.broadcasted_iota(jnp.int32, (sub, nk), 0) + si * sub
                col = lax.broadcasted_iota(jnp.int32, (sub, nk), 1) + k0
                mask = row >= col
            chains += [(mp, si * sub, base + k0, nk, mask) for mp in range(2)]
    run(chains)

    lam = _lambda(lq1_ref, lk1_ref, lq2_ref, lk2_ref, lam_init)
    o1 = a_ref[0, :, :LANES] * (1.0 / a_ref[0, :, LANES:])
    o2 = a_ref[1, :, :LANES] * (1.0 / a_ref[1, :, LANES:])
    o_ref[...] = (_rms(o1 - lam * o2, sub_ref[...]) * (1.0 - lam_init)).astype(BF16)


def _diff_prompt(q, k, v, lams, subln, batch, seq, lam_init):
    tq = min(ATTN_Q, seq)
    tk = min(ATTN_K, tq)
    sub = min(ATTN_SUB, tk)
    nq = seq // tq
    vec = _full(lams[0].shape)
    return pl.pallas_call(
        functools.partial(_diff_prompt_kernel, lam_init, tk, sub),
        grid=(batch, DIFF_HEADS, nq),
        in_specs=[pl.BlockSpec((tq, LANES), lambda b, h, i: (b * nq + i, h)),
                  pl.BlockSpec((seq, LANES), lambda b, h, i: (b, h)),
                  pl.BlockSpec((seq, LANES), lambda b, h, i: (b, h)),
                  vec, vec, vec, vec, _full(subln.shape)],
        out_specs=pl.BlockSpec((tq, LANES), lambda b, h, i: (b * nq + i, h)),
        out_shape=jax.ShapeDtypeStruct((batch * seq, DIFF_HEADS * LANES), BF16),
        scratch_shapes=[pltpu.VMEM((2, tq, LANES), F32), pltpu.VMEM((2, tq, 2 * LANES), F32)],
        compiler_params=_cparams(("parallel", "parallel", "arbitrary")),
        name="diff_prompt",
    )(q, k, v, *lams, subln)


def _diff_sample_kernel(lam_init, t_new, npg, pt_ref, q_ref, kn_ref, vn_ref, lq1_ref, lk1_ref, lq2_ref, lk2_ref,
                        sub_ref, *rest):
    k_refs, v_refs = rest[:npg], rest[npg:2 * npg]
    o_ref, m_ref, l_ref, acc_ref = rest[2 * npg:]
    g = pl.program_id(1)
    rows = q_ref.shape[1]
    grp = rows // DIFF_HEADS
    page = k_refs[0].shape[2]

    @pl.when(g == 0)
    def _():
        m_ref[...] = jnp.full(m_ref.shape, -jnp.inf, F32)
        l_ref[...] = jnp.zeros(l_ref.shape, F32)
        acc_ref[...] = jnp.zeros(acc_ref.shape, F32)

    q = q_ref[0]

    def attend(s, v_of_head):
        m_old = m_ref[...]
        m_new = jnp.maximum(m_old, jnp.max(s, axis=-1, keepdims=True))
        p = jnp.exp2(s - m_new)
        alpha = jnp.exp2(m_old - m_new)
        l_ref[...] = alpha * l_ref[...] + jnp.sum(p, axis=-1, keepdims=True)
        pv = [_dot(p[h * grp:(h + 1) * grp].astype(BF16), v_of_head(h)) for h in range(DIFF_HEADS)]
        acc_ref[...] = alpha * acc_ref[...] + jnp.concatenate(pv, axis=0)
        m_ref[...] = m_new

    kcat = jnp.concatenate([kr[0].astype(BF16) for kr in k_refs], axis=1)
    attend(_dot(q, kcat), lambda h: jnp.concatenate(
        [vr[0, pl.ds(h, page, stride=DIFF_HEADS), :].astype(BF16) for vr in v_refs], axis=0))

    @pl.when(g == pl.num_programs(1) - 1)
    def _():
        s_new = _dot(q, kn_ref[0])
        r = lax.broadcasted_iota(jnp.int32, s_new.shape, 0)
        c = lax.broadcasted_iota(jnp.int32, s_new.shape, 1)
        attend(jnp.where(c <= r % t_new, s_new, -jnp.inf), lambda h: vn_ref[0, h])
        lam = _lambda(lq1_ref, lk1_ref, lq2_ref, lk2_ref, lam_init)
        an = acc_ref[...] * (1.0 / l_ref[...])
        diff = an - lam * pltpu.roll(an, rows - t_new, 0)
        o_ref[0] = _rms(diff, sub_ref[...]) * (1.0 - lam_init)


def _diff_sample(page_table, q, k_new, v_new, lams, subln, cache_kt, cache_v, lam_init, t_new):
    batch, rows, _ = q.shape
    n_pages = page_table.shape[1]
    npg = _largest_divisor(n_pages, PAGES_PER_STEP)
    vec = pl.BlockSpec(lams[0].shape, lambda b, g, pt: (0, 0))
    per_batch = lambda a: pl.BlockSpec((1,) + a.shape[1:], lambda b, g, pt: (b,) + (0,) * (a.ndim - 1))

    def page_spec(a, j):
        return pl.BlockSpec((1,) + a.shape[1:], lambda b, g, pt: (pt[b * n_pages + g * npg + j], 0, 0))

    grid_spec = pltpu.PrefetchScalarGridSpec(
        num_scalar_prefetch=1,
        grid=(batch, n_pages // npg),
        in_specs=[per_batch(q), per_batch(k_new), per_batch(v_new), vec, vec, vec, vec,
                  pl.BlockSpec(subln.shape, lambda b, g, pt: (0, 0))]
        + [page_spec(cache_kt, j) for j in range(npg)] + [page_spec(cache_v, j) for j in range(npg)],
        out_specs=pl.BlockSpec((1, rows, LANES), lambda b, g, pt: (b, 0, 0)),
        scratch_shapes=[pltpu.VMEM((rows, 1), F32), pltpu.VMEM((rows, 1), F32), pltpu.VMEM((rows, LANES), F32)],
    )
    return pl.pallas_call(
        functools.partial(_diff_sample_kernel, lam_init, t_new, npg),
        grid_spec=grid_spec,
        out_shape=jax.ShapeDtypeStruct((batch, rows, LANES), F32),
        compiler_params=pltpu.CompilerParams(dimension_semantics=("parallel", "arbitrary"),
                                             vmem_limit_bytes=BIG_VMEM_LIMIT),
        name="diff_sample",
    )(page_table.reshape(-1), q, k_new, v_new, *lams, subln, *([cache_kt] * npg), *([cache_v] * npg))


def _merge_route_kernel(x_ref, og_ref, od_ref, sa_ref, sb_ref, wg_ref, wd_ref, wo_ref, gf_ref, wr_ref, br_ref,
                        cin_ref, *rest):
    h_ref, hn_ref, idx_ref, gate_ref, rank_ref, cnt_ref, carry_ref = rest[-7:]
    i = pl.program_id(0)
    tm = x_ref.shape[0]

    @pl.when(i == 0)
    def _():
        carry_ref[...] = cin_ref[...].astype(F32)

    sub = min(tm, ROUTE_SUB)
    tiles = hn_ref.shape[0] // tm
    hns = []
    for j in range(tm // sub):
        rows = pl.ds(j * sub, sub)
        merged = (sa_ref[rows, :] * _dot(og_ref[rows, :], wg_ref[...])
                  + sb_ref[rows, :] * _dot(od_ref[rows, :], wd_ref[...]))
        h = x_ref[rows, :] + _dot(merged.astype(BF16), wo_ref[...])
        h_ref[rows, :] = h
        hn = _rms(h, gf_ref[...]).astype(BF16)
        hn32 = hn.astype(F32)
        for s in range(tiles):
            hn_ref[pl.ds(j * sub * tiles + s, sub, stride=tiles), :] = hn32[:, s * LANES:(s + 1) * LANES]
        hns.append(hn)

    lane = lax.broadcasted_iota(jnp.int32, (sub, LANES), 1)
    row = lax.broadcasted_iota(jnp.int32, (sub, sub), 0)
    col = lax.broadcasted_iota(jnp.int32, (sub, sub), 1)
    strict = jnp.where(row > col, 1.0, 0.0).astype(BF16)
    n_sub = len(hns)
    logits = [jnp.where(lane < N_EXPERTS, _dot(hn, wr_ref[...]) + br_ref[...], -jnp.inf) for hn in hns]
    idx_out = [jnp.zeros((sub, LANES), jnp.int32) for _ in hns]
    val_out = [jnp.zeros((sub, LANES), F32) for _ in hns]
    onehot = [jnp.zeros((sub, LANES), F32) for _ in hns]
    picks = [[] for _ in hns]
    for kk in range(TOP_K):
        for j in range(n_sub):
            mx = jnp.max(logits[j], axis=-1, keepdims=True)
            pick = jnp.min(jnp.where(logits[j] == mx, lane, LANES), axis=-1, keepdims=True)
            hit = lane == pick
            logits[j] = jnp.where(hit, -jnp.inf, logits[j])
            onehot[j] = jnp.where(hit, 1.0, onehot[j])
            idx_out[j] = jnp.where(lane == kk, pick, idx_out[j])
            val_out[j] = jnp.where(lane == kk, mx, val_out[j])
            picks[j].append(hit)
    carry = carry_ref[...]
    for j in range(n_sub):
        rows = pl.ds(j * sub, sub)
        e = jnp.where(lane < TOP_K, jnp.exp(val_out[j] - val_out[j][:, :1]), 0.0)
        gate_ref[rows, :] = e * (1.0 / jnp.sum(e, axis=-1, keepdims=True))
        idx_ref[rows, :] = idx_out[j]
        before = _dot(strict, onehot[j].astype(BF16)) + carry
        rank_out = jnp.zeros((sub, LANES), jnp.int32)
        for kk in range(TOP_K):
            rk = jnp.sum(jnp.where(picks[j][kk], before, 0.0), axis=-1, keepdims=True)
            rank_out = jnp.where(lane == kk, rk.astype(jnp.int32), rank_out)
        rank_ref[rows, :] = rank_out
        carry = carry + jnp.sum(onehot[j], axis=0, keepdims=True)
    carry_ref[...] = carry
    cnt_ref[...] = carry.astype(jnp.int32)


def _merge_route(x, og, od, sa, sb, wg, wd, wo, gf, wr, br, counts_in, n_all, tok0, hn_all=None):
    n, d = x.shape
    tm = _row_tile(n, 512)
    assert tok0 % tm == 0
    b0 = tok0 // tm
    tiles = d // LANES
    row = lambda w: pl.BlockSpec((tm, w), lambda i: (i, 0))
    in_specs = [row(d)] * 5 + [_full(wg.shape), _full(wd.shape), _full(wo.shape), _full(gf.shape),
                               _full(wr.shape), _full(br.shape), _full(counts_in.shape)]
    args = [x, og, od, sa, sb, wg, wd, wo, gf, wr, br, counts_in]
    aliases = {}
    if hn_all is not None:
        in_specs.append(pl.BlockSpec(memory_space=pl.ANY))
        aliases = {len(args): 1}
        args.append(hn_all)
    return pl.pallas_call(
        _merge_route_kernel,
        grid=(n // tm,),
        in_specs=in_specs,
        out_specs=[row(d), pl.BlockSpec((tm * tiles, LANES), lambda i: (b0 + i, 0)),
                   row(LANES), row(LANES), row(LANES), _full((1, LANES))],
        out_shape=[jax.ShapeDtypeStruct((n, d), F32), jax.ShapeDtypeStruct((n_all * tiles, LANES), F32),
                   jax.ShapeDtypeStruct((n, LANES), jnp.int32), jax.ShapeDtypeStruct((n, LANES), F32),
                   jax.ShapeDtypeStruct((n, LANES), jnp.int32), jax.ShapeDtypeStruct((1, LANES), jnp.int32)],
        scratch_shapes=[pltpu.VMEM((1, LANES), F32)],
        input_output_aliases=aliases,
        compiler_params=_cparams(("arbitrary",)),
        name="merge_route",
    )(*args)


def _pair_shuffle(h):
    a, b = h[:, :LANES], h[:, LANES:]
    even = (lax.broadcasted_iota(jnp.int32, (1, LANES), 1) % 2) == 0
    glu = jnp.where(even, a, pltpu.roll(b, 1, 1))
    lin = jnp.where(even, pltpu.roll(a, LANES - 1, 1), b)
    return glu, lin


def _moe_kernel(be_ref, first_ref, nu_ref, x_ref, wu_ref, bu_ref, wd_ref, bd_ref, y_ref,
                xs_ref, wub_ref, wdb_ref, wds_ref):
    i = pl.program_id(0)
    bm, d = xs_ref.shape
    de = wdb_ref.shape[0]
    n_tiles = d // LANES
    half = LANES // 2

    @pl.when((i < nu_ref[0]) & (first_ref[i] == 1))
    def _():
        wub_ref[...] = wu_ref[0].astype(BF16)
        for t in range(n_tiles):
            lanes = slice(t * LANES, (t + 1) * LANES)
            for c in range(de // LANES):
                for a in range(2):
                    src = c * LANES + a * half
                    wds_ref[t, pl.ds(c * LANES + a, half, stride=2), :] = wd_ref[0, src:src + half, lanes]
            wdb_ref[:, lanes] = wds_ref[t].astype(BF16)

    @pl.when(i < nu_ref[0])
    def _():
        for s in range(n_tiles):
            xs_ref[:, s * LANES:(s + 1) * LANES] = x_ref[pl.ds(s, bm, stride=SUBLANES), :].astype(BF16)
        h = _dot(xs_ref[...], wub_ref[...]) + bu_ref[0]
        acts = []
        for c in range(h.shape[1] // (2 * LANES)):
            glu, lin = _pair_shuffle(h[:, 2 * c * LANES:2 * (c + 1) * LANES])
            x_glu = jnp.minimum(glu, SWIGLU_LIMIT)
            x_lin = jnp.clip(lin, -SWIGLU_LIMIT, SWIGLU_LIMIT)
            acts.append((x_glu * _sigmoid(SWIGLU_ALPHA * x_glu) * (x_lin + 1.0)).astype(BF16))
        y = _dot(jnp.concatenate(acts, axis=1), wdb_ref[...]) + bd_ref[0]
        for s in range(n_tiles):
            y_ref[pl.ds(s, bm, stride=SUBLANES), :] = y[:, s * LANES:(s + 1) * LANES]

    @pl.when(i >= nu_ref[0])
    def _():
        y_ref[...] = jnp.zeros(y_ref.shape, F32)


def _moe_experts(blk_exp, blk_first, n_used, xs, wu, bu, wd, bd, d):
    tiles = d // LANES
    n_rows = xs.shape[0] // tiles
    de2 = wu.shape[2]
    wspec = lambda a, b: pl.BlockSpec((1, a, b), lambda i, be, bf, nu: (be[i], 0, 0))
    rows = pl.BlockSpec((MOE_ROWS * tiles, LANES), lambda i, be, bf, nu: (i, 0))
    grid_spec = pltpu.PrefetchScalarGridSpec(
        num_scalar_prefetch=3,
        grid=(n_rows // MOE_ROWS,),
        in_specs=[rows, wspec(d, de2), wspec(1, de2), wspec(de2 // 2, d), wspec(1, d)],
        out_specs=rows,
        scratch_shapes=[pltpu.VMEM((MOE_ROWS, d), BF16), pltpu.VMEM((d, de2), BF16),
                        pltpu.VMEM((de2 // 2, d), BF16), pltpu.VMEM((d // LANES, de2 // 2, LANES), F32)],
    )
    return pl.pallas_call(
        _moe_kernel,
        grid_spec=grid_spec,
        out_shape=jax.ShapeDtypeStruct(xs.shape, F32),
        compiler_params=pltpu.CompilerParams(dimension_semantics=("arbitrary",), vmem_limit_bytes=BIG_VMEM_LIMIT),
        name="moe_experts",
    )(blk_exp, blk_first, n_used, xs, wu, bu, wd, bd)


def _combine_kernel(h_ref, gate_ref, y0_ref, y1_ref, y2_ref, y3_ref, gf_ref, o_ref):
    tm, d = h_ref.shape
    gate = gate_ref[...]
    parts = []
    for s in range(d // LANES):
        acc = h_ref[:, s * LANES:(s + 1) * LANES]
        for kk, y_ref in enumerate((y0_ref, y1_ref, y2_ref, y3_ref)):
            acc = acc + gate[:, kk:kk + 1] * y_ref[0, pl.ds(s, tm, stride=SUBLANES), :]
        parts.append(acc)
    o_ref[...] = _rms(jnp.concatenate(parts, axis=1), gf_ref[...])


def _combine(h, gate, yg, gf, tok0):
    n, d = h.shape
    tm = _row_tile(n, 256)
    assert tok0 % tm == 0
    b0 = tok0 // tm
    tiles = d // LANES
    row = lambda w: pl.BlockSpec((tm, w), lambda i: (i, 0))
    ysp = lambda kk: pl.BlockSpec((1, tm * tiles, LANES), lambda i: (kk, b0 + i, 0))
    return pl.pallas_call(
        _combine_kernel,
        grid=(n // tm,),
        in_specs=[row(d), row(LANES)] + [ysp(kk) for kk in range(TOP_K)] + [_full(gf.shape)],
        out_specs=row(d),
        out_shape=jax.ShapeDtypeStruct((n, d), F32),
        compiler_params=_cparams(("parallel",)),
        name="combine",
    )(h, gate, yg, yg, yg, yg, gf)


def _rope_tables(pos):
    half = DIFF_DH // 2
    inv_freq = ROPE_THETA ** (-jnp.arange(half, dtype=F32) / half)
    ang = pos.astype(F32)[:, None] * inv_freq[None, :]
    cos, sin = jnp.cos(ang), jnp.sin(ang)
    reps = LANES // DIFF_DH
    return jnp.tile(jnp.concatenate([cos, cos], axis=1), (1, reps)), jnp.tile(jnp.concatenate([-sin, sin], axis=1), (1, reps))


def kernel(x_prompt, x_sample, cache_k, cache_v, state_gla, page_table, norm_attn, w_in, gla_gate_up, gla_gate_bias, gla_norm, lam_q1, lam_k1, lam_q2, lam_k2, diff_subln, w_branch_gla, w_branch_diff, w_out, norm_ffn, w_router, b_router, w_up, b_up, w_down, b_down, norm_final):
    depth = w_in.shape[0]
    assert depth == 1, "single-layer step"
    layer = 0
    b_p, t_p, d = x_prompt.shape
    b_s, t_s, _ = x_sample.shape
    n_p, n_s = b_p * t_p, b_s * t_s
    n = n_p + n_s
    n_pool, page = cache_k.shape[1], cache_k.shape[2]
    n_pages = page_table.shape[1]
    past_len = n_pages * page
    kd, vd = GLA_HEADS * GLA_DK, GLA_HEADS * GLA_DV
    tiles = d // LANES
    row2 = lambda a: a.reshape(1, -1)
    assert t_s <= LANES

    splits = (kd, kd, vd, GLA_RANK, vd, d, d, d, d, d)
    offs = [0]
    for s in splits:
        offs.append(offs[-1] + s)
    wb = w_in.reshape(d, -1).astype(BF16)
    w_gq, w_gk, w_gv, w_glr, w_gr, w_dq, w_dk, w_dv, w_ga, w_gb = [wb[:, offs[j]:offs[j + 1]] for j in range(10)]
    w_glr = jnp.pad(w_glr, ((0, 0), (0, LANES - GLA_RANK)))
    gup = jnp.pad(gla_gate_up.reshape(GLA_RANK, kd).astype(BF16), ((0, LANES - GLA_RANK), (0, 0)))
    g_attn = row2(norm_attn)
    gn = row2(gla_norm)
    lam_init = 0.8 - 0.6 * math.exp(-0.3 * layer)
    lams = (row2(lam_q1), row2(lam_k1), row2(lam_q2), row2(lam_k2))
    subln = row2(diff_subln)
    w_bg = w_branch_gla.reshape(vd, d).astype(BF16)
    w_bd = w_branch_diff.reshape(d, d).astype(BF16)
    w_o = w_out.reshape(d, d).astype(BF16)
    w_r = jnp.pad(w_router.reshape(d, N_EXPERTS).astype(BF16), ((0, 0), (0, LANES - N_EXPERTS)))
    b_r = jnp.pad(row2(b_router), ((0, 0), (0, LANES - N_EXPERTS)))
    g_ffn = row2(norm_ffn)

    def mixers(x, pos, seq):
        gq, gk, gv, gla_la, gr = _gla_proj(x, g_attn, w_gq, w_gk, w_gv, w_glr, w_gr, gup, row2(gla_gate_bias))
        cos, sin_signed = _rope_tables(pos)
        return (gq, gk, gv, gla_la, gr) + tuple(
            _diff_proj(x, g_attn, cos, sin_signed, w_dq, w_dk, w_dv, w_ga, w_gb, seq))

    x_p = x_prompt.reshape(n_p, d)
    gq, gk, gv, gla_la, gr, dq, dk_b, dv_b, kt_p, v_cache_p, sig_a, sig_b = mixers(
        x_p, jnp.arange(t_p, dtype=jnp.int32), t_p)
    seqs = lambda a: a.reshape(b_p, t_p, -1)
    og_p, s_p = _gla_chunks(seqs(gq), seqs(gk), seqs(gv), seqs(gla_la), seqs(gr), gn, None,
                            GLA_CHUNK if t_p % GLA_CHUNK == 0 else t_p, _largest_divisor(b_p, GLA_PROMPT_SEQS))
    og_p = og_p.reshape(n_p, vd)
    od_p = _diff_prompt(dq, dk_b, dv_b, lams, subln, b_p, t_p, lam_init)
    zero_counts = jnp.zeros((1, LANES), jnp.int32)
    h_p, hn_p, idx_p, gate_p, rank_p, counts_p = _merge_route(
        x_p, og_p, od_p, sig_a, sig_b, w_bg, w_bd, w_o, g_ffn, w_r, b_r, zero_counts, n, 0)

    x_s = x_sample.reshape(n_s, d)
    gq, gk, gv, gla_la, gr, dq, dk_b, dv_b, kt_s, v_cache_s, sig_a, sig_b = mixers(
        x_s, jnp.tile(past_len + jnp.arange(t_s, dtype=jnp.int32), b_s), n_s)
    t_pad = -(-t_s // GLA_SAMPLE_CHUNK) * GLA_SAMPLE_CHUNK
    pad_s = lambda a: jnp.pad(a.reshape(b_s, t_s, -1), ((0, 0), (0, t_pad - t_s), (0, 0)))
    og_s, s_s = _gla_chunks(pad_s(gq), pad_s(gk), pad_s(gv), pad_s(gla_la), pad_s(gr), gn,
                            state_gla.reshape(b_s, GLA_HEADS, GLA_DK, GLA_DV), t_pad,
                            _largest_divisor(b_s, GLA_SAMPLE_SEQS))
    og_s = og_s[:, :t_s].reshape(n_s, vd)
    q4 = dq.reshape(b_s, t_s, DIFF_QK_HEADS, DIFF_DH).transpose(0, 2, 1, 3)
    q_bd = (q4[:, :, :, None, :] * jnp.eye(DIFF_QK_HEADS, dtype=BF16)[None, :, None, :, None]).reshape(
        b_s, DIFF_QK_HEADS * t_s, d)
    kn = jnp.pad(dk_b.reshape(b_s, t_s, d).transpose(0, 2, 1), ((0, 0), (0, 0), (0, LANES - t_s)))
    vn = jnp.pad(dv_b.reshape(b_s, t_s, DIFF_HEADS, LANES).transpose(0, 2, 1, 3),
                 ((0, 0), (0, 0), (0, LANES - t_s), (0, 0)))
    cache_kt = jnp.transpose(cache_k, (0, 1, 3, 4, 2)).reshape(depth * n_pool, DIFF_QK_HEADS * DIFF_DH, page)
    od_s = _diff_sample(page_table, q_bd, kn, vn, lams, subln, cache_kt,
                        cache_v.reshape(depth * n_pool, page * DIFF_HEADS, LANES), lam_init, t_s)
    od_s = od_s.reshape(b_s, DIFF_HEADS, 2, t_s, LANES)[:, :, 0].transpose(0, 2, 1, 3).reshape(n_s, d).astype(BF16)
    h_s, hn_all, idx_s, gate_s, rank_s, counts = _merge_route(
        x_s, og_s, od_s, sig_a, sig_b, w_bg, w_bd, w_o, g_ffn, w_r, b_r, counts_p, n, n_p, hn_p)

    counts = counts[0, :N_EXPERTS]
    padded = (counts + MOE_ROWS - 1) // MOE_ROWS * MOE_ROWS
    pad_end = jnp.cumsum(padded)
    pad_start = pad_end - padded
    idx4 = jnp.concatenate([idx_p[:, :TOP_K], idx_s[:, :TOP_K]], axis=0)
    rank4 = jnp.concatenate([rank_p[:, :TOP_K], rank_s[:, :TOP_K]], axis=0)
    dest = pad_start[idx4] + rank4
    n_rows = -(-(n * TOP_K + N_EXPERTS * (MOE_ROWS - 1)) // MOE_ROWS) * MOE_ROWS
    n_blocks = n_rows // MOE_ROWS
    blk_start = jnp.arange(n_blocks, dtype=jnp.int32) * MOE_ROWS
    blk_exp = jnp.minimum(jnp.sum((blk_start[:, None] >= pad_end[None, :]).astype(jnp.int32), axis=1), N_EXPERTS - 1)
    n_used = (pad_end[-1:] // MOE_ROWS).astype(jnp.int32)
    row_tok = (jnp.arange(n_rows, dtype=jnp.int32) % n).at[dest.reshape(-1)].set(
        jnp.repeat(jnp.arange(n, dtype=jnp.int32), TOP_K))
    xs = jnp.take(hn_all.reshape(n, tiles, LANES), row_tok, axis=0, mode='clip').reshape(n_rows * tiles, LANES)

    de = w_down.shape[2]
    blk_first = jnp.concatenate([jnp.ones((1,), jnp.int32), (blk_exp[1:] != blk_exp[:-1]).astype(jnp.int32)])
    y = _moe_experts(blk_exp, blk_first, n_used, xs, w_up.reshape(N_EXPERTS, d, 2 * de),
                     b_up.reshape(N_EXPERTS, 1, 2 * de), w_down.reshape(N_EXPERTS, de, d),
                     b_down.reshape(N_EXPERTS, 1, d), d)
    yg = jnp.take(y.reshape(n_rows, tiles, LANES), dest.T.reshape(-1), axis=0, mode='clip').reshape(TOP_K, n * tiles, LANES)

    g_fin = row2(norm_final)
    y_prompt = _combine(h_p, gate_p, yg, g_fin, 0).reshape(b_p, t_p, d)
    y_sample = _combine(h_s, gate_s, yg, g_fin, n_p).reshape(b_s, t_s, d)
    k_prompt = kt_p.reshape(1, b_p, DIFF_QK_HEADS, DIFF_DH, t_p).transpose(0, 1, 4, 2, 3)
    v_prompt = v_cache_p.reshape(1, b_p, t_p, DIFF_HEADS, 2 * DIFF_DH)
    k_sample = kt_s.reshape(1, DIFF_QK_HEADS, DIFF_DH, b_s, t_s).transpose(0, 3, 4, 1, 2)
    v_sample = v_cache_s.reshape(1, b_s, t_s, DIFF_HEADS, 2 * DIFF_DH)
    return (y_prompt, y_sample, k_prompt, v_prompt, s_p[None], k_sample, v_sample, s_s[None])
```

```python
import functools
import math

import jax
import jax.numpy as jnp
from jax import lax
from jax.experimental import pallas as pl
from jax.experimental.pallas import tpu as pltpu

F32 = jnp.float32
BF16 = jnp.bfloat16

EPS = 1e-6
GLA_HEADS = 4
GLA_DK = 128
GLA_DV = 256
GLA_RANK = 16
GLA_NORMALIZER = 16.0
GLA_CHUNK = 64
GLA_SAMPLE_CHUNK = 16
GLA_PROMPT_SEQS = 4
GLA_SAMPLE_SEQS = 8
DIFF_HEADS = 8
DIFF_QK_HEADS = 2 * DIFF_HEADS
DIFF_DH = 64
ROPE_THETA = 10000.0
N_EXPERTS = 32
TOP_K = 4
SWIGLU_ALPHA = 1.702
SWIGLU_LIMIT = 7.0
LOG2E = math.log2(math.e)

LANES = 128
SUBLANES = 8
ROUTE_SUB = 256
MOE_ROWS = 512
ATTN_Q = 2048
ATTN_K = 512
ATTN_SUB = 256
ATTN_LOOKAHEAD = 3
PAGES_PER_STEP = 16
VMEM_LIMIT = 48 * 1024 * 1024
BIG_VMEM_LIMIT = 56 * 1024 * 1024


def _cparams(sem):
    return pltpu.CompilerParams(dimension_semantics=sem, vmem_limit_bytes=VMEM_LIMIT)


def _dot(a, b):
    return jnp.dot(a, b, preferred_element_type=F32)


def _dot_nt(a, b):
    return lax.dot_general(a, b, (((1,), (1,)), ((), ())), preferred_element_type=F32)


def _dot_tn(a, b):
    return lax.dot_general(a, b, (((0,), (0,)), ((), ())), preferred_element_type=F32)


def _rms(x, g):
    ms = jnp.mean(x * x, axis=-1, keepdims=True)
    return x * lax.rsqrt(ms + EPS) * g


def _sigmoid(x):
    return 1.0 / (1.0 + jnp.exp(-x))


def _row_tile(n, cap=512):
    for t in (512, 256, 128, 64, 32, 16, 8):
        if t <= cap and n % t == 0:
            return t
    raise ValueError(f"token count {n} is not a multiple of 8")


def _largest_divisor(n, cap):
    return max(t for t in range(1, cap + 1) if n % t == 0)


def _full(shape):
    return pl.BlockSpec(shape, lambda *_: (0,) * len(shape))


def _gla_proj_kernel(x_ref, g_ref, wq_ref, wk_ref, wv_ref, wlr_ref, wr_ref, gup_ref, gb_ref,
                     q_ref, k_ref, v_ref, la_ref, r_ref):
    xn = _rms(x_ref[...], g_ref[...]).astype(BF16)
    q_ref[...] = _dot(xn, wq_ref[...]) * (GLA_DK ** -0.5)
    k_ref[...] = _dot(xn, wk_ref[...])
    v_ref[...] = _dot(xn, wv_ref[...]).astype(BF16)
    glr = _dot(xn, wlr_ref[...]).astype(BF16)
    z = _dot(glr, gup_ref[...]) + gb_ref[...]
    log_sig = jnp.minimum(z, 0.0) - jnp.log1p(jnp.exp(-jnp.abs(z)))
    la_ref[...] = log_sig * (1.0 / GLA_NORMALIZER)
    r_ref[...] = _dot(xn, wr_ref[...])


def _gla_proj(x, g, wq, wk, wv, wlr, wr, gup, gb):
    n, d = x.shape
    tm = _row_tile(n)
    kd, vd = wq.shape[1], wv.shape[1]
    row = lambda w: pl.BlockSpec((tm, w), lambda i: (i, 0))
    return pl.pallas_call(
        _gla_proj_kernel,
        grid=(n // tm,),
        in_specs=[row(d), _full(g.shape), _full(wq.shape), _full(wk.shape), _full(wv.shape),
                  _full(wlr.shape), _full(wr.shape), _full(gup.shape), _full(gb.shape)],
        out_specs=[row(kd), row(kd), row(vd), row(kd), row(vd)],
        out_shape=[jax.ShapeDtypeStruct((n, kd), F32), jax.ShapeDtypeStruct((n, kd), F32),
                   jax.ShapeDtypeStruct((n, vd), BF16), jax.ShapeDtypeStruct((n, kd), F32),
                   jax.ShapeDtypeStruct((n, vd), F32)],
        compiler_params=_cparams(("parallel",)),
        name="gla_proj",
    )(x, g, wq, wk, wv, wlr, wr, gup, gb)


def _rope_chunk(xc, cos, sin_signed, first_half):
    swapped = jnp.where(first_half, pltpu.roll(xc, LANES - DIFF_DH // 2, 1), pltpu.roll(xc, DIFF_DH // 2, 1))
    return xc * cos + swapped * sin_signed


def _diff_proj_kernel(x_ref, g_ref, cos_ref, sin_ref, wq_ref, wk_ref, wv_ref, wa_ref, wb_ref,
                      q_ref, kb_ref, vb_ref, kt_ref, vc_ref, sa_ref, sb_ref):
    tm = x_ref.shape[0]
    xn = _rms(x_ref[...], g_ref[...]).astype(BF16)
    cos = cos_ref[...]
    sin_signed = sin_ref[...]
    lane = lax.broadcasted_iota(jnp.int32, (1, LANES), 1)
    first_half = (lane % DIFF_DH) < (DIFF_DH // 2)
    n_chunks = q_ref.shape[1] // LANES
    q = _dot(xn, wq_ref[...])
    for c in range(n_chunks):
        sl = slice(c * LANES, (c + 1) * LANES)
        q_ref[:, sl] = (_rope_chunk(q[:, sl], cos, sin_signed, first_half) * (DIFF_DH ** -0.5 * LOG2E)).astype(BF16)
    k = _dot(xn, wk_ref[...])
    for c in range(n_chunks):
        sl = slice(c * LANES, (c + 1) * LANES)
        kr = _rope_chunk(k[:, sl], cos, sin_signed, first_half)
        kb_ref[:, sl] = kr.astype(BF16)
        kt_ref[0, sl, :] = kr.T
    v = _dot(xn, wv_ref[...])
    vb_ref[...] = v.astype(BF16)
    for c in range(n_chunks):
        vc_ref[pl.ds(c, tm, stride=DIFF_HEADS), :] = v[:, c * LANES:(c + 1) * LANES]
    sa_ref[...] = _sigmoid(_dot(xn, wa_ref[...]))
    sb_ref[...] = _sigmoid(_dot(xn, wb_ref[...]))


def _diff_proj(x, g, cos, sin_signed, wq, wk, wv, wa, wb, seq):
    n, d = x.shape
    tm = _row_tile(seq, 256)
    tps = seq // tm
    row = lambda w: pl.BlockSpec((tm, w), lambda i: (i, 0))
    pos_row = pl.BlockSpec((tm, LANES), lambda i: (i % (cos.shape[0] // tm), 0))
    sds = lambda dt: jax.ShapeDtypeStruct((n, d), dt)
    return pl.pallas_call(
        _diff_proj_kernel,
        grid=(n // tm,),
        in_specs=[row(d), _full(g.shape), pos_row, pos_row] + [_full(wq.shape)] * 5,
        out_specs=[row(d), row(d), row(d),
                   pl.BlockSpec((1, d, tm), lambda i: (i // tps, 0, i % tps)),
                   pl.BlockSpec((tm * DIFF_HEADS, LANES), lambda i: (i, 0)),
                   row(d), row(d)],
        out_shape=[sds(BF16), sds(BF16), sds(BF16),
                   jax.ShapeDtypeStruct((n // seq, d, seq), F32),
                   jax.ShapeDtypeStruct((n * DIFF_HEADS, LANES), F32),
                   sds(F32), sds(F32)],
        compiler_params=_cparams(("parallel",)),
        name="diff_proj",
    )(x, g, cos, sin_signed, wq, wk, wv, wa, wb)


def _gla_kernel(has_s0, *refs):
    if has_s0:
        q_ref, k_ref, v_ref, la_ref, r_ref, gn_ref, s0_ref, o_ref, sfin_ref, st_ref = refs
    else:
        q_ref, k_ref, v_ref, la_ref, r_ref, gn_ref, o_ref, sfin_ref, st_ref = refs
    n = pl.program_id(1)
    nb, c, _ = q_ref.shape
    pairs = [(b, h) for b in range(nb) for h in range(GLA_HEADS)]
    slot = lambda b, h: b * GLA_HEADS + h

    @pl.when(n == 0)
    def _():
        for b, h in pairs:
            if has_s0:
                st_ref[slot(b, h)] = s0_ref[b, h]
            else:
                st_ref[slot(b, h)] = jnp.zeros(st_ref.shape[1:], F32)

    row = lax.broadcasted_iota(jnp.int32, (c, c), 0)
    col = lax.broadcasted_iota(jnp.int32, (c, c), 1)
    causal = row >= col
    tri = jnp.where(causal, 1.0, 0.0).astype(BF16)
    gn = gn_ref[...]
    ks = lambda h: slice(h * GLA_DK, (h + 1) * GLA_DK)
    vs = lambda h: slice(h * GLA_DV, (h + 1) * GLA_DV)
    ones = jnp.ones((c, GLA_DK), BF16)
    bcum, decay = [], []
    for b, h in pairs:
        la = la_ref[b, :, ks(h)]
        hi = la.astype(BF16)
        rem = la - hi.astype(F32)
        mid = rem.astype(BF16)
        lo = (rem - mid.astype(F32)).astype(BF16)
        bcum.append(_dot(tri, hi) + _dot(tri, mid) + _dot(tri, lo))
        total = _dot_tn(hi, ones) + _dot_tn(mid, ones) + _dot_tn(lo, ones)
        decay.append(jnp.exp(jnp.concatenate([total] * (GLA_DV // GLA_DK), axis=1)))
    idx = range(len(pairs))
    q_t = [(q_ref[b, :, ks(h)] * jnp.exp(bcum[i])).astype(BF16) for i, (b, h) in enumerate(pairs)]
    k_t = [(k_ref[b, :, ks(h)] * jnp.exp(-bcum[i])).astype(BF16) for i, (b, h) in enumerate(pairs)]
    k_dec = [(k_ref[b, :, ks(h)] * jnp.exp(bcum[i][c - 1:c, :] - bcum[i])).astype(BF16)
             for i, (b, h) in enumerate(pairs)]
    st = [st_ref[slot(b, h)] for b, h in pairs]
    scores = [jnp.where(causal, _dot_nt(q_t[i], k_t[i]), 0.0).astype(BF16) for i in idx]
    o_state = [_dot(q_t[i], st[i].astype(BF16)) for i in idx]
    upd = [_dot_tn(k_dec[i], v_ref[b, :, vs(h)]) for i, (b, h) in enumerate(pairs)]
    o_intra = [_dot(scores[i], v_ref[b, :, vs(h)]) for i, (b, h) in enumerate(pairs)]
    for i, (b, h) in enumerate(pairs):
        st_ref[slot(b, h)] = st[i] * decay[i] + upd[i]
        rh = r_ref[b, :, vs(h)]
        o_ref[b, :, vs(h)] = (_rms(o_state[i] + o_intra[i], gn) * (rh * _sigmoid(rh))).astype(BF16)

    @pl.when(n == pl.num_programs(1) - 1)
    def _():
        for b, h in pairs:
            sfin_ref[b, h] = st_ref[slot(b, h)]


def _gla_chunks(q, k, v, la, r, gn, s0, chunk, nb):
    batch, seq, kd = q.shape
    vd = v.shape[2]
    nc = seq // chunk
    blk = lambda w: pl.BlockSpec((nb, chunk, w), lambda b, n: (b, n, 0))
    state_spec = pl.BlockSpec((nb, GLA_HEADS, GLA_DK, GLA_DV), lambda b, n: (b, 0, 0, 0))
    in_specs = [blk(kd), blk(kd), blk(vd), blk(kd), blk(vd), _full(gn.shape)]
    args = [q, k, v, la, r, gn]
    if s0 is not None:
        in_specs.append(state_spec)
        args.append(s0)
    return pl.pallas_call(
        functools.partial(_gla_kernel, s0 is not None),
        grid=(batch // nb, nc),
        in_specs=in_specs,
        out_specs=[blk(vd), state_spec],
        out_shape=[jax.ShapeDtypeStruct((batch, seq, vd), BF16),
                   jax.ShapeDtypeStruct((batch, GLA_HEADS, GLA_DK, GLA_DV), F32)],
        scratch_shapes=[pltpu.VMEM((nb * GLA_HEADS, GLA_DK, GLA_DV), F32)],
        compiler_params=_cparams(("parallel", "arbitrary")),
        name="gla_chunks",
    )(*args)


def _lambda(lq1_ref, lk1_ref, lq2_ref, lk2_ref, lam_init):
    s1 = jnp.sum(lq1_ref[...] * lk1_ref[...], axis=-1, keepdims=True)
    s2 = jnp.sum(lq2_ref[...] * lk2_ref[...], axis=-1, keepdims=True)
    return jnp.exp(s1) - jnp.exp(s2) + lam_init


def _diff_prompt_kernel(lam_init, tk, sub, q_ref, k_ref, v_ref, lq1_ref, lk1_ref, lq2_ref, lk2_ref, sub_ref,
                        o_ref, m_ref, a_ref):
    qi = pl.program_id(2)
    tq = q_ref.shape[0]
    lane = lax.broadcasted_iota(jnp.int32, (1, LANES), 1)
    m_ref[...] = jnp.full(m_ref.shape, -jnp.inf, F32)
    a_ref[...] = jnp.zeros(a_ref.shape, F32)

    def scores(mp, r0, k0, nk, mask):
        q = q_ref[pl.ds(r0, sub), :]
        qm = jnp.where((lane < DIFF_DH) if mp == 0 else (lane >= DIFF_DH), q, jnp.zeros_like(q))
        s = _dot_nt(qm, k_ref[pl.ds(k0, nk), :])
        return s if mask is None else jnp.where(mask, s, -jnp.inf)

    def update(mp, r0, k0, nk, s):
        rows = pl.ds(r0, sub)
        n_chunks = nk // LANES
        mx = s[:, :LANES]
        for c in range(1, n_chunks):
            mx = jnp.maximum(mx, s[:, c * LANES:(c + 1) * LANES])
        m_old = m_ref[mp, rows, :]
        m_new = jnp.maximum(m_old, jnp.max(mx, axis=-1, keepdims=True))
        alpha = jnp.exp2(m_old - m_new)
        p = jnp.concatenate(
            [jnp.exp2(s[:, c * LANES:(c + 1) * LANES] - m_new).astype(BF16) for c in range(n_chunks)], axis=1)
        m_ref[mp, rows, :] = m_new
        vj = v_ref[pl.ds(k0, nk), :]
        v_ones = jnp.concatenate([vj, jnp.ones_like(vj)], axis=1)
        a_ref[mp, rows, :] = jnp.concatenate([alpha, alpha], axis=1) * a_ref[mp, rows, :] + _dot(p, v_ones)

    def run(chains):
        pending = {}
        for i in range(min(ATTN_LOOKAHEAD, len(chains))):
            pending[i] = scores(*chains[i])
        for i, (mp, r0, k0, nk, _) in enumerate(chains):
            if i + ATTN_LOOKAHEAD < len(chains):
                pending[i + ATTN_LOOKAHEAD] = scores(*chains[i + ATTN_LOOKAHEAD])
            update(mp, r0, k0, nk, pending.pop(i))

    n_sub = tq // sub

    def body(j, carry):
        k0 = pl.multiple_of(j * tk, tk)
        run([(mp, si * sub, k0, tk, None) for si in range(n_sub) for mp in range(2)])
        return carry

    lax.fori_loop(0, qi * (tq // tk), body, 0)

    base = pl.multiple_of(qi * tq, tq)
    chains = []
    for si in range(n_sub):
        need = (si + 1) * sub
        for k0 in range(0, need, tk):
            nk = min(tk, need - k0)
            mask = None
            if k0 + nk == need:
                row = lax.broadcasted_iota(jnp.int32, (sub, nk), 0) + si * sub
                col = lax.broadcasted_iota(jnp.int32, (sub, nk), 1) + k0
                mask = row >= col
            chains += [(mp, si * sub, base + k0, nk, mask) for mp in range(2)]
    run(chains)

    lam = _lambda(lq1_ref, lk1_ref, lq2_ref, lk2_ref, lam_init)
    o1 = a_ref[0, :, :LANES] * (1.0 / a_ref[0, :, LANES:])
    o2 = a_ref[1, :, :LANES] * (1.0 / a_ref[1, :, LANES:])
    o_ref[...] = (_rms(o1 - lam * o2, sub_ref[...]) * (1.0 - lam_init)).astype(BF16)


def _diff_prompt(q, k, v, lams, subln, batch, seq, lam_init):
    tq = min(ATTN_Q, seq)
    tk = min(ATTN_K, tq)
    sub = min(ATTN_SUB, tk)
    nq = seq // tq
    vec = _full(lams[0].shape)
    return pl.pallas_call(
        functools.partial(_diff_prompt_kernel, lam_init, tk, sub),
        grid=(batch, DIFF_HEADS, nq),
        in_specs=[pl.BlockSpec((tq, LANES), lambda b, h, i: (b * nq + i, h)),
                  pl.BlockSpec((seq, LANES), lambda b, h, i: (b, h)),
                  pl.BlockSpec((seq, LANES), lambda b, h, i: (b, h)),
                  vec, vec, vec, vec, _full(subln.shape)],
        out_specs=pl.BlockSpec((tq, LANES), lambda b, h, i: (b * nq + i, h)),
        out_shape=jax.ShapeDtypeStruct((batch * seq, DIFF_HEADS * LANES), BF16),
        scratch_shapes=[pltpu.VMEM((2, tq, LANES), F32), pltpu.VMEM((2, tq, 2 * LANES), F32)],
        compiler_params=_cparams(("parallel", "parallel", "arbitrary")),
        name="diff_prompt",
    )(q, k, v, *lams, subln)


def _diff_sample_kernel(lam_init, t_new, npg, pt_ref, q_ref, kn_ref, vn_ref, lq1_ref, lk1_ref, lq2_ref, lk2_ref,
                        sub_ref, *rest):
    k_refs, v_refs = rest[:npg], rest[npg:2 * npg]
    o_ref, m_ref, l_ref, acc_ref = rest[2 * npg:]
    g = pl.program_id(1)
    rows = q_ref.shape[1]
    grp = rows // DIFF_HEADS
    page = k_refs[0].shape[2]

    @pl.when(g == 0)
    def _():
        m_ref[...] = jnp.full(m_ref.shape, -jnp.inf, F32)
        l_ref[...] = jnp.zeros(l_ref.shape, F32)
        acc_ref[...] = jnp.zeros(acc_ref.shape, F32)

    q = q_ref[0]

    def attend(s, v_of_head):
        m_old = m_ref[...]
        m_new = jnp.maximum(m_old, jnp.max(s, axis=-1, keepdims=True))
        p = jnp.exp2(s - m_new)
        alpha = jnp.exp2(m_old - m_new)
        l_ref[...] = alpha * l_ref[...] + jnp.sum(p, axis=-1, keepdims=True)
        pv = [_dot(p[h * grp:(h + 1) * grp].astype(BF16), v_of_head(h)) for h in range(DIFF_HEADS)]
        acc_ref[...] = alpha * acc_ref[...] + jnp.concatenate(pv, axis=0)
        m_ref[...] = m_new

    kcat = jnp.concatenate([kr[0].astype(BF16) for kr in k_refs], axis=1)
    attend(_dot(q, kcat), lambda h: jnp.concatenate(
        [vr[0, pl.ds(h, page, stride=DIFF_HEADS), :].astype(BF16) for vr in v_refs], axis=0))

    @pl.when(g == pl.num_programs(1) - 1)
    def _():
        s_new = _dot(q, kn_ref[0])
        r = lax.broadcasted_iota(jnp.int32, s_new.shape, 0)
        c = lax.broadcasted_iota(jnp.int32, s_new.shape, 1)
        attend(jnp.where(c <= r % t_new, s_new, -jnp.inf), lambda h: vn_ref[0, h])
        lam = _lambda(lq1_ref, lk1_ref, lq2_ref, lk2_ref, lam_init)
        an = acc_ref[...] * (1.0 / l_ref[...])
        diff = an - lam * pltpu.roll(an, rows - t_new, 0)
        o_ref[0] = _rms(diff, sub_ref[...]) * (1.0 - lam_init)


def _diff_sample(page_table, q, k_new, v_new, lams, subln, cache_kt, cache_v, lam_init, t_new):
    batch, rows, _ = q.shape
    n_pages = page_table.shape[1]
    npg = _largest_divisor(n_pages, PAGES_PER_STEP)
    vec = pl.BlockSpec(lams[0].shape, lambda b, g, pt: (0, 0))
    per_batch = lambda a: pl.BlockSpec((1,) + a.shape[1:], lambda b, g, pt: (b,) + (0,) * (a.ndim - 1))

    def page_spec(a, j):
        return pl.BlockSpec((1,) + a.shape[1:], lambda b, g, pt: (pt[b * n_pages + g * npg + j], 0, 0))

    grid_spec = pltpu.PrefetchScalarGridSpec(
        num_scalar_prefetch=1,
        grid=(batch, n_pages // npg),
        in_specs=[per_batch(q), per_batch(k_new), per_batch(v_new), vec, vec, vec, vec,
                  pl.BlockSpec(subln.shape, lambda b, g, pt: (0, 0))]
        + [page_spec(cache_kt, j) for j in range(npg)] + [page_spec(cache_v, j) for j in range(npg)],
        out_specs=pl.BlockSpec((1, rows, LANES), lambda b, g, pt: (b, 0, 0)),
        scratch_shapes=[pltpu.VMEM((rows, 1), F32), pltpu.VMEM((rows, 1), F32), pltpu.VMEM((rows, LANES), F32)],
    )
    return pl.pallas_call(
        functools.partial(_diff_sample_kernel, lam_init, t_new, npg),
        grid_spec=grid_spec,
        out_shape=jax.ShapeDtypeStruct((batch, rows, LANES), F32),
        compiler_params=pltpu.CompilerParams(dimension_semantics=("parallel", "arbitrary"),
                                             vmem_limit_bytes=BIG_VMEM_LIMIT),
        name="diff_sample",
    )(page_table.reshape(-1), q, k_new, v_new, *lams, subln, *([cache_kt] * npg), *([cache_v] * npg))


def _merge_route_kernel(x_ref, og_ref, od_ref, sa_ref, sb_ref, wg_ref, wd_ref, wo_ref, gf_ref, wr_ref, br_ref,
                        cin_ref, *rest):
    h_ref, hn_ref, idx_ref, gate_ref, rank_ref, cnt_ref, carry_ref = rest[-7:]
    i = pl.program_id(0)
    tm = x_ref.shape[0]

    @pl.when(i == 0)
    def _():
        carry_ref[...] = cin_ref[...].astype(F32)

    sub = min(tm, ROUTE_SUB)
    tiles = hn_ref.shape[0] // tm
    hns = []
    for j in range(tm // sub):
        rows = pl.ds(j * sub, sub)
        merged = (sa_ref[rows, :] * _dot(og_ref[rows, :], wg_ref[...])
                  + sb_ref[rows, :] * _dot(od_ref[rows, :], wd_ref[...]))
        h = x_ref[rows, :] + _dot(merged.astype(BF16), wo_ref[...])
        h_ref[rows, :] = h
        hn = _rms(h, gf_ref[...]).astype(BF16)
        hn32 = hn.astype(F32)
        for s in range(tiles):
            hn_ref[pl.ds(j * sub * tiles + s, sub, stride=tiles), :] = hn32[:, s * LANES:(s + 1) * LANES]
        hns.append(hn)

    lane = lax.broadcasted_iota(jnp.int32, (sub, LANES), 1)
    row = lax.broadcasted_iota(jnp.int32, (sub, sub), 0)
    col = lax.broadcasted_iota(jnp.int32, (sub, sub), 1)
    strict = jnp.where(row > col, 1.0, 0.0).astype(BF16)
    n_sub = len(hns)
    logits = [jnp.where(lane < N_EXPERTS, _dot(hn, wr_ref[...]) + br_ref[...], -jnp.inf) for hn in hns]
    idx_out = [jnp.zeros((sub, LANES), jnp.int32) for _ in hns]
    val_out = [jnp.zeros((sub, LANES), F32) for _ in hns]
    onehot = [jnp.zeros((sub, LANES), F32) for _ in hns]
    picks = [[] for _ in hns]
    for kk in range(TOP_K):
        for j in range(n_sub):
            mx = jnp.max(logits[j], axis=-1, keepdims=True)
            pick = jnp.min(jnp.where(logits[j] == mx, lane, LANES), axis=-1, keepdims=True)
            hit = lane == pick
            logits[j] = jnp.where(hit, -jnp.inf, logits[j])
            onehot[j] = jnp.where(hit, 1.0, onehot[j])
            idx_out[j] = jnp.where(lane == kk, pick, idx_out[j])
            val_out[j] = jnp.where(lane == kk, mx, val_out[j])
            picks[j].append(hit)
    carry = carry_ref[...]
    for j in range(n_sub):
        rows = pl.ds(j * sub, sub)
        e = jnp.where(lane < TOP_K, jnp.exp(val_out[j] - val_out[j][:, :1]), 0.0)
        gate_ref[rows, :] = e * (1.0 / jnp.sum(e, axis=-1, keepdims=True))
        idx_ref[rows, :] = idx_out[j]
        before = _dot(strict, onehot[j].astype(BF16)) + carry
        rank_out = jnp.zeros((sub, LANES), jnp.int32)
        for kk in range(TOP_K):
            rk = jnp.sum(jnp.where(picks[j][kk], before, 0.0), axis=-1, keepdims=True)
            rank_out = jnp.where(lane == kk, rk.astype(jnp.int32), rank_out)
        rank_ref[rows, :] = rank_out
        carry = carry + jnp.sum(onehot[j], axis=0, keepdims=True)
    carry_ref[...] = carry
    cnt_ref[...] = carry.astype(jnp.int32)


def _merge_route(x, og, od, sa, sb, wg, wd, wo, gf, wr, br, counts_in, n_all, tok0, hn_all=None):
    n, d = x.shape
    tm = _row_tile(n, 512)
    assert tok0 % tm == 0
    b0 = tok0 // tm
    tiles = d // LANES
    row = lambda w: pl.BlockSpec((tm, w), lambda i: (i, 0))
    in_specs = [row(d)] * 5 + [_full(wg.shape), _full(wd.shape), _full(wo.shape), _full(gf.shape),
                               _full(wr.shape), _full(br.shape), _full(counts_in.shape)]
    args = [x, og, od, sa, sb, wg, wd, wo, gf, wr, br, counts_in]
    aliases = {}
    if hn_all is not None:
        in_specs.append(pl.BlockSpec(memory_space=pl.ANY))
        aliases = {len(args): 1}
        args.append(hn_all)
    return pl.pallas_call(
        _merge_route_kernel,
        grid=(n // tm,),
        in_specs=in_specs,
        out_specs=[row(d), pl.BlockSpec((tm * tiles, LANES), lambda i: (b0 + i, 0)),
                   row(LANES), row(LANES), row(LANES), _full((1, LANES))],
        out_shape=[jax.ShapeDtypeStruct((n, d), F32), jax.ShapeDtypeStruct((n_all * tiles, LANES), F32),
                   jax.ShapeDtypeStruct((n, LANES), jnp.int32), jax.ShapeDtypeStruct((n, LANES), F32),
                   jax.ShapeDtypeStruct((n, LANES), jnp.int32), jax.ShapeDtypeStruct((1, LANES), jnp.int32)],
        scratch_shapes=[pltpu.VMEM((1, LANES), F32)],
        input_output_aliases=aliases,
        compiler_params=_cparams(("arbitrary",)),
        name="merge_route",
    )(*args)


def _pair_shuffle(h):
    a, b = h[:, :LANES], h[:, LANES:]
    even = (lax.broadcasted_iota(jnp.int32, (1, LANES), 1) % 2) == 0
    glu = jnp.where(even, a, pltpu.roll(b, 1, 1))
    lin = jnp.where(even, pltpu.roll(a, LANES - 1, 1), b)
    return glu, lin


def _moe_kernel(be_ref, first_ref, nu_ref, xo_ref, x_ref, wu_ref, bu_ref, wd_ref, bd_ref, y_ref,
                xs_ref, wub_ref, wdb_ref, wds_ref):
    del xo_ref
    i = pl.program_id(0)
    bm, d = xs_ref.shape
    de = wdb_ref.shape[0]
    n_tiles = d // LANES
    half = LANES // 2

    @pl.when((i < nu_ref[0]) & (first_ref[i] == 1))
    def _():
        wub_ref[...] = wu_ref[0].astype(BF16)
        for t in range(n_tiles):
            lanes = slice(t * LANES, (t + 1) * LANES)
            for c in range(de // LANES):
                for a in range(2):
                    src = c * LANES + a * half
                    wds_ref[t, pl.ds(c * LANES + a, half, stride=2), :] = wd_ref[0, src:src + half, lanes]
            wdb_ref[:, lanes] = wds_ref[t].astype(BF16)

    @pl.when(i < nu_ref[0])
    def _():
        for s in range(n_tiles):
            xs_ref[:, s * LANES:(s + 1) * LANES] = x_ref[pl.ds(s, bm, stride=SUBLANES), :].astype(BF16)
        h = _dot(xs_ref[...], wub_ref[...]) + bu_ref[0]
        acts = []
        for c in range(h.shape[1] // (2 * LANES)):
            glu, lin = _pair_shuffle(h[:, 2 * c * LANES:2 * (c + 1) * LANES])
            x_glu = jnp.minimum(glu, SWIGLU_LIMIT)
            x_lin = jnp.clip(lin, -SWIGLU_LIMIT, SWIGLU_LIMIT)
            acts.append((x_glu * _sigmoid(SWIGLU_ALPHA * x_glu) * (x_lin + 1.0)).astype(BF16))
        y = _dot(jnp.concatenate(acts, axis=1), wdb_ref[...]) + bd_ref[0]
        for s in range(n_tiles):
            y_ref[pl.ds(s, bm, stride=SUBLANES), :] = y[:, s * LANES:(s + 1) * LANES]

    @pl.when(i >= nu_ref[0])
    def _():
        y_ref[...] = jnp.zeros(y_ref.shape, F32)


def _moe_experts(blk_exp, blk_first, n_used, x_off, xs, wu, bu, wd, bd, d, n_rows):
    tiles = d // LANES
    de2 = wu.shape[2]
    wspec = lambda a, b: pl.BlockSpec((1, a, b), lambda i, be, bf, nu, xo: (be[i], 0, 0))
    rows_in = pl.BlockSpec((pl.Element(MOE_ROWS * tiles), pl.Element(LANES)),
                           lambda i, be, bf, nu, xo: (pl.multiple_of(xo[i], SUBLANES), 0))
    rows = pl.BlockSpec((MOE_ROWS * tiles, LANES), lambda i, be, bf, nu, xo: (i, 0))
    grid_spec = pltpu.PrefetchScalarGridSpec(
        num_scalar_prefetch=4,
        grid=(n_rows // MOE_ROWS,),
        in_specs=[rows_in, wspec(d, de2), wspec(1, de2), wspec(de2 // 2, d), wspec(1, d)],
        out_specs=rows,
        scratch_shapes=[pltpu.VMEM((MOE_ROWS, d), BF16), pltpu.VMEM((d, de2), BF16),
                        pltpu.VMEM((de2 // 2, d), BF16), pltpu.VMEM((d // LANES, de2 // 2, LANES), F32)],
    )
    return pl.pallas_call(
        _moe_kernel,
        grid_spec=grid_spec,
        out_shape=jax.ShapeDtypeStruct((n_rows * tiles, LANES), F32),
        compiler_params=pltpu.CompilerParams(dimension_semantics=("arbitrary",), vmem_limit_bytes=BIG_VMEM_LIMIT),
        name="moe_experts",
    )(blk_exp, blk_first, n_used, x_off, xs, wu, bu, wd, bd)


def _combine_kernel(h_ref, gate_ref, y0_ref, y1_ref, y2_ref, y3_ref, gf_ref, o_ref):
    tm, d = h_ref.shape
    gate = gate_ref[...]
    parts = []
    for s in range(d // LANES):
        acc = h_ref[:, s * LANES:(s + 1) * LANES]
        for kk, y_ref in enumerate((y0_ref, y1_ref, y2_ref, y3_ref)):
            acc = acc + gate[:, kk:kk + 1] * y_ref[0, pl.ds(s, tm, stride=SUBLANES), :]
        parts.append(acc)
    o_ref[...] = _rms(jnp.concatenate(parts, axis=1), gf_ref[...])


def _combine(h, gate, yg, gf, tok0):
    n, d = h.shape
    tm = _row_tile(n, 256)
    assert tok0 % tm == 0
    b0 = tok0 // tm
    tiles = d // LANES
    row = lambda w: pl.BlockSpec((tm, w), lambda i: (i, 0))
    ysp = lambda kk: pl.BlockSpec((1, tm * tiles, LANES), lambda i: (kk, b0 + i, 0))
    return pl.pallas_call(
        _combine_kernel,
        grid=(n // tm,),
        in_specs=[row(d), row(LANES)] + [ysp(kk) for kk in range(TOP_K)] + [_full(gf.shape)],
        out_specs=row(d),
        out_shape=jax.ShapeDtypeStruct((n, d), F32),
        compiler_params=_cparams(("parallel",)),
        name="combine",
    )(h, gate, yg, yg, yg, yg, gf)


def _rope_tables(pos):
    half = DIFF_DH // 2
    inv_freq = ROPE_THETA ** (-jnp.arange(half, dtype=F32) / half)
    ang = pos.astype(F32)[:, None] * inv_freq[None, :]
    cos, sin = jnp.cos(ang), jnp.sin(ang)
    reps = LANES // DIFF_DH
    return jnp.tile(jnp.concatenate([cos, cos], axis=1), (1, reps)), jnp.tile(jnp.concatenate([-sin, sin], axis=1), (1, reps))


def kernel(x_prompt, x_sample, cache_k, cache_v, state_gla, page_table, norm_attn, w_in, gla_gate_up, gla_gate_bias, gla_norm, lam_q1, lam_k1, lam_q2, lam_k2, diff_subln, w_branch_gla, w_branch_diff, w_out, norm_ffn, w_router, b_router, w_up, b_up, w_down, b_down, norm_final):
    depth = w_in.shape[0]
    assert depth == 1, "single-layer step"
    layer = 0
    b_p, t_p, d = x_prompt.shape
    b_s, t_s, _ = x_sample.shape
    n_p, n_s = b_p * t_p, b_s * t_s
    n = n_p + n_s
    n_pool, page = cache_k.shape[1], cache_k.shape[2]
    n_pages = page_table.shape[1]
    past_len = n_pages * page
    kd, vd = GLA_HEADS * GLA_DK, GLA_HEADS * GLA_DV
    tiles = d // LANES
    row2 = lambda a: a.reshape(1, -1)
    assert t_s <= LANES

    splits = (kd, kd, vd, GLA_RANK, vd, d, d, d, d, d)
    offs = [0]
    for s in splits:
        offs.append(offs[-1] + s)
    wb = w_in.reshape(d, -1).astype(BF16)
    w_gq, w_gk, w_gv, w_glr, w_gr, w_dq, w_dk, w_dv, w_ga, w_gb = [wb[:, offs[j]:offs[j + 1]] for j in range(10)]
    w_glr = jnp.pad(w_glr, ((0, 0), (0, LANES - GLA_RANK)))
    gup = jnp.pad(gla_gate_up.reshape(GLA_RANK, kd).astype(BF16), ((0, LANES - GLA_RANK), (0, 0)))
    g_attn = row2(norm_attn)
    gn = row2(gla_norm)
    lam_init = 0.8 - 0.6 * math.exp(-0.3 * layer)
    lams = (row2(lam_q1), row2(lam_k1), row2(lam_q2), row2(lam_k2))
    subln = row2(diff_subln)
    w_bg = w_branch_gla.reshape(vd, d).astype(BF16)
    w_bd = w_branch_diff.reshape(d, d).astype(BF16)
    w_o = w_out.reshape(d, d).astype(BF16)
    w_r = jnp.pad(w_router.reshape(d, N_EXPERTS).astype(BF16), ((0, 0), (0, LANES - N_EXPERTS)))
    b_r = jnp.pad(row2(b_router), ((0, 0), (0, LANES - N_EXPERTS)))
    g_ffn = row2(norm_ffn)

    def mixers(x, pos, seq):
        gq, gk, gv, gla_la, gr = _gla_proj(x, g_attn, w_gq, w_gk, w_gv, w_glr, w_gr, gup, row2(gla_gate_bias))
        cos, sin_signed = _rope_tables(pos)
        return (gq, gk, gv, gla_la, gr) + tuple(
            _diff_proj(x, g_attn, cos, sin_signed, w_dq, w_dk, w_dv, w_ga, w_gb, seq))

    x_p = x_prompt.reshape(n_p, d)
    gq, gk, gv, gla_la, gr, dq, dk_b, dv_b, kt_p, v_cache_p, sig_a, sig_b = mixers(
        x_p, jnp.arange(t_p, dtype=jnp.int32), t_p)
    seqs = lambda a: a.reshape(b_p, t_p, -1)
    og_p, s_p = _gla_chunks(seqs(gq), seqs(gk), seqs(gv), seqs(gla_la), seqs(gr), gn, None,
                            GLA_CHUNK if t_p % GLA_CHUNK == 0 else t_p, _largest_divisor(b_p, GLA_PROMPT_SEQS))
    og_p = og_p.reshape(n_p, vd)
    od_p = _diff_prompt(dq, dk_b, dv_b, lams, subln, b_p, t_p, lam_init)
    zero_counts = jnp.zeros((1, LANES), jnp.int32)
    h_p, hn_p, idx_p, gate_p, rank_p, counts_p = _merge_route(
        x_p, og_p, od_p, sig_a, sig_b, w_bg, w_bd, w_o, g_ffn, w_r, b_r, zero_counts, n, 0)

    x_s = x_sample.reshape(n_s, d)
    gq, gk, gv, gla_la, gr, dq, dk_b, dv_b, kt_s, v_cache_s, sig_a, sig_b = mixers(
        x_s, jnp.tile(past_len + jnp.arange(t_s, dtype=jnp.int32), b_s), n_s)
    t_pad = -(-t_s // GLA_SAMPLE_CHUNK) * GLA_SAMPLE_CHUNK
    pad_s = lambda a: jnp.pad(a.reshape(b_s, t_s, -1), ((0, 0), (0, t_pad - t_s), (0, 0)))
    og_s, s_s = _gla_chunks(pad_s(gq), pad_s(gk), pad_s(gv), pad_s(gla_la), pad_s(gr), gn,
                            state_gla.reshape(b_s, GLA_HEADS, GLA_DK, GLA_DV), t_pad,
                            _largest_divisor(b_s, GLA_SAMPLE_SEQS))
    og_s = og_s[:, :t_s].reshape(n_s, vd)
    q4 = dq.reshape(b_s, t_s, DIFF_QK_HEADS, DIFF_DH).transpose(0, 2, 1, 3)
    q_bd = (q4[:, :, :, None, :] * jnp.eye(DIFF_QK_HEADS, dtype=BF16)[None, :, None, :, None]).reshape(
        b_s, DIFF_QK_HEADS * t_s, d)
    kn = jnp.pad(dk_b.reshape(b_s, t_s, d).transpose(0, 2, 1), ((0, 0), (0, 0), (0, LANES - t_s)))
    vn = jnp.pad(dv_b.reshape(b_s, t_s, DIFF_HEADS, LANES).transpose(0, 2, 1, 3),
                 ((0, 0), (0, 0), (0, LANES - t_s), (0, 0)))
    cache_kt = jnp.transpose(cache_k, (0, 1, 3, 4, 2)).reshape(depth * n_pool, DIFF_QK_HEADS * DIFF_DH, page)
    od_s = _diff_sample(page_table, q_bd, kn, vn, lams, subln, cache_kt,
                        cache_v.reshape(depth * n_pool, page * DIFF_HEADS, LANES), lam_init, t_s)
    od_s = od_s.reshape(b_s, DIFF_HEADS, 2, t_s, LANES)[:, :, 0].transpose(0, 2, 1, 3).reshape(n_s, d).astype(BF16)
    h_s, hn_all, idx_s, gate_s, rank_s, counts = _merge_route(
        x_s, og_s, od_s, sig_a, sig_b, w_bg, w_bd, w_o, g_ffn, w_r, b_r, counts_p, n, n_p, hn_p)

    counts = counts[0, :N_EXPERTS]
    padded = (counts + MOE_ROWS - 1) // MOE_ROWS * MOE_ROWS
    pad_end = jnp.cumsum(padded)
    pad_start = pad_end - padded
    idx4 = jnp.concatenate([idx_p[:, :TOP_K], idx_s[:, :TOP_K]], axis=0)
    rank4 = jnp.concatenate([rank_p[:, :TOP_K], rank_s[:, :TOP_K]], axis=0)
    dest = pad_start[idx4] + rank4
    n_rows = -(-(n * TOP_K + N_EXPERTS * (MOE_ROWS - 1)) // MOE_ROWS) * MOE_ROWS
    n_blocks = n_rows // MOE_ROWS
    blk_start = jnp.arange(n_blocks, dtype=jnp.int32) * MOE_ROWS
    blk_exp = jnp.minimum(jnp.sum((blk_start[:, None] >= pad_end[None, :]).astype(jnp.int32), axis=1), N_EXPERTS - 1)
    n_used = (pad_end[-1:] // MOE_ROWS).astype(jnp.int32)
    start = jnp.cumsum(counts) - counts
    _, sorted_tok = lax.sort_key_val((start[idx4] + rank4).reshape(-1),
                                     jnp.repeat(jnp.arange(n, dtype=jnp.int32), TOP_K))
    src_tok = jnp.concatenate([sorted_tok, jnp.arange(MOE_ROWS, dtype=jnp.int32) % n])
    xs = jnp.take(hn_all.reshape(n, tiles, LANES), src_tok, axis=0, mode='clip').reshape(-1, LANES)
    blk_in_exp = blk_start - pad_start[blk_exp]
    x_off = jnp.clip(start[blk_exp] + blk_in_exp, 0, n * TOP_K).astype(jnp.int32) * tiles

    de = w_down.shape[2]
    blk_first = jnp.concatenate([jnp.ones((1,), jnp.int32), (blk_exp[1:] != blk_exp[:-1]).astype(jnp.int32)])
    y = _moe_experts(blk_exp, blk_first, n_used, x_off, xs, w_up.reshape(N_EXPERTS, d, 2 * de),
                     b_up.reshape(N_EXPERTS, 1, 2 * de), w_down.reshape(N_EXPERTS, de, d),
                     b_down.reshape(N_EXPERTS, 1, d), d, n_rows)
    yg = jnp.take(y.reshape(n_rows, tiles, LANES), dest.T.reshape(-1), axis=0, mode='clip').reshape(TOP_K, n * tiles, LANES)

    g_fin = row2(norm_final)
    y_prompt = _combine(h_p, gate_p, yg, g_fin, 0).reshape(b_p, t_p, d)
    y_sample = _combine(h_s, gate_s, yg, g_fin, n_p).reshape(b_s, t_s, d)
    k_prompt = kt_p.reshape(1, b_p, DIFF_QK_HEADS, DIFF_DH, t_p).transpose(0, 1, 4, 2, 3)
    v_prompt = v_cache_p.reshape(1, b_p, t_p, DIFF_HEADS, 2 * DIFF_DH)
    k_sample = kt_s.reshape(1, DIFF_QK_HEADS, DIFF_DH, b_s, t_s).transpose(0, 3, 4, 1, 2)
    v_sample = v_cache_s.reshape(1, b_s, t_s, DIFF_HEADS, 2 * DIFF_DH)
    return (y_prompt, y_sample, k_prompt, v_prompt, s_p[None], k_sample, v_sample, s_s[None])
```

```python
import functools
import math

import jax
import jax.numpy as jnp
from jax import lax
from jax.experimental import pallas as pl
from jax.experimental.pallas import tpu as pltpu

F32 = jnp.float32
BF16 = jnp.bfloat16

EPS = 1e-6
GLA_HEADS = 4
GLA_DK = 128
GLA_DV = 256
GLA_RANK = 16
GLA_NORMALIZER = 16.0
GLA_CHUNK = 64
GLA_SAMPLE_CHUNK = 16
GLA_PROMPT_SEQS = 4
GLA_SAMPLE_SEQS = 8
DIFF_HEADS = 8
DIFF_QK_HEADS = 2 * DIFF_HEADS
DIFF_DH = 64
ROPE_THETA = 10000.0
N_EXPERTS = 32
TOP_K = 4
SWIGLU_ALPHA = 1.702
SWIGLU_LIMIT = 7.0
LOG2E = math.log2(math.e)

LANES = 128
SUBLANES = 8
ROUTE_SUB = 256
MOE_ROWS = 512
ATTN_Q = 2048
ATTN_K = 512
ATTN_SUB = 256
ATTN_LOOKAHEAD = 3
PAGES_PER_STEP = 16
VMEM_LIMIT = 48 * 1024 * 1024
BIG_VMEM_LIMIT = 56 * 1024 * 1024


def _cparams(sem):
    return pltpu.CompilerParams(dimension_semantics=sem, vmem_limit_bytes=VMEM_LIMIT)


def _dot(a, b):
    return jnp.dot(a, b, preferred_element_type=F32)


def _dot_nt(a, b):
    return lax.dot_general(a, b, (((1,), (1,)), ((), ())), preferred_element_type=F32)


def _dot_tn(a, b):
    return lax.dot_general(a, b, (((0,), (0,)), ((), ())), preferred_element_type=F32)


def _rms(x, g):
    ms = jnp.mean(x * x, axis=-1, keepdims=True)
    return x * lax.rsqrt(ms + EPS) * g


def _sigmoid(x):
    return 1.0 / (1.0 + jnp.exp(-x))


def _row_tile(n, cap=512):
    for t in (512, 256, 128, 64, 32, 16, 8):
        if t <= cap and n % t == 0:
            return t
    raise ValueError(f"token count {n} is not a multiple of 8")


def _largest_divisor(n, cap):
    return max(t for t in range(1, cap + 1) if n % t == 0)


def _full(shape):
    return pl.BlockSpec(shape, lambda *_: (0,) * len(shape))


def _gla_proj_kernel(x_ref, g_ref, wq_ref, wk_ref, wv_ref, wlr_ref, wr_ref, gup_ref, gb_ref,
                     q_ref, k_ref, v_ref, la_ref, r_ref):
    xn = _rms(x_ref[...], g_ref[...]).astype(BF16)
    q_ref[...] = _dot(xn, wq_ref[...]) * (GLA_DK ** -0.5)
    k_ref[...] = _dot(xn, wk_ref[...])
    v_ref[...] = _dot(xn, wv_ref[...]).astype(BF16)
    glr = _dot(xn, wlr_ref[...]).astype(BF16)
    z = _dot(glr, gup_ref[...]) + gb_ref[...]
    log_sig = jnp.minimum(z, 0.0) - jnp.log1p(jnp.exp(-jnp.abs(z)))
    la_ref[...] = log_sig * (1.0 / GLA_NORMALIZER)
    r_ref[...] = _dot(xn, wr_ref[...])


def _gla_proj(x, g, wq, wk, wv, wlr, wr, gup, gb):
    n, d = x.shape
    tm = _row_tile(n)
    kd, vd = wq.shape[1], wv.shape[1]
    row = lambda w: pl.BlockSpec((tm, w), lambda i: (i, 0))
    return pl.pallas_call(
        _gla_proj_kernel,
        grid=(n // tm,),
        in_specs=[row(d), _full(g.shape), _full(wq.shape), _full(wk.shape), _full(wv.shape),
                  _full(wlr.shape), _full(wr.shape), _full(gup.shape), _full(gb.shape)],
        out_specs=[row(kd), row(kd), row(vd), row(kd), row(vd)],
        out_shape=[jax.ShapeDtypeStruct((n, kd), F32), jax.ShapeDtypeStruct((n, kd), F32),
                   jax.ShapeDtypeStruct((n, vd), BF16), jax.ShapeDtypeStruct((n, kd), F32),
                   jax.ShapeDtypeStruct((n, vd), F32)],
        compiler_params=_cparams(("parallel",)),
        name="gla_proj",
    )(x, g, wq, wk, wv, wlr, wr, gup, gb)


def _rope_chunk(xc, cos, sin_signed, first_half):
    swapped = jnp.where(first_half, pltpu.roll(xc, LANES - DIFF_DH // 2, 1), pltpu.roll(xc, DIFF_DH // 2, 1))
    return xc * cos + swapped * sin_signed


def _diff_proj_kernel(x_ref, g_ref, cos_ref, sin_ref, wq_ref, wk_ref, wv_ref, wa_ref, wb_ref,
                      q_ref, kb_ref, vb_ref, kt_ref, vc_ref, sa_ref, sb_ref):
    tm = x_ref.shape[0]
    xn = _rms(x_ref[...], g_ref[...]).astype(BF16)
    cos = cos_ref[...]
    sin_signed = sin_ref[...]
    lane = lax.broadcasted_iota(jnp.int32, (1, LANES), 1)
    first_half = (lane % DIFF_DH) < (DIFF_DH // 2)
    n_chunks = q_ref.shape[1] // LANES
    q = _dot(xn, wq_ref[...])
    for c in range(n_chunks):
        sl = slice(c * LANES, (c + 1) * LANES)
        q_ref[:, sl] = (_rope_chunk(q[:, sl], cos, sin_signed, first_half) * (DIFF_DH ** -0.5 * LOG2E)).astype(BF16)
    k = _dot(xn, wk_ref[...])
    for c in range(n_chunks):
        sl = slice(c * LANES, (c + 1) * LANES)
        kr = _rope_chunk(k[:, sl], cos, sin_signed, first_half)
        kb_ref[:, sl] = kr.astype(BF16)
        kt_ref[0, sl, :] = kr.T
    v = _dot(xn, wv_ref[...])
    vb_ref[...] = v.astype(BF16)
    for c in range(n_chunks):
        vc_ref[pl.ds(c, tm, stride=DIFF_HEADS), :] = v[:, c * LANES:(c + 1) * LANES]
    sa_ref[...] = _sigmoid(_dot(xn, wa_ref[...]))
    sb_ref[...] = _sigmoid(_dot(xn, wb_ref[...]))


def _diff_proj(x, g, cos, sin_signed, wq, wk, wv, wa, wb, seq):
    n, d = x.shape
    tm = _row_tile(seq, 256)
    tps = seq // tm
    row = lambda w: pl.BlockSpec((tm, w), lambda i: (i, 0))
    pos_row = pl.BlockSpec((tm, LANES), lambda i: (i % (cos.shape[0] // tm), 0))
    sds = lambda dt: jax.ShapeDtypeStruct((n, d), dt)
    return pl.pallas_call(
        _diff_proj_kernel,
        grid=(n // tm,),
        in_specs=[row(d), _full(g.shape), pos_row, pos_row] + [_full(wq.shape)] * 5,
        out_specs=[row(d), row(d), row(d),
                   pl.BlockSpec((1, d, tm), lambda i: (i // tps, 0, i % tps)),
                   pl.BlockSpec((tm * DIFF_HEADS, LANES), lambda i: (i, 0)),
                   row(d), row(d)],
        out_shape=[sds(BF16), sds(BF16), sds(BF16),
                   jax.ShapeDtypeStruct((n // seq, d, seq), F32),
                   jax.ShapeDtypeStruct((n * DIFF_HEADS, LANES), F32),
                   sds(F32), sds(F32)],
        compiler_params=_cparams(("parallel",)),
        name="diff_proj",
    )(x, g, cos, sin_signed, wq, wk, wv, wa, wb)


def _gla_kernel(has_s0, *refs):
    if has_s0:
        q_ref, k_ref, v_ref, la_ref, r_ref, gn_ref, s0_ref, o_ref, sfin_ref, st_ref = refs
    else:
        q_ref, k_ref, v_ref, la_ref, r_ref, gn_ref, o_ref, sfin_ref, st_ref = refs
    n = pl.program_id(1)
    nb, c, _ = q_ref.shape
    pairs = [(b, h) for b in range(nb) for h in range(GLA_HEADS)]
    slot = lambda b, h: b * GLA_HEADS + h

    @pl.when(n == 0)
    def _():
        for b, h in pairs:
            if has_s0:
                st_ref[slot(b, h)] = s0_ref[b, h]
            else:
                st_ref[slot(b, h)] = jnp.zeros(st_ref.shape[1:], F32)

    row = lax.broadcasted_iota(jnp.int32, (c, c), 0)
    col = lax.broadcasted_iota(jnp.int32, (c, c), 1)
    causal = row >= col
    tri = jnp.where(causal, 1.0, 0.0).astype(BF16)
    gn = gn_ref[...]
    ks = lambda h: slice(h * GLA_DK, (h + 1) * GLA_DK)
    vs = lambda h: slice(h * GLA_DV, (h + 1) * GLA_DV)
    ones = jnp.ones((c, GLA_DK), BF16)
    bcum, decay = [], []
    for b, h in pairs:
        la = la_ref[b, :, ks(h)]
        hi = la.astype(BF16)
        rem = la - hi.astype(F32)
        mid = rem.astype(BF16)
        lo = (rem - mid.astype(F32)).astype(BF16)
        bcum.append(_dot(tri, hi) + _dot(tri, mid) + _dot(tri, lo))
        total = _dot_tn(hi, ones) + _dot_tn(mid, ones) + _dot_tn(lo, ones)
        decay.append(jnp.exp(jnp.concatenate([total] * (GLA_DV // GLA_DK), axis=1)))
    idx = range(len(pairs))
    q_t = [(q_ref[b, :, ks(h)] * jnp.exp(bcum[i])).astype(BF16) for i, (b, h) in enumerate(pairs)]
    k_t = [(k_ref[b, :, ks(h)] * jnp.exp(-bcum[i])).astype(BF16) for i, (b, h) in enumerate(pairs)]
    k_dec = [(k_ref[b, :, ks(h)] * jnp.exp(bcum[i][c - 1:c, :] - bcum[i])).astype(BF16)
             for i, (b, h) in enumerate(pairs)]
    st = [st_ref[slot(b, h)] for b, h in pairs]
    scores = [jnp.where(causal, _dot_nt(q_t[i], k_t[i]), 0.0).astype(BF16) for i in idx]
    o_state = [_dot(q_t[i], st[i].astype(BF16)) for i in idx]
    upd = [_dot_tn(k_dec[i], v_ref[b, :, vs(h)]) for i, (b, h) in enumerate(pairs)]
    o_intra = [_dot(scores[i], v_ref[b, :, vs(h)]) for i, (b, h) in enumerate(pairs)]
    for i, (b, h) in enumerate(pairs):
        st_ref[slot(b, h)] = st[i] * decay[i] + upd[i]
        rh = r_ref[b, :, vs(h)]
        o_ref[b, :, vs(h)] = (_rms(o_state[i] + o_intra[i], gn) * (rh * _sigmoid(rh))).astype(BF16)

    @pl.when(n == pl.num_programs(1) - 1)
    def _():
        for b, h in pairs:
            sfin_ref[b, h] = st_ref[slot(b, h)]


def _gla_chunks(q, k, v, la, r, gn, s0, chunk, nb):
    batch, seq, kd = q.shape
    vd = v.shape[2]
    nc = seq // chunk
    blk = lambda w: pl.BlockSpec((nb, chunk, w), lambda b, n: (b, n, 0))
    state_spec = pl.BlockSpec((nb, GLA_HEADS, GLA_DK, GLA_DV), lambda b, n: (b, 0, 0, 0))
    in_specs = [blk(kd), blk(kd), blk(vd), blk(kd), blk(vd), _full(gn.shape)]
    args = [q, k, v, la, r, gn]
    if s0 is not None:
        in_specs.append(state_spec)
        args.append(s0)
    return pl.pallas_call(
        functools.partial(_gla_kernel, s0 is not None),
        grid=(batch // nb, nc),
        in_specs=in_specs,
        out_specs=[blk(vd), state_spec],
        out_shape=[jax.ShapeDtypeStruct((batch, seq, vd), BF16),
                   jax.ShapeDtypeStruct((batch, GLA_HEADS, GLA_DK, GLA_DV), F32)],
        scratch_shapes=[pltpu.VMEM((nb * GLA_HEADS, GLA_DK, GLA_DV), F32)],
        compiler_params=_cparams(("parallel", "arbitrary")),
        name="gla_chunks",
    )(*args)


def _lambda(lq1_ref, lk1_ref, lq2_ref, lk2_ref, lam_init):
    s1 = jnp.sum(lq1_ref[...] * lk1_ref[...], axis=-1, keepdims=True)
    s2 = jnp.sum(lq2_ref[...] * lk2_ref[...], axis=-1, keepdims=True)
    return jnp.exp(s1) - jnp.exp(s2) + lam_init


def _diff_prompt_kernel(lam_init, tk, sub, q_ref, k_ref, v_ref, lq1_ref, lk1_ref, lq2_ref, lk2_ref, sub_ref,
                        o_ref, m_ref, a_ref):
    qi = pl.program_id(2)
    tq = q_ref.shape[0]
    lane = lax.broadcasted_iota(jnp.int32, (1, LANES), 1)
    m_ref[...] = jnp.full(m_ref.shape, -jnp.inf, F32)
    a_ref[...] = jnp.zeros(a_ref.shape, F32)

    def scores(mp, r0, k0, nk, mask):
        q = q_ref[pl.ds(r0, sub), :]
        qm = jnp.where((lane < DIFF_DH) if mp == 0 else (lane >= DIFF_DH), q, jnp.zeros_like(q))
        s = _dot_nt(qm, k_ref[pl.ds(k0, nk), :])
        return s if mask is None else jnp.where(mask, s, -jnp.inf)

    def update(mp, r0, k0, nk, s):
        rows = pl.ds(r0, sub)
        n_chunks = nk // LANES
        mx = s[:, :LANES]
        for c in range(1, n_chunks):
            mx = jnp.maximum(mx, s[:, c * LANES:(c + 1) * LANES])
        m_old = m_ref[mp, rows, :]
        m_new = jnp.maximum(m_old, jnp.max(mx, axis=-1, keepdims=True))
        alpha = jnp.exp2(m_old - m_new)
        p = jnp.concatenate(
            [jnp.exp2(s[:, c * LANES:(c + 1) * LANES] - m_new).astype(BF16) for c in range(n_chunks)], axis=1)
        m_ref[mp, rows, :] = m_new
        vj = v_ref[pl.ds(k0, nk), :]
        v_ones = jnp.concatenate([vj, jnp.ones_like(vj)], axis=1)
        a_ref[mp, rows, :] = jnp.concatenate([alpha, alpha], axis=1) * a_ref[mp, rows, :] + _dot(p, v_ones)

    def run(chains):
        pending = {}
        for i in range(min(ATTN_LOOKAHEAD, len(chains))):
            pending[i] = scores(*chains[i])
        for i, (mp, r0, k0, nk, _) in enumerate(chains):
            if i + ATTN_LOOKAHEAD < len(chains):
                pending[i + ATTN_LOOKAHEAD] = scores(*chains[i + ATTN_LOOKAHEAD])
            update(mp, r0, k0, nk, pending.pop(i))

    n_sub = tq // sub

    def body(j, carry):
        k0 = pl.multiple_of(j * tk, tk)
        run([(mp, si * sub, k0, tk, None) for si in range(n_sub) for mp in range(2)])
        return carry

    lax.fori_loop(0, qi * (tq // tk), body, 0)

    base = pl.multiple_of(qi * tq, tq)
    chains = []
    for si in range(n_sub):
        need = (si + 1) * sub
        for k0 in range(0, need, tk):
            nk = min(tk, need - k0)
            mask = None
            if k0 + nk == need:
                row = lax.broadcasted_iota(jnp.int32, (sub, nk), 0) + si * sub
                col = lax.broadcasted_iota(jnp.int32, (sub, nk), 1) + k0
                mask = row >= col
            chains += [(mp, si * sub, base + k0, nk, mask) for mp in range(2)]
    run(chains)

    lam = _lambda(lq1_ref, lk1_ref, lq2_ref, lk2_ref, lam_init)
    o1 = a_ref[0, :, :LANES] * (1.0 / a_ref[0, :, LANES:])
    o2 = a_ref[1, :, :LANES] * (1.0 / a_ref[1, :, LANES:])
    o_ref[...] = (_rms(o1 - lam * o2, sub_ref[...]) * (1.0 - lam_init)).astype(BF16)


def _diff_prompt(q, k, v, lams, subln, batch, seq, lam_init):
    tq = min(ATTN_Q, seq)
    tk = min(ATTN_K, tq)
    sub = min(ATTN_SUB, tk)
    nq = seq // tq
    vec = _full(lams[0].shape)
    return pl.pallas_call(
        functools.partial(_diff_prompt_kernel, lam_init, tk, sub),
        grid=(batch, DIFF_HEADS, nq),
        in_specs=[pl.BlockSpec((tq, LANES), lambda b, h, i: (b * nq + i, h)),
                  pl.BlockSpec((seq, LANES), lambda b, h, i: (b, h)),
                  pl.BlockSpec((seq, LANES), lambda b, h, i: (b, h)),
                  vec, vec, vec, vec, _full(subln.shape)],
        out_specs=pl.BlockSpec((tq, LANES), lambda b, h, i: (b * nq + i, h)),
        out_shape=jax.ShapeDtypeStruct((batch * seq, DIFF_HEADS * LANES), BF16),
        scratch_shapes=[pltpu.VMEM((2, tq, LANES), F32), pltpu.VMEM((2, tq, 2 * LANES), F32)],
        compiler_params=_cparams(("parallel", "parallel", "arbitrary")),
        name="diff_prompt",
    )(q, k, v, *lams, subln)


def _diff_sample_kernel(lam_init, t_new, npg, pt_ref, q_ref, kn_ref, vn_ref, lq1_ref, lk1_ref, lq2_ref, lk2_ref,
                        sub_ref, *rest):
    k_refs, v_refs = rest[:npg], rest[npg:2 * npg]
    o_ref, m_ref, l_ref, acc_ref = rest[2 * npg:]
    g = pl.program_id(1)
    rows = q_ref.shape[1]
    grp = rows // DIFF_HEADS
    page = k_refs[0].shape[2]

    @pl.when(g == 0)
    def _():
        m_ref[...] = jnp.full(m_ref.shape, -jnp.inf, F32)
        l_ref[...] = jnp.zeros(l_ref.shape, F32)
        acc_ref[...] = jnp.zeros(acc_ref.shape, F32)

    q = q_ref[0]

    def attend(s, v_of_head):
        m_old = m_ref[...]
        m_new = jnp.maximum(m_old, jnp.max(s, axis=-1, keepdims=True))
        p = jnp.exp2(s - m_new)
        alpha = jnp.exp2(m_old - m_new)
        l_ref[...] = alpha * l_ref[...] + jnp.sum(p, axis=-1, keepdims=True)
        pv = [_dot(p[h * grp:(h + 1) * grp].astype(BF16), v_of_head(h)) for h in range(DIFF_HEADS)]
        acc_ref[...] = alpha * acc_ref[...] + jnp.concatenate(pv, axis=0)
        m_ref[...] = m_new

    kcat = jnp.concatenate([kr[0].astype(BF16) for kr in k_refs], axis=1)
    attend(_dot(q, kcat), lambda h: jnp.concatenate(
        [vr[0, pl.ds(h, page, stride=DIFF_HEADS), :].astype(BF16) for vr in v_refs], axis=0))

    @pl.when(g == pl.num_programs(1) - 1)
    def _():
        s_new = _dot(q, kn_ref[0])
        r = lax.broadcasted_iota(jnp.int32, s_new.shape, 0)
        c = lax.broadcasted_iota(jnp.int32, s_new.shape, 1)
        attend(jnp.where(c <= r % t_new, s_new, -jnp.inf), lambda h: vn_ref[0, h])
        lam = _lambda(lq1_ref, lk1_ref, lq2_ref, lk2_ref, lam_init)
        an = acc_ref[...] * (1.0 / l_ref[...])
        diff = an - lam * pltpu.roll(an, rows - t_new, 0)
        o_ref[0] = _rms(diff, sub_ref[...]) * (1.0 - lam_init)


def _diff_sample(page_table, q, k_new, v_new, lams, subln, cache_kt, cache_v, lam_init, t_new):
    batch, rows, _ = q.shape
    n_pages = page_table.shape[1]
    npg = _largest_divisor(n_pages, PAGES_PER_STEP)
    vec = pl.BlockSpec(lams[0].shape, lambda b, g, pt: (0, 0))
    per_batch = lambda a: pl.BlockSpec((1,) + a.shape[1:], lambda b, g, pt: (b,) + (0,) * (a.ndim - 1))

    def page_spec(a, j):
        return pl.BlockSpec((1,) + a.shape[1:], lambda b, g, pt: (pt[b * n_pages + g * npg + j], 0, 0))

    grid_spec = pltpu.PrefetchScalarGridSpec(
        num_scalar_prefetch=1,
        grid=(batch, n_pages // npg),
        in_specs=[per_batch(q), per_batch(k_new), per_batch(v_new), vec, vec, vec, vec,
                  pl.BlockSpec(subln.shape, lambda b, g, pt: (0, 0))]
        + [page_spec(cache_kt, j) for j in range(npg)] + [page_spec(cache_v, j) for j in range(npg)],
        out_specs=pl.BlockSpec((1, rows, LANES), lambda b, g, pt: (b, 0, 0)),
        scratch_shapes=[pltpu.VMEM((rows, 1), F32), pltpu.VMEM((rows, 1), F32), pltpu.VMEM((rows, LANES), F32)],
    )
    return pl.pallas_call(
        functools.partial(_diff_sample_kernel, lam_init, t_new, npg),
        grid_spec=grid_spec,
        out_shape=jax.ShapeDtypeStruct((batch, rows, LANES), F32),
        compiler_params=pltpu.CompilerParams(dimension_semantics=("parallel", "arbitrary"),
                                             vmem_limit_bytes=BIG_VMEM_LIMIT),
        name="diff_sample",
    )(page_table.reshape(-1), q, k_new, v_new, *lams, subln, *([cache_kt] * npg), *([cache_v] * npg))


def _merge_route_kernel(x_ref, og_ref, od_ref, sa_ref, sb_ref, wg_ref, wd_ref, wo_ref, gf_ref, wr_ref, br_ref,
                        cin_ref, *rest):
    h_ref, hn_ref, idx_ref, gate_ref, rank_ref, cnt_ref, carry_ref = rest[-7:]
    i = pl.program_id(0)
    tm = x_ref.shape[0]

    @pl.when(i == 0)
    def _():
        carry_ref[...] = cin_ref[...].astype(F32)

    sub = min(tm, ROUTE_SUB)
    tiles = hn_ref.shape[0] // tm
    hns = []
    for j in range(tm // sub):
        rows = pl.ds(j * sub, sub)
        merged = (sa_ref[rows, :] * _dot(og_ref[rows, :], wg_ref[...])
                  + sb_ref[rows, :] * _dot(od_ref[rows, :], wd_ref[...]))
        h = x_ref[rows, :] + _dot(merged.astype(BF16), wo_ref[...])
        h_ref[rows, :] = h
        hn = _rms(h, gf_ref[...]).astype(BF16)
        hn32 = hn.astype(F32)
        for s in range(tiles):
            hn_ref[pl.ds(j * sub * tiles + s, sub, stride=tiles), :] = hn32[:, s * LANES:(s + 1) * LANES]
        hns.append(hn)

    lane = lax.broadcasted_iota(jnp.int32, (sub, LANES), 1)
    row = lax.broadcasted_iota(jnp.int32, (sub, sub), 0)
    col = lax.broadcasted_iota(jnp.int32, (sub, sub), 1)
    strict = jnp.where(row > col, 1.0, 0.0).astype(BF16)
    n_sub = len(hns)
    logits = [jnp.where(lane < N_EXPERTS, _dot(hn, wr_ref[...]) + br_ref[...], -jnp.inf) for hn in hns]
    idx_out = [jnp.zeros((sub, LANES), jnp.int32) for _ in hns]
    val_out = [jnp.zeros((sub, LANES), F32) for _ in hns]
    onehot = [jnp.zeros((sub, LANES), F32) for _ in hns]
    picks = [[] for _ in hns]
    for kk in range(TOP_K):
        for j in range(n_sub):
            mx = jnp.max(logits[j], axis=-1, keepdims=True)
            pick = jnp.min(jnp.where(logits[j] == mx, lane, LANES), axis=-1, keepdims=True)
            hit = lane == pick
            logits[j] = jnp.where(hit, -jnp.inf, logits[j])
            onehot[j] = jnp.where(hit, 1.0, onehot[j])
            idx_out[j] = jnp.where(lane == kk, pick, idx_out[j])
            val_out[j] = jnp.where(lane == kk, mx, val_out[j])
            picks[j].append(hit)
    carry = carry_ref[...]
    for j in range(n_sub):
        rows = pl.ds(j * sub, sub)
        e = jnp.where(lane < TOP_K, jnp.exp(val_out[j] - val_out[j][:, :1]), 0.0)
        gate_ref[rows, :] = e * (1.0 / jnp.sum(e, axis=-1, keepdims=True))
        idx_ref[rows, :] = idx_out[j]
        before = _dot(strict, onehot[j].astype(BF16)) + carry
        rank_out = jnp.zeros((sub, LANES), jnp.int32)
        for kk in range(TOP_K):
            rk = jnp.sum(jnp.where(picks[j][kk], before, 0.0), axis=-1, keepdims=True)
            rank_out = jnp.where(lane == kk, rk.astype(jnp.int32), rank_out)
        rank_ref[rows, :] = rank_out
        carry = carry + jnp.sum(onehot[j], axis=0, keepdims=True)
    carry_ref[...] = carry
    cnt_ref[...] = carry.astype(jnp.int32)


def _merge_route(x, og, od, sa, sb, wg, wd, wo, gf, wr, br, counts_in, n_all, tok0, hn_all=None):
    n, d = x.shape
    tm = _row_tile(n, 512)
    assert tok0 % tm == 0
    b0 = tok0 // tm
    tiles = d // LANES
    row = lambda w: pl.BlockSpec((tm, w), lambda i: (i, 0))
    in_specs = [row(d)] * 5 + [_full(wg.shape), _full(wd.shape), _full(wo.shape), _full(gf.shape),
                               _full(wr.shape), _full(br.shape), _full(counts_in.shape)]
    args = [x, og, od, sa, sb, wg, wd, wo, gf, wr, br, counts_in]
    aliases = {}
    if hn_all is not None:
        in_specs.append(pl.BlockSpec(memory_space=pl.ANY))
        aliases = {len(args): 1}
        args.append(hn_all)
    return pl.pallas_call(
        _merge_route_kernel,
        grid=(n // tm,),
        in_specs=in_specs,
        out_specs=[row(d), pl.BlockSpec((tm * tiles, LANES), lambda i: (b0 + i, 0)),
                   row(LANES), row(LANES), row(LANES), _full((1, LANES))],
        out_shape=[jax.ShapeDtypeStruct((n, d), F32), jax.ShapeDtypeStruct((n_all * tiles, LANES), F32),
                   jax.ShapeDtypeStruct((n, LANES), jnp.int32), jax.ShapeDtypeStruct((n, LANES), F32),
                   jax.ShapeDtypeStruct((n, LANES), jnp.int32), jax.ShapeDtypeStruct((1, LANES), jnp.int32)],
        scratch_shapes=[pltpu.VMEM((1, LANES), F32)],
        input_output_aliases=aliases,
        compiler_params=_cparams(("arbitrary",)),
        name="merge_route",
    )(*args)


def _pair_shuffle(h):
    a, b = h[:, :LANES], h[:, LANES:]
    even = (lax.broadcasted_iota(jnp.int32, (1, LANES), 1) % 2) == 0
    glu = jnp.where(even, a, pltpu.roll(b, 1, 1))
    lin = jnp.where(even, pltpu.roll(a, LANES - 1, 1), b)
    return glu, lin


def _moe_kernel(be_ref, first_ref, nu_ref, xo_ref, x_ref, wu_ref, bu_ref, wd_ref, bd_ref, y_ref,
                xs_ref, wub_ref, wdb_ref, wds_ref):
    del xo_ref
    i = pl.program_id(0)
    bm, d = xs_ref.shape
    de = wdb_ref.shape[0]
    n_tiles = d // LANES
    half = LANES // 2

    @pl.when((i < nu_ref[0]) & (first_ref[i] == 1))
    def _():
        wub_ref[...] = wu_ref[0].astype(BF16)
        for t in range(n_tiles):
            lanes = slice(t * LANES, (t + 1) * LANES)
            for c in range(de // LANES):
                for a in range(2):
                    src = c * LANES + a * half
                    wds_ref[t, pl.ds(c * LANES + a, half, stride=2), :] = wd_ref[0, src:src + half, lanes]
            wdb_ref[:, lanes] = wds_ref[t].astype(BF16)

    @pl.when(i < nu_ref[0])
    def _():
        for s in range(n_tiles):
            xs_ref[:, s * LANES:(s + 1) * LANES] = x_ref[pl.ds(s, bm, stride=SUBLANES), :].astype(BF16)
        h = _dot(xs_ref[...], wub_ref[...]) + bu_ref[0]
        acts = []
        for c in range(h.shape[1] // (2 * LANES)):
            glu, lin = _pair_shuffle(h[:, 2 * c * LANES:2 * (c + 1) * LANES])
            x_glu = jnp.minimum(glu, SWIGLU_LIMIT)
            x_lin = jnp.clip(lin, -SWIGLU_LIMIT, SWIGLU_LIMIT)
            acts.append((x_glu * _sigmoid(SWIGLU_ALPHA * x_glu) * (x_lin + 1.0)).astype(BF16))
        y = _dot(jnp.concatenate(acts, axis=1), wdb_ref[...]) + bd_ref[0]
        for s in range(n_tiles):
            y_ref[pl.ds(s, bm, stride=SUBLANES), :] = y[:, s * LANES:(s + 1) * LANES]

    @pl.when(i >= nu_ref[0])
    def _():
        y_ref[...] = jnp.zeros(y_ref.shape, F32)


def _moe_experts(blk_exp, blk_first, n_used, x_off, xs, wu, bu, wd, bd, d, n_rows):
    tiles = d // LANES
    de2 = wu.shape[2]
    wspec = lambda a, b: pl.BlockSpec((1, a, b), lambda i, be, bf, nu, xo: (be[i], 0, 0))
    rows_in = pl.BlockSpec((pl.Element(MOE_ROWS * tiles), pl.Element(LANES)),
                           lambda i, be, bf, nu, xo: (pl.multiple_of(xo[i], SUBLANES), 0))
    rows = pl.BlockSpec((MOE_ROWS * tiles, LANES), lambda i, be, bf, nu, xo: (i, 0))
    grid_spec = pltpu.PrefetchScalarGridSpec(
        num_scalar_prefetch=4,
        grid=(n_rows // MOE_ROWS,),
        in_specs=[rows_in, wspec(d, de2), wspec(1, de2), wspec(de2 // 2, d), wspec(1, d)],
        out_specs=rows,
        scratch_shapes=[pltpu.VMEM((MOE_ROWS, d), BF16), pltpu.VMEM((d, de2), BF16),
                        pltpu.VMEM((de2 // 2, d), BF16), pltpu.VMEM((d // LANES, de2 // 2, LANES), F32)],
    )
    return pl.pallas_call(
        _moe_kernel,
        grid_spec=grid_spec,
        out_shape=jax.ShapeDtypeStruct((n_rows * tiles, LANES), F32),
        compiler_params=pltpu.CompilerParams(dimension_semantics=("arbitrary",), vmem_limit_bytes=BIG_VMEM_LIMIT),
        name="moe_experts",
    )(blk_exp, blk_first, n_used, x_off, xs, wu, bu, wd, bd)


def _combine_kernel(h_ref, gate_ref, y0_ref, y1_ref, y2_ref, y3_ref, gf_ref, o_ref):
    tm, d = h_ref.shape
    gate = gate_ref[...]
    parts = []
    for s in range(d // LANES):
        acc = h_ref[:, s * LANES:(s + 1) * LANES]
        for kk, y_ref in enumerate((y0_ref, y1_ref, y2_ref, y3_ref)):
            acc = acc + gate[:, kk:kk + 1] * y_ref[0, pl.ds(s, tm, stride=SUBLANES), :]
        parts.append(acc)
    o_ref[...] = _rms(jnp.concatenate(parts, axis=1), gf_ref[...])


def _combine(h, gate, yg, gf, tok0):
    n, d = h.shape
    tm = _row_tile(n, 256)
    assert tok0 % tm == 0
    b0 = tok0 // tm
    tiles = d // LANES
    row = lambda w: pl.BlockSpec((tm, w), lambda i: (i, 0))
    ysp = lambda kk: pl.BlockSpec((1, tm * tiles, LANES), lambda i: (kk, b0 + i, 0))
    return pl.pallas_call(
        _combine_kernel,
        grid=(n // tm,),
        in_specs=[row(d), row(LANES)] + [ysp(kk) for kk in range(TOP_K)] + [_full(gf.shape)],
        out_specs=row(d),
        out_shape=jax.ShapeDtypeStruct((n, d), F32),
        compiler_params=_cparams(("parallel",)),
        name="combine",
    )(h, gate, yg, yg, yg, yg, gf)


def _rope_tables(pos):
    half = DIFF_DH // 2
    inv_freq = ROPE_THETA ** (-jnp.arange(half, dtype=F32) / half)
    ang = pos.astype(F32)[:, None] * inv_freq[None, :]
    cos, sin = jnp.cos(ang), jnp.sin(ang)
    reps = LANES // DIFF_DH
    return jnp.tile(jnp.concatenate([cos, cos], axis=1), (1, reps)), jnp.tile(jnp.concatenate([-sin, sin], axis=1), (1, reps))


def kernel(x_prompt, x_sample, cache_k, cache_v, state_gla, page_table, norm_attn, w_in, gla_gate_up, gla_gate_bias, gla_norm, lam_q1, lam_k1, lam_q2, lam_k2, diff_subln, w_branch_gla, w_branch_diff, w_out, norm_ffn, w_router, b_router, w_up, b_up, w_down, b_down, norm_final):
    depth = w_in.shape[0]
    assert depth == 1, "single-layer step"
    layer = 0
    b_p, t_p, d = x_prompt.shape
    b_s, t_s, _ = x_sample.shape
    n_p, n_s = b_p * t_p, b_s * t_s
    n = n_p + n_s
    n_pool, page = cache_k.shape[1], cache_k.shape[2]
    n_pages = page_table.shape[1]
    past_len = n_pages * page
    kd, vd = GLA_HEADS * GLA_DK, GLA_HEADS * GLA_DV
    tiles = d // LANES
    row2 = lambda a: a.reshape(1, -1)
    assert t_s <= LANES

    splits = (kd, kd, vd, GLA_RANK, vd, d, d, d, d, d)
    offs = [0]
    for s in splits:
        offs.append(offs[-1] + s)
    wb = w_in.reshape(d, -1).astype(BF16)
    w_gq, w_gk, w_gv, w_glr, w_gr, w_dq, w_dk, w_dv, w_ga, w_gb = [wb[:, offs[j]:offs[j + 1]] for j in range(10)]
    w_glr = jnp.pad(w_glr, ((0, 0), (0, LANES - GLA_RANK)))
    gup = jnp.pad(gla_gate_up.reshape(GLA_RANK, kd).astype(BF16), ((0, LANES - GLA_RANK), (0, 0)))
    g_attn = row2(norm_attn)
    gn = row2(gla_norm)
    lam_init = 0.8 - 0.6 * math.exp(-0.3 * layer)
    lams = (row2(lam_q1), row2(lam_k1), row2(lam_q2), row2(lam_k2))
    subln = row2(diff_subln)
    w_bg = w_branch_gla.reshape(vd, d).astype(BF16)
    w_bd = w_branch_diff.reshape(d, d).astype(BF16)
    w_o = w_out.reshape(d, d).astype(BF16)
    w_r = jnp.pad(w_router.reshape(d, N_EXPERTS).astype(BF16), ((0, 0), (0, LANES - N_EXPERTS)))
    b_r = jnp.pad(row2(b_router), ((0, 0), (0, LANES - N_EXPERTS)))
    g_ffn = row2(norm_ffn)

    def mixers(x, pos, seq):
        gq, gk, gv, gla_la, gr = _gla_proj(x, g_attn, w_gq, w_gk, w_gv, w_glr, w_gr, gup, row2(gla_gate_bias))
        cos, sin_signed = _rope_tables(pos)
        return (gq, gk, gv, gla_la, gr) + tuple(
            _diff_proj(x, g_attn, cos, sin_signed, w_dq, w_dk, w_dv, w_ga, w_gb, seq))

    x_p = x_prompt.reshape(n_p, d)
    gq, gk, gv, gla_la, gr, dq, dk_b, dv_b, kt_p, v_cache_p, sig_a, sig_b = mixers(
        x_p, jnp.arange(t_p, dtype=jnp.int32), t_p)
    seqs = lambda a: a.reshape(b_p, t_p, -1)
    og_p, s_p = _gla_chunks(seqs(gq), seqs(gk), seqs(gv), seqs(gla_la), seqs(gr), gn, None,
                            GLA_CHUNK if t_p % GLA_CHUNK == 0 else t_p, _largest_divisor(b_p, GLA_PROMPT_SEQS))
    og_p = og_p.reshape(n_p, vd)
    od_p = _diff_prompt(dq, dk_b, dv_b, lams, subln, b_p, t_p, lam_init)
    zero_counts = jnp.zeros((1, LANES), jnp.int32)
    h_p, hn_p, idx_p, gate_p, rank_p, counts_p = _merge_route(
        x_p, og_p, od_p, sig_a, sig_b, w_bg, w_bd, w_o, g_ffn, w_r, b_r, zero_counts, n, 0)

    x_s = x_sample.reshape(n_s, d)
    gq, gk, gv, gla_la, gr, dq, dk_b, dv_b, kt_s, v_cache_s, sig_a, sig_b = mixers(
        x_s, jnp.tile(past_len + jnp.arange(t_s, dtype=jnp.int32), b_s), n_s)
    t_pad = -(-t_s // GLA_SAMPLE_CHUNK) * GLA_SAMPLE_CHUNK
    pad_s = lambda a: jnp.pad(a.reshape(b_s, t_s, -1), ((0, 0), (0, t_pad - t_s), (0, 0)))
    og_s, s_s = _gla_chunks(pad_s(gq), pad_s(gk), pad_s(gv), pad_s(gla_la), pad_s(gr), gn,
                            state_gla.reshape(b_s, GLA_HEADS, GLA_DK, GLA_DV), t_pad,
                            _largest_divisor(b_s, GLA_SAMPLE_SEQS))
    og_s = og_s[:, :t_s].reshape(n_s, vd)
    q4 = dq.reshape(b_s, t_s, DIFF_QK_HEADS, DIFF_DH).transpose(0, 2, 1, 3)
    q_bd = (q4[:, :, :, None, :] * jnp.eye(DIFF_QK_HEADS, dtype=BF16)[None, :, None, :, None]).reshape(
        b_s, DIFF_QK_HEADS * t_s, d)
    kn = jnp.pad(dk_b.reshape(b_s, t_s, d).transpose(0, 2, 1), ((0, 0), (0, 0), (0, LANES - t_s)))
    vn = jnp.pad(dv_b.reshape(b_s, t_s, DIFF_HEADS, LANES).transpose(0, 2, 1, 3),
                 ((0, 0), (0, 0), (0, LANES - t_s), (0, 0)))
    cache_kt = jnp.transpose(cache_k, (0, 1, 3, 4, 2)).reshape(depth * n_pool, DIFF_QK_HEADS * DIFF_DH, page)
    od_s = _diff_sample(page_table, q_bd, kn, vn, lams, subln, cache_kt,
                        cache_v.reshape(depth * n_pool, page * DIFF_HEADS, LANES), lam_init, t_s)
    od_s = od_s.reshape(b_s, DIFF_HEADS, 2, t_s, LANES)[:, :, 0].transpose(0, 2, 1, 3).reshape(n_s, d).astype(BF16)
    h_s, hn_all, idx_s, gate_s, rank_s, counts = _merge_route(
        x_s, og_s, od_s, sig_a, sig_b, w_bg, w_bd, w_o, g_ffn, w_r, b_r, counts_p, n, n_p, hn_p)

    counts = counts[0, :N_EXPERTS]
    padded = (counts + MOE_ROWS - 1) // MOE_ROWS * MOE_ROWS
    pad_end = jnp.cumsum(padded)
    pad_start = pad_end - padded
    idx4 = jnp.concatenate([idx_p[:, :TOP_K], idx_s[:, :TOP_K]], axis=0)
    rank4 = jnp.concatenate([rank_p[:, :TOP_K], rank_s[:, :TOP_K]], axis=0)
    experts = jnp.arange(N_EXPERTS, dtype=jnp.int32)

    def lookup(table, e):
        return jnp.sum(jnp.where(e[..., None] == experts, table, 0), axis=-1)

    dest = lookup(pad_start, idx4) + rank4
    n_rows = -(-(n * TOP_K + N_EXPERTS * (MOE_ROWS - 1)) // MOE_ROWS) * MOE_ROWS
    n_blocks = n_rows // MOE_ROWS
    blk_start = jnp.arange(n_blocks, dtype=jnp.int32) * MOE_ROWS
    blk_exp = jnp.minimum(jnp.sum((blk_start[:, None] >= pad_end[None, :]).astype(jnp.int32), axis=1), N_EXPERTS - 1)
    n_used = (pad_end[-1:] // MOE_ROWS).astype(jnp.int32)
    start = jnp.cumsum(counts) - counts
    _, sorted_tok = lax.sort(((lookup(start, idx4) + rank4).reshape(-1),
                              jnp.repeat(jnp.arange(n, dtype=jnp.int32), TOP_K)), num_keys=1, is_stable=False)
    src_tok = jnp.concatenate([sorted_tok, jnp.arange(MOE_ROWS, dtype=jnp.int32) % n])
    xs = jnp.take(hn_all.reshape(n, tiles, LANES), src_tok, axis=0, mode='clip').reshape(-1, LANES)
    blk_in_exp = blk_start - lookup(pad_start, blk_exp)
    x_off = jnp.clip(lookup(start, blk_exp) + blk_in_exp, 0, n * TOP_K).astype(jnp.int32) * tiles

    de = w_down.shape[2]
    blk_first = jnp.concatenate([jnp.ones((1,), jnp.int32), (blk_exp[1:] != blk_exp[:-1]).astype(jnp.int32)])
    y = _moe_experts(blk_exp, blk_first, n_used, x_off, xs, w_up.reshape(N_EXPERTS, d, 2 * de),
                     b_up.reshape(N_EXPERTS, 1, 2 * de), w_down.reshape(N_EXPERTS, de, d),
                     b_down.reshape(N_EXPERTS, 1, d), d, n_rows)
    yg = jnp.take(y.reshape(n_rows, tiles, LANES), dest.T.reshape(-1), axis=0, mode='clip').reshape(TOP_K, n * tiles, LANES)

    g_fin = row2(norm_final)
    y_prompt = _combine(h_p, gate_p, yg, g_fin, 0).reshape(b_p, t_p, d)
    y_sample = _combine(h_s, gate_s, yg, g_fin, n_p).reshape(b_s, t_s, d)
    k_prompt = kt_p.reshape(1, b_p, DIFF_QK_HEADS, DIFF_DH, t_p).transpose(0, 1, 4, 2, 3)
    v_prompt = v_cache_p.reshape(1, b_p, t_p, DIFF_HEADS, 2 * DIFF_DH)
    k_sample = kt_s.reshape(1, DIFF_QK_HEADS, DIFF_DH, b_s, t_s).transpose(0, 3, 4, 1, 2)
    v_sample = v_cache_s.reshape(1, b_s, t_s, DIFF_HEADS, 2 * DIFF_DH)
    return (y_prompt, y_sample, k_prompt, v_prompt, s_p[None], k_sample, v_sample, s_s[None])
```

```python
import functools
import math

import jax
import jax.numpy as jnp
from jax import lax
from jax.experimental import pallas as pl
from jax.experimental.pallas import tpu as pltpu

F32 = jnp.float32
BF16 = jnp.bfloat16

EPS = 1e-6
GLA_HEADS = 4
GLA_DK = 128
GLA_DV = 256
GLA_RANK = 16
GLA_NORMALIZER = 16.0
GLA_CHUNK = 64
GLA_SAMPLE_CHUNK = 16
GLA_PROMPT_SEQS = 4
GLA_SAMPLE_SEQS = 8
DIFF_HEADS = 8
DIFF_QK_HEADS = 2 * DIFF_HEADS
DIFF_DH = 64
ROPE_THETA = 10000.0
N_EXPERTS = 32
TOP_K = 4
SWIGLU_ALPHA = 1.702
SWIGLU_LIMIT = 7.0
LOG2E = math.log2(math.e)

LANES = 128
SUBLANES = 8
ROUTE_SUB = 256
MOE_ROWS = 512
ATTN_Q = 2048
ATTN_K = 512
ATTN_SUB = 256
ATTN_LOOKAHEAD = 3
PAGES_PER_STEP = 16
VMEM_LIMIT = 48 * 1024 * 1024
BIG_VMEM_LIMIT = 56 * 1024 * 1024


def _cparams(sem):
    return pltpu.CompilerParams(dimension_semantics=sem, vmem_limit_bytes=VMEM_LIMIT)


def _dot(a, b):
    return jnp.dot(a, b, preferred_element_type=F32)


def _dot_nt(a, b):
    return lax.dot_general(a, b, (((1,), (1,)), ((), ())), preferred_element_type=F32)


def _dot_tn(a, b):
    return lax.dot_general(a, b, (((0,), (0,)), ((), ())), preferred_element_type=F32)


def _rms(x, g):
    ms = jnp.mean(x * x, axis=-1, keepdims=True)
    return x * lax.rsqrt(ms + EPS) * g


def _sigmoid(x):
    return 1.0 / (1.0 + jnp.exp(-x))


def _row_tile(n, cap=512):
    for t in (512, 256, 128, 64, 32, 16, 8):
        if t <= cap and n % t == 0:
            return t
    raise ValueError(f"token count {n} is not a multiple of 8")


def _largest_divisor(n, cap):
    return max(t for t in range(1, cap + 1) if n % t == 0)


def _full(shape):
    return pl.BlockSpec(shape, lambda *_: (0,) * len(shape), pipeline_mode=pl.Buffered(1))


def _gla_proj_kernel(x_ref, g_ref, wq_ref, wk_ref, wv_ref, wlr_ref, wr_ref, gup_ref, gb_ref,
                     q_ref, k_ref, v_ref, la_ref, r_ref):
    xn = _rms(x_ref[...], g_ref[...]).astype(BF16)
    q_ref[...] = _dot(xn, wq_ref[...]) * (GLA_DK ** -0.5)
    k_ref[...] = _dot(xn, wk_ref[...])
    v_ref[...] = _dot(xn, wv_ref[...]).astype(BF16)
    glr = _dot(xn, wlr_ref[...]).astype(BF16)
    z = _dot(glr, gup_ref[...]) + gb_ref[...]
    log_sig = jnp.minimum(z, 0.0) - jnp.log1p(jnp.exp(-jnp.abs(z)))
    la_ref[...] = log_sig * (1.0 / GLA_NORMALIZER)
    r_ref[...] = _dot(xn, wr_ref[...])


def _gla_proj(x, g, wq, wk, wv, wlr, wr, gup, gb):
    n, d = x.shape
    tm = _row_tile(n)
    kd, vd = wq.shape[1], wv.shape[1]
    row = lambda w: pl.BlockSpec((tm, w), lambda i: (i, 0))
    return pl.pallas_call(
        _gla_proj_kernel,
        grid=(n // tm,),
        in_specs=[row(d), _full(g.shape), _full(wq.shape), _full(wk.shape), _full(wv.shape),
                  _full(wlr.shape), _full(wr.shape), _full(gup.shape), _full(gb.shape)],
        out_specs=[row(kd), row(kd), row(vd), row(kd), row(vd)],
        out_shape=[jax.ShapeDtypeStruct((n, kd), F32), jax.ShapeDtypeStruct((n, kd), F32),
                   jax.ShapeDtypeStruct((n, vd), BF16), jax.ShapeDtypeStruct((n, kd), F32),
                   jax.ShapeDtypeStruct((n, vd), F32)],
        compiler_params=_cparams(("parallel",)),
        name="gla_proj",
    )(x, g, wq, wk, wv, wlr, wr, gup, gb)


def _rope_chunk(xc, cos, sin_signed, first_half):
    swapped = jnp.where(first_half, pltpu.roll(xc, LANES - DIFF_DH // 2, 1), pltpu.roll(xc, DIFF_DH // 2, 1))
    return xc * cos + swapped * sin_signed


def _diff_proj_kernel(x_ref, g_ref, cos_ref, sin_ref, wq_ref, wk_ref, wv_ref, wa_ref, wb_ref,
                      q_ref, kb_ref, vb_ref, kt_ref, vc_ref, sa_ref, sb_ref):
    tm = x_ref.shape[0]
    xn = _rms(x_ref[...], g_ref[...]).astype(BF16)
    cos = cos_ref[...]
    sin_signed = sin_ref[...]
    lane = lax.broadcasted_iota(jnp.int32, (1, LANES), 1)
    first_half = (lane % DIFF_DH) < (DIFF_DH // 2)
    n_chunks = q_ref.shape[1] // LANES
    q = _dot(xn, wq_ref[...])
    for c in range(n_chunks):
        sl = slice(c * LANES, (c + 1) * LANES)
        q_ref[:, sl] = (_rope_chunk(q[:, sl], cos, sin_signed, first_half) * (DIFF_DH ** -0.5 * LOG2E)).astype(BF16)
    k = _dot(xn, wk_ref[...])
    for c in range(n_chunks):
        sl = slice(c * LANES, (c + 1) * LANES)
        kr = _rope_chunk(k[:, sl], cos, sin_signed, first_half)
        kb_ref[:, sl] = kr.astype(BF16)
        kt_ref[0, sl, :] = kr.T
    v = _dot(xn, wv_ref[...])
    vb_ref[...] = v.astype(BF16)
    for c in range(n_chunks):
        vc_ref[pl.ds(c, tm, stride=DIFF_HEADS), :] = v[:, c * LANES:(c + 1) * LANES]
    sa_ref[...] = _sigmoid(_dot(xn, wa_ref[...]))
    sb_ref[...] = _sigmoid(_dot(xn, wb_ref[...]))


def _diff_proj(x, g, cos, sin_signed, wq, wk, wv, wa, wb, seq):
    n, d = x.shape
    tm = _row_tile(seq, 512)
    tps = seq // tm
    row = lambda w: pl.BlockSpec((tm, w), lambda i: (i, 0))
    pos_row = pl.BlockSpec((tm, LANES), lambda i: (i % (cos.shape[0] // tm), 0))
    sds = lambda dt: jax.ShapeDtypeStruct((n, d), dt)
    return pl.pallas_call(
        _diff_proj_kernel,
        grid=(n // tm,),
        in_specs=[row(d), _full(g.shape), pos_row, pos_row] + [_full(wq.shape)] * 5,
        out_specs=[row(d), row(d), row(d),
                   pl.BlockSpec((1, d, tm), lambda i: (i // tps, 0, i % tps)),
                   pl.BlockSpec((tm * DIFF_HEADS, LANES), lambda i: (i, 0)),
                   row(d), row(d)],
        out_shape=[sds(BF16), sds(BF16), sds(BF16),
                   jax.ShapeDtypeStruct((n // seq, d, seq), F32),
                   jax.ShapeDtypeStruct((n * DIFF_HEADS, LANES), F32),
                   sds(F32), sds(F32)],
        compiler_params=_cparams(("parallel",)),
        name="diff_proj",
    )(x, g, cos, sin_signed, wq, wk, wv, wa, wb)


def _gla_kernel(has_s0, *refs):
    if has_s0:
        q_ref, k_ref, v_ref, la_ref, r_ref, gn_ref, s0_ref, o_ref, sfin_ref, st_ref = refs
    else:
        q_ref, k_ref, v_ref, la_ref, r_ref, gn_ref, o_ref, sfin_ref, st_ref = refs
    n = pl.program_id(1)
    nb, c, _ = q_ref.shape
    pairs = [(b, h) for b in range(nb) for h in range(GLA_HEADS)]
    slot = lambda b, h: b * GLA_HEADS + h

    @pl.when(n == 0)
    def _():
        for b, h in pairs:
            if has_s0:
                st_ref[slot(b, h)] = s0_ref[b, h]
            else:
                st_ref[slot(b, h)] = jnp.zeros(st_ref.shape[1:], F32)

    row = lax.broadcasted_iota(jnp.int32, (c, c), 0)
    col = lax.broadcasted_iota(jnp.int32, (c, c), 1)
    causal = row >= col
    tri = jnp.where(causal, 1.0, 0.0).astype(BF16)
    gn = gn_ref[...]
    ks = lambda h: slice(h * GLA_DK, (h + 1) * GLA_DK)
    vs = lambda h: slice(h * GLA_DV, (h + 1) * GLA_DV)
    ones = jnp.ones((c, GLA_DK), BF16)
    bcum, decay = [], []
    for b, h in pairs:
        la = la_ref[b, :, ks(h)]
        hi = la.astype(BF16)
        rem = la - hi.astype(F32)
        mid = rem.astype(BF16)
        lo = (rem - mid.astype(F32)).astype(BF16)
        bcum.append(_dot(tri, hi) + _dot(tri, mid) + _dot(tri, lo))
        total = _dot_tn(hi, ones) + _dot_tn(mid, ones) + _dot_tn(lo, ones)
        decay.append(jnp.exp(jnp.concatenate([total] * (GLA_DV // GLA_DK), axis=1)))
    idx = range(len(pairs))
    q_t = [(q_ref[b, :, ks(h)] * jnp.exp(bcum[i])).astype(BF16) for i, (b, h) in enumerate(pairs)]
    k_t = [(k_ref[b, :, ks(h)] * jnp.exp(-bcum[i])).astype(BF16) for i, (b, h) in enumerate(pairs)]
    k_dec = [(k_ref[b, :, ks(h)] * jnp.exp(bcum[i][c - 1:c, :] - bcum[i])).astype(BF16)
             for i, (b, h) in enumerate(pairs)]
    st = [st_ref[slot(b, h)] for b, h in pairs]
    scores = [jnp.where(causal, _dot_nt(q_t[i], k_t[i]), 0.0).astype(BF16) for i in idx]
    o_state = [_dot(q_t[i], st[i].astype(BF16)) for i in idx]
    upd = [_dot_tn(k_dec[i], v_ref[b, :, vs(h)]) for i, (b, h) in enumerate(pairs)]
    o_intra = [_dot(scores[i], v_ref[b, :, vs(h)]) for i, (b, h) in enumerate(pairs)]
    for i, (b, h) in enumerate(pairs):
        st_ref[slot(b, h)] = st[i] * decay[i] + upd[i]
        rh = r_ref[b, :, vs(h)]
        o_ref[b, :, vs(h)] = (_rms(o_state[i] + o_intra[i], gn) * (rh * _sigmoid(rh))).astype(BF16)

    @pl.when(n == pl.num_programs(1) - 1)
    def _():
        for b, h in pairs:
            sfin_ref[b, h] = st_ref[slot(b, h)]


def _gla_chunks(q, k, v, la, r, gn, s0, chunk, nb):
    batch, seq, kd = q.shape
    vd = v.shape[2]
    nc = seq // chunk
    blk = lambda w: pl.BlockSpec((nb, chunk, w), lambda b, n: (b, n, 0))
    state_spec = pl.BlockSpec((nb, GLA_HEADS, GLA_DK, GLA_DV), lambda b, n: (b, 0, 0, 0))
    in_specs = [blk(kd), blk(kd), blk(vd), blk(kd), blk(vd), _full(gn.shape)]
    args = [q, k, v, la, r, gn]
    if s0 is not None:
        in_specs.append(state_spec)
        args.append(s0)
    return pl.pallas_call(
        functools.partial(_gla_kernel, s0 is not None),
        grid=(batch // nb, nc),
        in_specs=in_specs,
        out_specs=[blk(vd), state_spec],
        out_shape=[jax.ShapeDtypeStruct((batch, seq, vd), BF16),
                   jax.ShapeDtypeStruct((batch, GLA_HEADS, GLA_DK, GLA_DV), F32)],
        scratch_shapes=[pltpu.VMEM((nb * GLA_HEADS, GLA_DK, GLA_DV), F32)],
        compiler_params=_cparams(("parallel", "arbitrary")),
        name="gla_chunks",
    )(*args)


def _lambda(lq1_ref, lk1_ref, lq2_ref, lk2_ref, lam_init):
    s1 = jnp.sum(lq1_ref[...] * lk1_ref[...], axis=-1, keepdims=True)
    s2 = jnp.sum(lq2_ref[...] * lk2_ref[...], axis=-1, keepdims=True)
    return jnp.exp(s1) - jnp.exp(s2) + lam_init


def _diff_prompt_kernel(lam_init, tk, sub, q_ref, k_ref, v_ref, lq1_ref, lk1_ref, lq2_ref, lk2_ref, sub_ref,
                        o_ref, m_ref, a_ref):
    qi = pl.program_id(2)
    tq = q_ref.shape[0]
    lane = lax.broadcasted_iota(jnp.int32, (1, LANES), 1)
    m_ref[...] = jnp.full(m_ref.shape, -jnp.inf, F32)
    a_ref[...] = jnp.zeros(a_ref.shape, F32)

    def scores(mp, r0, k0, nk, mask):
        q = q_ref[pl.ds(r0, sub), :]
        qm = jnp.where((lane < DIFF_DH) if mp == 0 else (lane >= DIFF_DH), q, jnp.zeros_like(q))
        s = _dot_nt(qm, k_ref[pl.ds(k0, nk), :])
        return s if mask is None else jnp.where(mask, s, -jnp.inf)

    def update(mp, r0, k0, nk, s):
        rows = pl.ds(r0, sub)
        n_chunks = nk // LANES
        mx = s[:, :LANES]
        for c in range(1, n_chunks):
            mx = jnp.maximum(mx, s[:, c * LANES:(c + 1) * LANES])
        m_old = m_ref[mp, rows, :]
        m_new = jnp.maximum(m_old, jnp.max(mx, axis=-1, keepdims=True))
        alpha = jnp.exp2(m_old - m_new)
        p = jnp.concatenate(
            [jnp.exp2(s[:, c * LANES:(c + 1) * LANES] - m_new).astype(BF16) for c in range(n_chunks)], axis=1)
        m_ref[mp, rows, :] = m_new
        vj = v_ref[pl.ds(k0, nk), :]
        v_ones = jnp.concatenate([vj, jnp.ones_like(vj)], axis=1)
        a_ref[mp, rows, :] = jnp.concatenate([alpha, alpha], axis=1) * a_ref[mp, rows, :] + _dot(p, v_ones)

    def run(chains):
        pending = {}
        for i in range(min(ATTN_LOOKAHEAD, len(chains))):
            pending[i] = scores(*chains[i])
        for i, (mp, r0, k0, nk, _) in enumerate(chains):
            if i + ATTN_LOOKAHEAD < len(chains):
                pending[i + ATTN_LOOKAHEAD] = scores(*chains[i + ATTN_LOOKAHEAD])
            update(mp, r0, k0, nk, pending.pop(i))

    n_sub = tq // sub

    def body(j, carry):
        k0 = pl.multiple_of(j * tk, tk)
        run([(mp, si * sub, k0, tk, None) for si in range(n_sub) for mp in range(2)])
        return carry

    lax.fori_loop(0, qi * (tq // tk), body, 0)

    base = pl.multiple_of(qi * tq, tq)
    chains = []
    for si in range(n_sub):
        need = (si + 1) * sub
        for k0 in range(0, need, tk):
            nk = min(tk, need - k0)
            mask = None
            if k0 + nk == need:
                row = lax.broadcasted_iota(jnp.int32, (sub, nk), 0) + si * sub
                col = lax.broadcasted_iota(jnp.int32, (sub, nk), 1) + k0
                mask = row >= col
            chains += [(mp, si * sub, base + k0, nk, mask) for mp in range(2)]
    run(chains)

    lam = _lambda(lq1_ref, lk1_ref, lq2_ref, lk2_ref, lam_init)
    o1 = a_ref[0, :, :LANES] * (1.0 / a_ref[0, :, LANES:])
    o2 = a_ref[1, :, :LANES] * (1.0 / a_ref[1, :, LANES:])
    o_ref[...] = (_rms(o1 - lam * o2, sub_ref[...]) * (1.0 - lam_init)).astype(BF16)


def _diff_prompt(q, k, v, lams, subln, batch, seq, lam_init):
    tq = min(ATTN_Q, seq)
    tk = min(ATTN_K, tq)
    sub = min(ATTN_SUB, tk)
    nq = seq // tq
    vec = _full(lams[0].shape)
    return pl.pallas_call(
        functools.partial(_diff_prompt_kernel, lam_init, tk, sub),
        grid=(batch, DIFF_HEADS, nq),
        in_specs=[pl.BlockSpec((tq, LANES), lambda b, h, i: (b * nq + i, h)),
                  pl.BlockSpec((seq, LANES), lambda b, h, i: (b, h)),
                  pl.BlockSpec((seq, LANES), lambda b, h, i: (b, h)),
                  vec, vec, vec, vec, _full(subln.shape)],
        out_specs=pl.BlockSpec((tq, LANES), lambda b, h, i: (b * nq + i, h)),
        out_shape=jax.ShapeDtypeStruct((batch * seq, DIFF_HEADS * LANES), BF16),
        scratch_shapes=[pltpu.VMEM((2, tq, LANES), F32), pltpu.VMEM((2, tq, 2 * LANES), F32)],
        compiler_params=_cparams(("parallel", "parallel", "arbitrary")),
        name="diff_prompt",
    )(q, k, v, *lams, subln)


def _diff_sample_kernel(lam_init, t_new, npg, pt_ref, q_ref, kn_ref, vn_ref, lq1_ref, lk1_ref, lq2_ref, lk2_ref,
                        sub_ref, *rest):
    k_refs, v_refs = rest[:npg], rest[npg:2 * npg]
    o_ref, m_ref, l_ref, acc_ref = rest[2 * npg:]
    g = pl.program_id(1)
    rows = q_ref.shape[1]
    grp = rows // DIFF_HEADS
    page = k_refs[0].shape[2]

    @pl.when(g == 0)
    def _():
        m_ref[...] = jnp.full(m_ref.shape, -jnp.inf, F32)
        l_ref[...] = jnp.zeros(l_ref.shape, F32)
        acc_ref[...] = jnp.zeros(acc_ref.shape, F32)

    q = q_ref[0]

    def attend(s, v_of_head):
        m_old = m_ref[...]
        m_new = jnp.maximum(m_old, jnp.max(s, axis=-1, keepdims=True))
        p = jnp.exp2(s - m_new)
        alpha = jnp.exp2(m_old - m_new)
        l_ref[...] = alpha * l_ref[...] + jnp.sum(p, axis=-1, keepdims=True)
        pv = [_dot(p[h * grp:(h + 1) * grp].astype(BF16), v_of_head(h)) for h in range(DIFF_HEADS)]
        acc_ref[...] = alpha * acc_ref[...] + jnp.concatenate(pv, axis=0)
        m_ref[...] = m_new

    kcat = jnp.concatenate([kr[0].astype(BF16) for kr in k_refs], axis=1)
    attend(_dot(q, kcat), lambda h: jnp.concatenate(
        [vr[0, pl.ds(h, page, stride=DIFF_HEADS), :].astype(BF16) for vr in v_refs], axis=0))

    @pl.when(g == pl.num_programs(1) - 1)
    def _():
        s_new = _dot(q, kn_ref[0])
        r = lax.broadcasted_iota(jnp.int32, s_new.shape, 0)
        c = lax.broadcasted_iota(jnp.int32, s_new.shape, 1)
        attend(jnp.where(c <= r % t_new, s_new, -jnp.inf), lambda h: vn_ref[0, h])
        lam = _lambda(lq1_ref, lk1_ref, lq2_ref, lk2_ref, lam_init)
        an = acc_ref[...] * (1.0 / l_ref[...])
        diff = an - lam * pltpu.roll(an, rows - t_new, 0)
        o_ref[0] = _rms(diff, sub_ref[...]) * (1.0 - lam_init)


def _diff_sample(page_table, q, k_new, v_new, lams, subln, cache_kt, cache_v, lam_init, t_new):
    batch, rows, _ = q.shape
    n_pages = page_table.shape[1]
    npg = _largest_divisor(n_pages, PAGES_PER_STEP)
    vec = pl.BlockSpec(lams[0].shape, lambda b, g, pt: (0, 0))
    per_batch = lambda a: pl.BlockSpec((1,) + a.shape[1:], lambda b, g, pt: (b,) + (0,) * (a.ndim - 1))

    def page_spec(a, j):
        return pl.BlockSpec((1,) + a.shape[1:], lambda b, g, pt: (pt[b * n_pages + g * npg + j], 0, 0))

    grid_spec = pltpu.PrefetchScalarGridSpec(
        num_scalar_prefetch=1,
        grid=(batch, n_pages // npg),
        in_specs=[per_batch(q), per_batch(k_new), per_batch(v_new), vec, vec, vec, vec,
                  pl.BlockSpec(subln.shape, lambda b, g, pt: (0, 0))]
        + [page_spec(cache_kt, j) for j in range(npg)] + [page_spec(cache_v, j) for j in range(npg)],
        out_specs=pl.BlockSpec((1, rows, LANES), lambda b, g, pt: (b, 0, 0)),
        scratch_shapes=[pltpu.VMEM((rows, 1), F32), pltpu.VMEM((rows, 1), F32), pltpu.VMEM((rows, LANES), F32)],
    )
    return pl.pallas_call(
        functools.partial(_diff_sample_kernel, lam_init, t_new, npg),
        grid_spec=grid_spec,
        out_shape=jax.ShapeDtypeStruct((batch, rows, LANES), F32),
        compiler_params=pltpu.CompilerParams(dimension_semantics=("parallel", "arbitrary"),
                                             vmem_limit_bytes=BIG_VMEM_LIMIT),
        name="diff_sample",
    )(page_table.reshape(-1), q, k_new, v_new, *lams, subln, *([cache_kt] * npg), *([cache_v] * npg))


def _merge_route_kernel(x_ref, og_ref, od_ref, sa_ref, sb_ref, wg_ref, wd_ref, wo_ref, gf_ref, wr_ref, br_ref,
                        cin_ref, *rest):
    h_ref, hn_ref, idx_ref, gate_ref, rank_ref, cnt_ref, carry_ref = rest[-7:]
    i = pl.program_id(0)
    tm = x_ref.shape[0]

    @pl.when(i == 0)
    def _():
        carry_ref[...] = cin_ref[...].astype(F32)

    sub = min(tm, ROUTE_SUB)
    tiles = hn_ref.shape[0] // tm
    hns = []
    for j in range(tm // sub):
        rows = pl.ds(j * sub, sub)
        merged = (sa_ref[rows, :] * _dot(og_ref[rows, :], wg_ref[...])
                  + sb_ref[rows, :] * _dot(od_ref[rows, :], wd_ref[...]))
        h = x_ref[rows, :] + _dot(merged.astype(BF16), wo_ref[...])
        h_ref[rows, :] = h
        hn = _rms(h, gf_ref[...]).astype(BF16)
        hn32 = hn.astype(F32)
        for s in range(tiles):
            hn_ref[pl.ds(j * sub * tiles + s, sub, stride=tiles), :] = hn32[:, s * LANES:(s + 1) * LANES]
        hns.append(hn)

    lane = lax.broadcasted_iota(jnp.int32, (sub, LANES), 1)
    row = lax.broadcasted_iota(jnp.int32, (sub, sub), 0)
    col = lax.broadcasted_iota(jnp.int32, (sub, sub), 1)
    strict = jnp.where(row > col, 1.0, 0.0).astype(BF16)
    n_sub = len(hns)
    logits = [jnp.where(lane < N_EXPERTS, _dot(hn, wr_ref[...]) + br_ref[...], -jnp.inf) for hn in hns]
    idx_out = [jnp.zeros((sub, LANES), jnp.int32) for _ in hns]
    val_out = [jnp.zeros((sub, LANES), F32) for _ in hns]
    onehot = [jnp.zeros((sub, LANES), F32) for _ in hns]
    picks = [[] for _ in hns]
    for kk in range(TOP_K):
        for j in range(n_sub):
            mx = jnp.max(logits[j], axis=-1, keepdims=True)
            pick = jnp.min(jnp.where(logits[j] == mx, lane, LANES), axis=-1, keepdims=True)
            hit = lane == pick
            logits[j] = jnp.where(hit, -jnp.inf, logits[j])
            onehot[j] = jnp.where(hit, 1.0, onehot[j])
            idx_out[j] = jnp.where(lane == kk, pick, idx_out[j])
            val_out[j] = jnp.where(lane == kk, mx, val_out[j])
            picks[j].append(hit)
    carry = carry_ref[...]
    for j in range(n_sub):
        rows = pl.ds(j * sub, sub)
        e = jnp.where(lane < TOP_K, jnp.exp(val_out[j] - val_out[j][:, :1]), 0.0)
        gate_ref[rows, :] = e * (1.0 / jnp.sum(e, axis=-1, keepdims=True))
        idx_ref[rows, :] = idx_out[j]
        before = _dot(strict, onehot[j].astype(BF16)) + carry
        rank_out = jnp.zeros((sub, LANES), jnp.int32)
        for kk in range(TOP_K):
            rk = jnp.sum(jnp.where(picks[j][kk], before, 0.0), axis=-1, keepdims=True)
            rank_out = jnp.where(lane == kk, rk.astype(jnp.int32), rank_out)
        rank_ref[rows, :] = rank_out
        carry = carry + jnp.sum(onehot[j], axis=0, keepdims=True)
    carry_ref[...] = carry
    cnt_ref[...] = carry.astype(jnp.int32)


def _merge_route(x, og, od, sa, sb, wg, wd, wo, gf, wr, br, counts_in, n_all, tok0, hn_all=None):
    n, d = x.shape
    tm = _row_tile(n, 512)
    assert tok0 % tm == 0
    b0 = tok0 // tm
    tiles = d // LANES
    row = lambda w: pl.BlockSpec((tm, w), lambda i: (i, 0))
    in_specs = [row(d)] * 5 + [_full(wg.shape), _full(wd.shape), _full(wo.shape), _full(gf.shape),
                               _full(wr.shape), _full(br.shape), _full(counts_in.shape)]
    args = [x, og, od, sa, sb, wg, wd, wo, gf, wr, br, counts_in]
    aliases = {}
    if hn_all is not None:
        in_specs.append(pl.BlockSpec(memory_space=pl.ANY))
        aliases = {len(args): 1}
        args.append(hn_all)
    return pl.pallas_call(
        _merge_route_kernel,
        grid=(n // tm,),
        in_specs=in_specs,
        out_specs=[row(d), pl.BlockSpec((tm * tiles, LANES), lambda i: (b0 + i, 0)),
                   row(LANES), row(LANES), row(LANES), _full((1, LANES))],
        out_shape=[jax.ShapeDtypeStruct((n, d), F32), jax.ShapeDtypeStruct((n_all * tiles, LANES), F32),
                   jax.ShapeDtypeStruct((n, LANES), jnp.int32), jax.ShapeDtypeStruct((n, LANES), F32),
                   jax.ShapeDtypeStruct((n, LANES), jnp.int32), jax.ShapeDtypeStruct((1, LANES), jnp.int32)],
        scratch_shapes=[pltpu.VMEM((1, LANES), F32)],
        input_output_aliases=aliases,
        compiler_params=_cparams(("arbitrary",)),
        name="merge_route",
    )(*args)


def _pair_shuffle(h):
    a, b = h[:, :LANES], h[:, LANES:]
    even = (lax.broadcasted_iota(jnp.int32, (1, LANES), 1) % 2) == 0
    glu = jnp.where(even, a, pltpu.roll(b, 1, 1))
    lin = jnp.where(even, pltpu.roll(a, LANES - 1, 1), b)
    return glu, lin


def _moe_kernel(be_ref, first_ref, nu_ref, xo_ref, x_ref, wu_ref, bu_ref, wd_ref, bd_ref, y_ref,
                xs_ref, wub_ref, wdb_ref, wds_ref):
    del xo_ref
    i = pl.program_id(0)
    bm, d = xs_ref.shape
    de = wdb_ref.shape[0]
    n_tiles = d // LANES
    half = LANES // 2

    @pl.when((i < nu_ref[0]) & (first_ref[i] == 1))
    def _():
        wub_ref[...] = wu_ref[0].astype(BF16)
        for t in range(n_tiles):
            lanes = slice(t * LANES, (t + 1) * LANES)
            for c in range(de // LANES):
                for a in range(2):
                    src = c * LANES + a * half
                    wds_ref[t, pl.ds(c * LANES + a, half, stride=2), :] = wd_ref[0, src:src + half, lanes]
            wdb_ref[:, lanes] = wds_ref[t].astype(BF16)

    @pl.when(i < nu_ref[0])
    def _():
        for s in range(n_tiles):
            xs_ref[:, s * LANES:(s + 1) * LANES] = x_ref[pl.ds(s, bm, stride=SUBLANES), :].astype(BF16)
        h = _dot(xs_ref[...], wub_ref[...]) + bu_ref[0]
        acts = []
        for c in range(h.shape[1] // (2 * LANES)):
            glu, lin = _pair_shuffle(h[:, 2 * c * LANES:2 * (c + 1) * LANES])
            x_glu = jnp.minimum(glu, SWIGLU_LIMIT)
            x_lin = jnp.clip(lin, -SWIGLU_LIMIT, SWIGLU_LIMIT)
            acts.append((x_glu * _sigmoid(SWIGLU_ALPHA * x_glu) * (x_lin + 1.0)).astype(BF16))
        y = _dot(jnp.concatenate(acts, axis=1), wdb_ref[...]) + bd_ref[0]
        for s in range(n_tiles):
            y_ref[pl.ds(s, bm, stride=SUBLANES), :] = y[:, s * LANES:(s + 1) * LANES]

    @pl.when(i >= nu_ref[0])
    def _():
        y_ref[...] = jnp.zeros(y_ref.shape, F32)


def _moe_experts(blk_exp, blk_first, n_used, x_off, xs, wu, bu, wd, bd, d, n_rows):
    tiles = d // LANES
    de2 = wu.shape[2]
    wspec = lambda a, b: pl.BlockSpec((1, a, b), lambda i, be, bf, nu, xo: (be[i], 0, 0))
    rows_in = pl.BlockSpec((pl.Element(MOE_ROWS * tiles), pl.Element(LANES)),
                           lambda i, be, bf, nu, xo: (pl.multiple_of(xo[i], SUBLANES), 0))
    rows = pl.BlockSpec((MOE_ROWS * tiles, LANES), lambda i, be, bf, nu, xo: (i, 0))
    grid_spec = pltpu.PrefetchScalarGridSpec(
        num_scalar_prefetch=4,
        grid=(n_rows // MOE_ROWS,),
        in_specs=[rows_in, wspec(d, de2), wspec(1, de2), wspec(de2 // 2, d), wspec(1, d)],
        out_specs=rows,
        scratch_shapes=[pltpu.VMEM((MOE_ROWS, d), BF16), pltpu.VMEM((d, de2), BF16),
                        pltpu.VMEM((de2 // 2, d), BF16), pltpu.VMEM((d // LANES, de2 // 2, LANES), F32)],
    )
    return pl.pallas_call(
        _moe_kernel,
        grid_spec=grid_spec,
        out_shape=jax.ShapeDtypeStruct((n_rows * tiles, LANES), F32),
        compiler_params=pltpu.CompilerParams(dimension_semantics=("arbitrary",), vmem_limit_bytes=BIG_VMEM_LIMIT),
        name="moe_experts",
    )(blk_exp, blk_first, n_used, x_off, xs, wu, bu, wd, bd)


def _combine_kernel(h_ref, gate_ref, y0_ref, y1_ref, y2_ref, y3_ref, gf_ref, o_ref):
    tm, d = h_ref.shape
    gate = gate_ref[...]
    parts = []
    for s in range(d // LANES):
        acc = h_ref[:, s * LANES:(s + 1) * LANES]
        for kk, y_ref in enumerate((y0_ref, y1_ref, y2_ref, y3_ref)):
            acc = acc + gate[:, kk:kk + 1] * y_ref[0, pl.ds(s, tm, stride=SUBLANES), :]
        parts.append(acc)
    o_ref[...] = _rms(jnp.concatenate(parts, axis=1), gf_ref[...])


def _combine(h, gate, yg, gf, tok0):
    n, d = h.shape
    tm = _row_tile(n, 256)
    assert tok0 % tm == 0
    b0 = tok0 // tm
    tiles = d // LANES
    row = lambda w: pl.BlockSpec((tm, w), lambda i: (i, 0))
    ysp = lambda kk: pl.BlockSpec((1, tm * tiles, LANES), lambda i: (kk, b0 + i, 0))
    return pl.pallas_call(
        _combine_kernel,
        grid=(n // tm,),
        in_specs=[row(d), row(LANES)] + [ysp(kk) for kk in range(TOP_K)] + [_full(gf.shape)],
        out_specs=row(d),
        out_shape=jax.ShapeDtypeStruct((n, d), F32),
        compiler_params=_cparams(("parallel",)),
        name="combine",
    )(h, gate, yg, yg, yg, yg, gf)


def _rope_tables(pos):
    half = DIFF_DH // 2
    inv_freq = ROPE_THETA ** (-jnp.arange(half, dtype=F32) / half)
    ang = pos.astype(F32)[:, None] * inv_freq[None, :]
    cos, sin = jnp.cos(ang), jnp.sin(ang)
    reps = LANES // DIFF_DH
    return jnp.tile(jnp.concatenate([cos, cos], axis=1), (1, reps)), jnp.tile(jnp.concatenate([-sin, sin], axis=1), (1, reps))


def kernel(x_prompt, x_sample, cache_k, cache_v, state_gla, page_table, norm_attn, w_in, gla_gate_up, gla_gate_bias, gla_norm, lam_q1, lam_k1, lam_q2, lam_k2, diff_subln, w_branch_gla, w_branch_diff, w_out, norm_ffn, w_router, b_router, w_up, b_up, w_down, b_down, norm_final):
    depth = w_in.shape[0]
    assert depth == 1, "single-layer step"
    layer = 0
    b_p, t_p, d = x_prompt.shape
    b_s, t_s, _ = x_sample.shape
    n_p, n_s = b_p * t_p, b_s * t_s
    n = n_p + n_s
    n_pool, page = cache_k.shape[1], cache_k.shape[2]
    n_pages = page_table.shape[1]
    past_len = n_pages * page
    kd, vd = GLA_HEADS * GLA_DK, GLA_HEADS * GLA_DV
    tiles = d // LANES
    row2 = lambda a: a.reshape(1, -1)
    assert t_s <= LANES

    splits = (kd, kd, vd, GLA_RANK, vd, d, d, d, d, d)
    offs = [0]
    for s in splits:
        offs.append(offs[-1] + s)
    wb = w_in.reshape(d, -1).astype(BF16)
    w_gq, w_gk, w_gv, w_glr, w_gr, w_dq, w_dk, w_dv, w_ga, w_gb = [wb[:, offs[j]:offs[j + 1]] for j in range(10)]
    w_glr = jnp.pad(w_glr, ((0, 0), (0, LANES - GLA_RANK)))
    gup = jnp.pad(gla_gate_up.reshape(GLA_RANK, kd).astype(BF16), ((0, LANES - GLA_RANK), (0, 0)))
    g_attn = row2(norm_attn)
    gn = row2(gla_norm)
    lam_init = 0.8 - 0.6 * math.exp(-0.3 * layer)
    lams = (row2(lam_q1), row2(lam_k1), row2(lam_q2), row2(lam_k2))
    subln = row2(diff_subln)
    w_bg = w_branch_gla.reshape(vd, d).astype(BF16)
    w_bd = w_branch_diff.reshape(d, d).astype(BF16)
    w_o = w_out.reshape(d, d).astype(BF16)
    w_r = jnp.pad(w_router.reshape(d, N_EXPERTS).astype(BF16), ((0, 0), (0, LANES - N_EXPERTS)))
    b_r = jnp.pad(row2(b_router), ((0, 0), (0, LANES - N_EXPERTS)))
    g_ffn = row2(norm_ffn)

    def mixers(x, pos, seq):
        gq, gk, gv, gla_la, gr = _gla_proj(x, g_attn, w_gq, w_gk, w_gv, w_glr, w_gr, gup, row2(gla_gate_bias))
        cos, sin_signed = _rope_tables(pos)
        return (gq, gk, gv, gla_la, gr) + tuple(
            _diff_proj(x, g_attn, cos, sin_signed, w_dq, w_dk, w_dv, w_ga, w_gb, seq))

    x_p = x_prompt.reshape(n_p, d)
    gq, gk, gv, gla_la, gr, dq, dk_b, dv_b, kt_p, v_cache_p, sig_a, sig_b = mixers(
        x_p, jnp.arange(t_p, dtype=jnp.int32), t_p)
    seqs = lambda a: a.reshape(b_p, t_p, -1)
    og_p, s_p = _gla_chunks(seqs(gq), seqs(gk), seqs(gv), seqs(gla_la), seqs(gr), gn, None,
                            GLA_CHUNK if t_p % GLA_CHUNK == 0 else t_p, _largest_divisor(b_p, GLA_PROMPT_SEQS))
    og_p = og_p.reshape(n_p, vd)
    od_p = _diff_prompt(dq, dk_b, dv_b, lams, subln, b_p, t_p, lam_init)
    zero_counts = jnp.zeros((1, LANES), jnp.int32)
    h_p, hn_p, idx_p, gate_p, rank_p, counts_p = _merge_route(
        x_p, og_p, od_p, sig_a, sig_b, w_bg, w_bd, w_o, g_ffn, w_r, b_r, zero_counts, n, 0)

    x_s = x_sample.reshape(n_s, d)
    gq, gk, gv, gla_la, gr, dq, dk_b, dv_b, kt_s, v_cache_s, sig_a, sig_b = mixers(
        x_s, jnp.tile(past_len + jnp.arange(t_s, dtype=jnp.int32), b_s), n_s)
    t_pad = -(-t_s // GLA_SAMPLE_CHUNK) * GLA_SAMPLE_CHUNK
    pad_s = lambda a: jnp.pad(a.reshape(b_s, t_s, -1), ((0, 0), (0, t_pad - t_s), (0, 0)))
    og_s, s_s = _gla_chunks(pad_s(gq), pad_s(gk), pad_s(gv), pad_s(gla_la), pad_s(gr), gn,
                            state_gla.reshape(b_s, GLA_HEADS, GLA_DK, GLA_DV), t_pad,
                            _largest_divisor(b_s, GLA_SAMPLE_SEQS))
    og_s = og_s[:, :t_s].reshape(n_s, vd)
    q4 = dq.reshape(b_s, t_s, DIFF_QK_HEADS, DIFF_DH).transpose(0, 2, 1, 3)
    q_bd = (q4[:, :, :, None, :] * jnp.eye(DIFF_QK_HEADS, dtype=BF16)[None, :, None, :, None]).reshape(
        b_s, DIFF_QK_HEADS * t_s, d)
    kn = jnp.pad(dk_b.reshape(b_s, t_s, d).transpose(0, 2, 1), ((0, 0), (0, 0), (0, LANES - t_s)))
    vn = jnp.pad(dv_b.reshape(b_s, t_s, DIFF_HEADS, LANES).transpose(0, 2, 1, 3),
                 ((0, 0), (0, 0), (0, LANES - t_s), (0, 0)))
    cache_kt = jnp.transpose(cache_k, (0, 1, 3, 4, 2)).reshape(depth * n_pool, DIFF_QK_HEADS * DIFF_DH, page)
    od_s = _diff_sample(page_table, q_bd, kn, vn, lams, subln, cache_kt,
                        cache_v.reshape(depth * n_pool, page * DIFF_HEADS, LANES), lam_init, t_s)
    od_s = od_s.reshape(b_s, DIFF_HEADS, 2, t_s, LANES)[:, :, 0].transpose(0, 2, 1, 3).reshape(n_s, d).astype(BF16)
    h_s, hn_all, idx_s, gate_s, rank_s, counts = _merge_route(
        x_s, og_s, od_s, sig_a, sig_b, w_bg, w_bd, w_o, g_ffn, w_r, b_r, counts_p, n, n_p, hn_p)

    counts = counts[0, :N_EXPERTS]
    padded = (counts + MOE_ROWS - 1) // MOE_ROWS * MOE_ROWS
    pad_end = jnp.cumsum(padded)
    pad_start = pad_end - padded
    idx4 = jnp.concatenate([idx_p[:, :TOP_K], idx_s[:, :TOP_K]], axis=0)
    rank4 = jnp.concatenate([rank_p[:, :TOP_K], rank_s[:, :TOP_K]], axis=0)
    experts = jnp.arange(N_EXPERTS, dtype=jnp.int32)

    def lookup(table, e):
        return jnp.sum(jnp.where(e[..., None] == experts, table, 0), axis=-1)

    dest = lookup(pad_start, idx4) + rank4
    n_rows = -(-(n * TOP_K + N_EXPERTS * (MOE_ROWS - 1)) // MOE_ROWS) * MOE_ROWS
    n_blocks = n_rows // MOE_ROWS
    blk_start = jnp.arange(n_blocks, dtype=jnp.int32) * MOE_ROWS
    blk_exp = jnp.minimum(jnp.sum((blk_start[:, None] >= pad_end[None, :]).astype(jnp.int32), axis=1), N_EXPERTS - 1)
    n_used = (pad_end[-1:] // MOE_ROWS).astype(jnp.int32)
    start = jnp.cumsum(counts) - counts
    _, sorted_tok = lax.sort(((lookup(start, idx4) + rank4).reshape(-1),
                              jnp.repeat(jnp.arange(n, dtype=jnp.int32), TOP_K)), num_keys=1, is_stable=False)
    src_tok = jnp.concatenate([sorted_tok, jnp.arange(MOE_ROWS, dtype=jnp.int32) % n])
    xs = jnp.take(hn_all.reshape(n, tiles, LANES), src_tok, axis=0, mode='clip').reshape(-1, LANES)
    blk_in_exp = blk_start - lookup(pad_start, blk_exp)
    x_off = jnp.clip(lookup(start, blk_exp) + blk_in_exp, 0, n * TOP_K).astype(jnp.int32) * tiles

    de = w_down.shape[2]
    blk_first = jnp.concatenate([jnp.ones((1,), jnp.int32), (blk_exp[1:] != blk_exp[:-1]).astype(jnp.int32)])
    y = _moe_experts(blk_exp, blk_first, n_used, x_off, xs, w_up.reshape(N_EXPERTS, d, 2 * de),
                     b_up.reshape(N_EXPERTS, 1, 2 * de), w_down.reshape(N_EXPERTS, de, d),
                     b_down.reshape(N_EXPERTS, 1, d), d, n_rows)
    yg = jnp.take(y.reshape(n_rows, tiles, LANES), dest.T.reshape(-1), axis=0, mode='clip').reshape(TOP_K, n * tiles, LANES)

    g_fin = row2(norm_final)
    y_prompt = _combine(h_p, gate_p, yg, g_fin, 0).reshape(b_p, t_p, d)
    y_sample = _combine(h_s, gate_s, yg, g_fin, n_p).reshape(b_s, t_s, d)
    k_prompt = kt_p.reshape(1, b_p, DIFF_QK_HEADS, DIFF_DH, t_p).transpose(0, 1, 4, 2, 3)
    v_prompt = v_cache_p.reshape(1, b_p, t_p, DIFF_HEADS, 2 * DIFF_DH)
    k_sample = kt_s.reshape(1, DIFF_QK_HEADS, DIFF_DH, b_s, t_s).transpose(0, 3, 4, 1, 2)
    v_sample = v_cache_s.reshape(1, b_s, t_s, DIFF_HEADS, 2 * DIFF_DH)
    return (y_prompt, y_sample, k_prompt, v_prompt, s_p[None], k_sample, v_sample, s_s[None])
```
